```python
import math
import jax, jax.numpy as jnp
from jax import lax
import numpy as np

D_MODEL = 2048
BATCH = 2
SEQ = 8192
DEPTH = 2

D_MIX = D_MODEL
SSM_WIDTH = D_MIX // 2
SSM_GROUP = 16
SSM_GROUPS = SSM_WIDTH // SSM_GROUP
SSM_STATE = 64
SSM_CHUNK = 128
DIFF_WIDTH = D_MIX // 4
DIFF_HEADS = 4
DIFF_HEAD_DIM = DIFF_WIDTH // (2 * DIFF_HEADS)
DIFF_V_DIM = 2 * DIFF_HEAD_DIM
XATTN_WIDTH = D_MIX - SSM_WIDTH - DIFF_WIDTH
XATTN_HEADS = 4
XATTN_HEAD_DIM = XATTN_WIDTH // XATTN_HEADS
MEM_TOKENS = 256
ROPE_THETA = 10000.0
Q_BLOCK = 128
NORM_EPS = 1e-6
MASK_VALUE = -1e30

kernel_name = "hybrid_s5_diffattn_memxattn_block"


def rmsnorm(x, g):
    x32 = x.astype(jnp.float32)
    var = jnp.mean(x32 * x32, axis=-1, keepdims=True)
    return (x32 * lax.rsqrt(var + NORM_EPS) * g.astype(jnp.float32)).astype(x.dtype)


def rope_tables(positions, dim):
    inv = ROPE_THETA ** (-jnp.arange(0, dim, 2, dtype=jnp.float32) / dim)
    ang = positions.astype(jnp.float32)[..., None] * inv
    return jnp.cos(ang), jnp.sin(ang)


def apply_rope(x, cos, sin):
    extra = x.ndim - 3
    shp = cos.shape[:2] + (1,) * extra + cos.shape[-1:]
    c, s = cos.reshape(shp), sin.reshape(shp)
    x1, x2 = jnp.split(x.astype(jnp.float32), 2, axis=-1)
    return jnp.concatenate([x1 * c - x2 * s, x2 * c + x1 * s], axis=-1).astype(x.dtype)


def s5_scan(u, a_re, a_im, log_dt, b_re, b_im, c_re, c_im, d_skip):
    f32 = jnp.float32
    bsz, seq = u.shape[0], u.shape[1]
    n_chunks = seq // SSM_CHUNK
    dt = jnp.exp(log_dt.astype(f32))[:, None]
    lr, li = a_re.astype(f32), a_im.astype(f32)
    mag = jnp.exp(lr * dt)
    abar_re, abar_im = mag * jnp.cos(li * dt), mag * jnp.sin(li * dt)
    den = lr * lr + li * li
    nr, ni = abar_re - 1.0, abar_im
    z_re = (nr * lr + ni * li) / den
    z_im = (ni * lr - nr * li) / den
    br, bi = b_re.astype(f32), b_im.astype(f32)
    bbar_re = z_re[..., None] * br - z_im[..., None] * bi
    bbar_im = z_re[..., None] * bi + z_im[..., None] * br
    cr, ci = c_re.astype(f32), c_im.astype(f32)

    uc = u.astype(f32).reshape(bsz, n_chunks, SSM_CHUNK, SSM_GROUPS, SSM_GROUP)
    uc = uc.transpose(1, 2, 0, 3, 4)
    a_el_re = jnp.broadcast_to(abar_re[None, None], (SSM_CHUNK, 1, SSM_GROUPS, SSM_STATE))
    a_el_im = jnp.broadcast_to(abar_im[None, None], (SSM_CHUNK, 1, SSM_GROUPS, SSM_STATE))

    def combine(e1, e2):
        a1r, a1i, b1r, b1i = e1
        a2r, a2i, b2r, b2i = e2
        return (a2r * a1r - a2i * a1i,
                a2r * a1i + a2i * a1r,
                a2r * b1r - a2i * b1i + b2r,
                a2r * b1i + a2i * b1r + b2i)

    def chunk_step(carry, u_t):
        h_re, h_im = carry
        bu_re = jnp.einsum('tbgh,gph->tbgp', u_t, bbar_re)
        bu_im = jnp.einsum('tbgh,gph->tbgp', u_t, bbar_im)
        pa_re, pa_im, s_re, s_im = lax.associative_scan(
            combine, (a_el_re, a_el_im, bu_re, bu_im), axis=0)
        x_re = s_re + pa_re * h_re - pa_im * h_im
        x_im = s_im + pa_re * h_im + pa_im * h_re
        y = jnp.einsum('tbgp,ghp->tbgh', x_re, cr) - jnp.einsum('tbgp,ghp->tbgh', x_im, ci)
        return (x_re[-1], x_im[-1]), y

    h0 = (jnp.zeros((bsz, SSM_GROUPS, SSM_STATE), f32), jnp.zeros((bsz, SSM_GROUPS, SSM_STATE), f32))
    _, ys = lax.scan(chunk_step, h0, uc)
    y = ys.transpose(2, 0, 1, 3, 4).reshape(bsz, seq, SSM_WIDTH)
    y = y + d_skip.astype(f32).reshape(SSM_WIDTH) * u.astype(f32)
    return y.astype(u.dtype)


def diff_attention(q, k, v, lam, lambda_init, subln_g):
    bsz, seq = q.shape[0], q.shape[1]
    n_blocks = seq // Q_BLOCK
    scale = DIFF_HEAD_DIM ** -0.5
    k_pos = jnp.arange(seq)
    qb = q.reshape(bsz, n_blocks, Q_BLOCK, DIFF_HEADS, 2, DIFF_HEAD_DIM).transpose(1, 0, 2, 3, 4, 5)

    def block(args):
        qi, blk = args
        s = jnp.einsum('bqhcd,bkhcd->bhcqk', qi, k).astype(jnp.float32) * scale
        q_pos = blk * Q_BLOCK + jnp.arange(Q_BLOCK)
        mask = k_pos[None, :] <= q_pos[:, None]
        s = jnp.where(mask, s, MASK_VALUE)
        p = jax.nn.softmax(s, axis=-1)
        w = p[:, :, 0] - lam * p[:, :, 1]
        return jnp.einsum('bhqk,bkhd->bqhd', w.astype(v.dtype), v)

    o = lax.map(block, (qb, jnp.arange(n_blocks)))
    o = o.transpose(1, 0, 2, 3, 4).reshape(bsz, seq, DIFF_HEADS, DIFF_V_DIM)
    o = rmsnorm(o, subln_g) * (1.0 - lambda_init)
    return o.reshape(bsz, seq, DIFF_WIDTH)


def memory_attention(q, mem_n, w_mem_kv):
    bsz, seq = q.shape[0], q.shape[1]
    kv = mem_n @ w_mem_kv
    mk, mv = jnp.split(kv, 2, axis=-1)
    mk = mk.reshape(bsz, MEM_TOKENS, XATTN_HEADS, XATTN_HEAD_DIM)
    mv = mv.reshape(bsz, MEM_TOKENS, XATTN_HEADS, XATTN_HEAD_DIM)
    s = jnp.einsum('bqhd,bmhd->bhqm', q, mk).astype(jnp.float32) * (XATTN_HEAD_DIM ** -0.5)
    p = jax.nn.softmax(s, axis=-1).astype(mv.dtype)
    o = jnp.einsum('bhqm,bmhd->bqhd', p, mv)
    return o.reshape(bsz, seq, XATTN_WIDTH)


def setup_inputs(seed: int = 0) -> dict:
    key = jax.random.key(seed)
    ks = jax.random.split(key, 24)
    f32 = jnp.float32
    in_cols = 2 * SSM_WIDTH + 4 * DIFF_WIDTH + 2 * XATTN_WIDTH
    nrm = lambda k, shp, s: jax.random.normal(k, shp, f32) * s
    x = jax.random.normal(ks[0], (BATCH, SEQ, D_MODEL), f32)
    mem = jax.random.normal(ks[1], (BATCH, MEM_TOKENS, D_MODEL), f32)
    positions = jnp.broadcast_to(jnp.arange(SEQ, dtype=jnp.int32), (BATCH, SEQ))
    norm_pre = 1.0 + nrm(ks[2], (DEPTH, D_MODEL), 0.02)
    norm_post = 1.0 + nrm(ks[3], (DEPTH, D_MODEL), 0.02)
    norm_mem = 1.0 + nrm(ks[4], (DEPTH, D_MODEL), 0.02)
    w_in = nrm(ks[5], (DEPTH, D_MODEL, in_cols), D_MODEL ** -0.5)
    w_out = nrm(ks[6], (DEPTH, D_MIX, D_MODEL), D_MIX ** -0.5)
    w_mem_kv = nrm(ks[7], (DEPTH, D_MODEL, 2 * XATTN_WIDTH), D_MODEL ** -0.5)
    gp = (DEPTH, SSM_GROUPS, SSM_STATE)
    ssm_a_re = -0.5 + nrm(ks[8], gp, 0.01)
    ssm_a_im = jnp.pi * jnp.arange(SSM_STATE, dtype=f32)[None, None, :] + nrm(ks[9], gp, 0.01)
    ssm_log_dt = jax.random.uniform(ks[10], (DEPTH, SSM_GROUPS), f32, math.log(1e-3), math.log(1e-1))
    ssm_b_re = nrm(ks[11], (DEPTH, SSM_GROUPS, SSM_STATE, SSM_GROUP), (2 * SSM_GROUP) ** -0.5)
    ssm_b_im = nrm(ks[12], (DEPTH, SSM_GROUPS, SSM_STATE, SSM_GROUP), (2 * SSM_GROUP) ** -0.5)
    ssm_c_re = nrm(ks[13], (DEPTH, SSM_GROUPS, SSM_GROUP, SSM_STATE), (2 * SSM_STATE) ** -0.5)
    ssm_c_im = nrm(ks[14], (DEPTH, SSM_GROUPS, SSM_GROUP, SSM_STATE), (2 * SSM_STATE) ** -0.5)
    ssm_d = nrm(ks[15], (DEPTH, SSM_GROUPS, SSM_GROUP), 1.0)
    w_glu = nrm(ks[16], (DEPTH, SSM_WIDTH, SSM_WIDTH), SSM_WIDTH ** -0.5)
    b_glu = nrm(ks[17], (DEPTH, SSM_WIDTH), 0.01)
    diff_lq1 = nrm(ks[18], (DEPTH, DIFF_HEAD_DIM), 0.1)
    diff_lk1 = nrm(ks[19], (DEPTH, DIFF_HEAD_DIM), 0.1)
    diff_lq2 = nrm(ks[20], (DEPTH, DIFF_HEAD_DIM), 0.1)
    diff_lk2 = nrm(ks[21], (DEPTH, DIFF_HEAD_DIM), 0.1)
    diff_subln = 1.0 + nrm(ks[22], (DEPTH, DIFF_V_DIM), 0.02)
    return {"x": x, "mem": mem, "positions": positions,
            "norm_pre": norm_pre, "norm_post": norm_post, "norm_mem": norm_mem,
            "w_in": w_in, "w_out": w_out, "w_mem_kv": w_mem_kv,
            "ssm_a_re": ssm_a_re, "ssm_a_im": ssm_a_im, "ssm_log_dt": ssm_log_dt,
            "ssm_b_re": ssm_b_re, "ssm_b_im": ssm_b_im, "ssm_c_re": ssm_c_re, "ssm_c_im": ssm_c_im,
            "ssm_d": ssm_d, "w_glu": w_glu, "b_glu": b_glu,
            "diff_lq1": diff_lq1, "diff_lk1": diff_lk1, "diff_lq2": diff_lq2, "diff_lk2": diff_lk2,
            "diff_subln": diff_subln}


def reference(x, mem, positions, norm_pre, norm_post, norm_mem, w_in, w_out, w_mem_kv,
              ssm_a_re, ssm_a_im, ssm_log_dt, ssm_b_re, ssm_b_im, ssm_c_re, ssm_c_im,
              ssm_d, w_glu, b_glu, diff_lq1, diff_lk1, diff_lq2, diff_lk2, diff_subln):
    bsz, seq = x.shape[0], x.shape[1]
    cos, sin = rope_tables(positions, DIFF_HEAD_DIM)
    widths = [SSM_WIDTH, SSM_WIDTH, DIFF_WIDTH, DIFF_WIDTH, DIFF_WIDTH, DIFF_WIDTH,
              XATTN_WIDTH, XATTN_WIDTH]
    split_at = [sum(widths[:i + 1]) for i in range(len(widths) - 1)]
    for l in range(DEPTH):
        lambda_init = 0.8 - 0.6 * math.exp(-0.3 * l)
        h = rmsnorm(x, norm_pre[l])
        proj = h @ w_in[l]
        u_s, g_s, q_d, k_d, v_d, g_d, q_x, g_x = jnp.split(proj, split_at, axis=-1)

        y_s = s5_scan(u_s, ssm_a_re[l], ssm_a_im[l], ssm_log_dt[l], ssm_b_re[l], ssm_b_im[l],
                      ssm_c_re[l], ssm_c_im[l], ssm_d[l])
        y_s = jax.nn.gelu(y_s)
        y_s = y_s * jax.nn.sigmoid(y_s @ w_glu[l] + b_glu[l])
        y_s = y_s * jax.nn.silu(g_s)

        q = apply_rope(q_d.reshape(bsz, seq, DIFF_HEADS, 2, DIFF_HEAD_DIM), cos, sin)
        k = apply_rope(k_d.reshape(bsz, seq, DIFF_HEADS, 2, DIFF_HEAD_DIM), cos, sin)
        v = v_d.reshape(bsz, seq, DIFF_HEADS, DIFF_V_DIM)
        lam = (jnp.exp(jnp.sum(diff_lq1[l].astype(jnp.float32) * diff_lk1[l].astype(jnp.float32)))
               - jnp.exp(jnp.sum(diff_lq2[l].astype(jnp.float32) * diff_lk2[l].astype(jnp.float32)))
               + lambda_init)
        y_d = diff_attention(q, k, v, lam, lambda_init, diff_subln[l]) * jax.nn.silu(g_d)

        mem_n = rmsnorm(mem, norm_mem[l])
        y_x = memory_attention(q_x.reshape(bsz, seq, XATTN_HEADS, XATTN_HEAD_DIM), mem_n, w_mem_kv[l])
        y_x = y_x * jax.nn.silu(g_x)

        mix = jnp.concatenate([y_s, y_d, y_x], axis=-1) @ w_out[l]
        x = x + rmsnorm(mix, norm_post[l])
    return x
```

```python
import functools
import math

import jax
import jax.numpy as jnp
import numpy as np
from jax import lax
from jax.experimental import pallas as pl
from jax.experimental.pallas import tpu as pltpu

F32 = jnp.float32
BF16 = jnp.bfloat16

SSM_GROUP = 16
SSM_STATE = 64
CHUNK = 16
DIFF_HEADS = 4
DIFF_HEAD_DIM = 64
XATTN_HEADS = 4
XATTN_HEAD_DIM = 128
ROPE_THETA = 10000.0
NORM_EPS = 1e-6
MASK_VALUE = -1e30
LANES = 128
SUBLANES = 8
VMEM_LIMIT = 56 * 1024 * 1024


def _cparams(*sem):
    return pltpu.CompilerParams(dimension_semantics=sem, vmem_limit_bytes=VMEM_LIMIT)


def _norm_matmul_kernel(x_ref, g_ref, w_ref, o_ref, h_ref):
    @pl.when(pl.program_id(1) == 0)
    def _():
        x = x_ref[...]
        var = jnp.mean(x * x, axis=-1, keepdims=True)
        h_ref[...] = (x * lax.rsqrt(var + NORM_EPS) * g_ref[...]).astype(BF16)

    o_ref[...] = jnp.dot(h_ref[...], w_ref[...], preferred_element_type=F32).astype(o_ref.dtype)


def _norm_matmul(x, g, w, *, tm, tn, name):
    m, d = x.shape
    n = w.shape[1]
    tm, tn = min(tm, m), min(tn, n)
    return pl.pallas_call(
        _norm_matmul_kernel,
        grid=(m // tm, n // tn),
        in_specs=[pl.BlockSpec((tm, d), lambda i, j: (i, 0)),
                  pl.BlockSpec((1, d), lambda i, j: (0, 0)),
                  pl.BlockSpec((d, tn), lambda i, j: (0, j))],
        out_specs=pl.BlockSpec((tm, tn), lambda i, j: (i, j)),
        out_shape=jax.ShapeDtypeStruct((m, n), BF16),
        scratch_shapes=[pltpu.VMEM((tm, d), BF16)],
        compiler_params=_cparams("parallel", "arbitrary"),
        name=name,
    )(x, g.reshape(1, d), w)


def _rope_kernel(x_ref, cos_ref, sin_ref, o_ref, *, scale):
    j = pl.program_id(1)
    mult = jnp.where(j == 0, scale, 1.0).astype(F32)
    cos = cos_ref[...]
    sin = sin_ref[...]
    lane = lax.broadcasted_iota(jnp.int32, cos.shape, 1)
    first_half = (lane % DIFF_HEAD_DIM) < (DIFF_HEAD_DIM // 2)
    for h in range(DIFF_HEADS):
        x = x_ref[:, h * LANES:(h + 1) * LANES].astype(F32)
        partner = jnp.where(first_half,
                            pltpu.roll(x, LANES - DIFF_HEAD_DIM // 2, axis=1),
                            pltpu.roll(x, DIFF_HEAD_DIM // 2, axis=1))
        o_ref[0, :, h * LANES:(h + 1) * LANES] = ((x * cos + partner * sin) * mult).astype(o_ref.dtype)


def _rope(proj, cos_t, sin_t, *, q_col_block, tm):
    m = proj.shape[0]
    tm = min(tm, m)
    w = DIFF_HEADS * LANES
    return pl.pallas_call(
        functools.partial(_rope_kernel, scale=DIFF_HEAD_DIM ** -0.5),
        grid=(m // tm, 2),
        in_specs=[pl.BlockSpec((tm, w), lambda i, j: (i, q_col_block + j)),
                  pl.BlockSpec((tm, LANES), lambda i, j: (i, 0)),
                  pl.BlockSpec((tm, LANES), lambda i, j: (i, 0))],
        out_specs=pl.BlockSpec((1, tm, w), lambda i, j: (j, i, 0)),
        out_shape=jax.ShapeDtypeStruct((2, m, w), BF16),
        compiler_params=_cparams("parallel", "arbitrary"),
        name="rope",
    )(proj, cos_t, sin_t)


def _diff_attn_kernel(qi_ref, ki_ref, q_ref, k_ref, v_ref, g_ref, lam_ref, sg_ref, o_ref,
                      m1, l1, a1, m2, l2, a2):
    t = pl.program_id(2)
    qi = qi_ref[t]
    ki = ki_ref[t]

    @pl.when(ki == 0)
    def _():
        m1[...] = jnp.full_like(m1, MASK_VALUE)
        m2[...] = jnp.full_like(m2, MASK_VALUE)
        l1[...] = jnp.zeros_like(l1)
        l2[...] = jnp.zeros_like(l2)
        a1[...] = jnp.zeros_like(a1)
        a2[...] = jnp.zeros_like(a2)

    def scores():
        q = q_ref[0]
        k = k_ref[0]
        lane = lax.broadcasted_iota(jnp.int32, q.shape, 1)
        q1 = jnp.where(lane < DIFF_HEAD_DIM, q, jnp.zeros_like(q))
        q2 = jnp.where(lane >= DIFF_HEAD_DIM, q, jnp.zeros_like(q))
        dims = (((1,), (1,)), ((), ()))
        return (lax.dot_general(q1, k, dims, preferred_element_type=F32),
                lax.dot_general(q2, k, dims, preferred_element_type=F32))

    def update(s, m_ref, l_ref, a_ref, v):
        m_old = m_ref[...]
        m_new = jnp.maximum(m_old, jnp.max(s, axis=-1, keepdims=True))
        alpha = jnp.exp(m_old - m_new)
        p = jnp.exp(s - m_new)
        l_ref[...] = alpha * l_ref[...] + jnp.sum(p, axis=-1, keepdims=True)
        a_ref[...] = alpha * a_ref[...] + jnp.dot(p.astype(BF16), v, preferred_element_type=F32)
        m_ref[...] = m_new

    @pl.when(ki < qi)
    def _():
        s1, s2 = scores()
        v = v_ref[...]
        update(s1, m1, l1, a1, v)
        update(s2, m2, l2, a2, v)

    @pl.when(ki == qi)
    def _():
        s1, s2 = scores()
        row = lax.broadcasted_iota(jnp.int32, s1.shape, 0)
        col = lax.broadcasted_iota(jnp.int32, s1.shape, 1)
        keep = col <= row
        v = v_ref[...]
        update(jnp.where(keep, s1, MASK_VALUE), m1, l1, a1, v)
        update(jnp.where(keep, s2, MASK_VALUE), m2, l2, a2, v)
        o = a1[...] / l1[...] - lam_ref[...] * (a2[...] / l2[...])
        var = jnp.mean(o * o, axis=-1, keepdims=True)
        o = o * lax.rsqrt(var + NORM_EPS) * sg_ref[...]
        o_ref[...] = (o * jax.nn.silu(g_ref[...].astype(F32))).astype(o_ref.dtype)


def _diff_attn(qk, proj, lam_vec, sg_vec, *, bsz, seq, v_col, g_col, tq):
    tq = min(tq, seq)
    nq = seq // tq
    pairs = [(qi, ki) for qi in range(nq) for ki in range(qi + 1)]
    qi_tbl = jnp.asarray([p[0] for p in pairs], jnp.int32)
    ki_tbl = jnp.asarray([p[1] for p in pairs], jnp.int32)
    m = bsz * seq
    grid_spec = pltpu.PrefetchScalarGridSpec(
        num_scalar_prefetch=2,
        grid=(bsz, DIFF_HEADS, len(pairs)),
        in_specs=[
            pl.BlockSpec((1, tq, LANES), lambda b, h, t, qt, kt: (0, b * nq + qt[t], h)),
            pl.BlockSpec((1, tq, LANES), lambda b, h, t, qt, kt: (1, b * nq + kt[t], h)),
            pl.BlockSpec((tq, LANES), lambda b, h, t, qt, kt: (b * nq + kt[t], v_col + h)),
            pl.BlockSpec((tq, LANES), lambda b, h, t, qt, kt: (b * nq + qt[t], g_col + h)),
            pl.BlockSpec((1, LANES), lambda b, h, t, qt, kt: (0, 0)),
            pl.BlockSpec((1, LANES), lambda b, h, t, qt, kt: (0, 0)),
        ],
        out_specs=pl.BlockSpec((tq, LANES), lambda b, h, t, qt, kt: (b * nq + qt[t], h)),
        scratch_shapes=[pltpu.VMEM((tq, 1), F32), pltpu.VMEM((tq, 1), F32), pltpu.VMEM((tq, LANES), F32),
                        pltpu.VMEM((tq, 1), F32), pltpu.VMEM((tq, 1), F32), pltpu.VMEM((tq, LANES), F32)],
    )
    return pl.pallas_call(
        _diff_attn_kernel,
        grid_spec=grid_spec,
        out_shape=jax.ShapeDtypeStruct((m, DIFF_HEADS * LANES), BF16),
        compiler_params=_cparams("parallel", "parallel", "arbitrary"),
        name="diff_attn",
    )(qi_tbl, ki_tbl, qk, qk, proj, proj, lam_vec, sg_vec)


def _mem_attn_kernel(q_ref, g_ref, mk_ref, mv_ref, o_ref):
    scale = XATTN_HEAD_DIM ** -0.5
    for h in range(XATTN_HEADS):
        sl = slice(h * XATTN_HEAD_DIM, (h + 1) * XATTN_HEAD_DIM)
        s = lax.dot_general(q_ref[:, sl], mk_ref[:, sl], (((1,), (1,)), ((), ())),
                            preferred_element_type=F32) * scale
        p = jnp.exp(s - jnp.max(s, axis=-1, keepdims=True))
        p = p / jnp.sum(p, axis=-1, keepdims=True)
        o = jnp.dot(p.astype(BF16), mv_ref[:, sl], preferred_element_type=F32)
        o_ref[:, sl] = (o * jax.nn.silu(g_ref[:, sl].astype(F32))).astype(o_ref.dtype)


def _mem_attn(proj, mem_kv, *, bsz, seq, mem_tokens, q_col_block, g_col_block, tq):
    tq = min(tq, seq)
    nq = seq // tq
    w = XATTN_HEADS * XATTN_HEAD_DIM
    return pl.pallas_call(
        _mem_attn_kernel,
        grid=(bsz, nq),
        in_specs=[pl.BlockSpec((tq, w), lambda b, i: (b * nq + i, q_col_block)),
                  pl.BlockSpec((tq, w), lambda b, i: (b * nq + i, g_col_block)),
                  pl.BlockSpec((mem_tokens, w), lambda b, i: (b, 0)),
                  pl.BlockSpec((mem_tokens, w), lambda b, i: (b, 1))],
        out_specs=pl.BlockSpec((tq, w), lambda b, i: (b * nq + i, 0)),
        out_shape=jax.ShapeDtypeStruct((bsz * seq, w), BF16),
        compiler_params=_cparams("parallel", "parallel"),
        name="mem_attn",
    )(proj, proj, mem_kv, mem_kv)


def _s5_taps_kernel(lhs_ref, rhs_ref, o_ref):
    o_ref[0] = jnp.dot(lhs_ref[0], rhs_ref[0], preferred_element_type=F32,
                       precision=lax.Precision.HIGHEST)


def _s5_taps(lhs, rhs):
    g, r, k = lhs.shape
    n = rhs.shape[2]
    return pl.pallas_call(
        _s5_taps_kernel,
        grid=(g,),
        in_specs=[pl.BlockSpec((1, r, k), lambda i: (i, 0, 0)),
                  pl.BlockSpec((1, k, n), lambda i: (i, 0, 0))],
        out_specs=pl.BlockSpec((1, r, n), lambda i: (i, 0, 0)),
        out_shape=jax.ShapeDtypeStruct((g, r, n), F32),
        compiler_params=_cparams("parallel"),
        name="s5_taps",
    )(lhs, rhs)


def _s5_weights(a_re, a_im, log_dt, b_re, b_im, c_re, c_im, d_skip):
    g = a_re.shape[0]
    p, h, t = SSM_STATE, SSM_GROUP, CHUNK
    dt = jnp.exp(log_dt.astype(F32))[:, None]
    lr, li = a_re.astype(F32), a_im.astype(F32)
    mag = jnp.exp(lr * dt)
    abar_re, abar_im = mag * jnp.cos(li * dt), mag * jnp.sin(li * dt)
    den = lr * lr + li * li
    nr, ni = abar_re - 1.0, abar_im
    z_re = (nr * lr + ni * li) / den
    z_im = (ni * lr - nr * li) / den
    br, bi = b_re.astype(F32), b_im.astype(F32)
    bbar_re = z_re[..., None] * br - z_im[..., None] * bi
    bbar_im = z_re[..., None] * bi + z_im[..., None] * br
    cr, ci = c_re.astype(F32), c_im.astype(F32)

    tau = jnp.arange(t + 1, dtype=F32)[None, :, None]
    pmag = jnp.exp(tau * (lr * dt)[:, None, :])
    pw_re = pmag * jnp.cos(tau * (li * dt)[:, None, :])
    pw_im = pmag * jnp.sin(tau * (li * dt)[:, None, :])

    ca_re = cr[:, None] * pw_re[:, :, None, :] - ci[:, None] * pw_im[:, :, None, :]
    ca_im = cr[:, None] * pw_im[:, :, None, :] + ci[:, None] * pw_re[:, :, None, :]

    lhs = jnp.concatenate([ca_re[:, :t], ca_im[:, :t]], axis=-1).reshape(g, t * h, 2 * p)
    rhs = jnp.concatenate([bbar_re, -bbar_im], axis=1)
    taps = _s5_taps(lhs, rhs).reshape(g, t, h, h)
    taps = taps.at[:, 0].add(jax.vmap(jnp.diag)(d_skip.astype(F32)))

    idx = jnp.arange(t)
    lag = idx[None, :] - idx[:, None]
    toep = jnp.where((lag >= 0)[None, :, :, None, None], taps[:, jnp.clip(lag, 0, t - 1)], 0.0)
    m_loc = toep.transpose(0, 1, 4, 2, 3).reshape(g, t * h, t * h)

    pj_re, pj_im = pw_re[:, t - 1 - idx], pw_im[:, t - 1 - idx]
    win_re = pj_re[:, :, None, :] * bbar_re.transpose(0, 2, 1)[:, None] \
        - pj_im[:, :, None, :] * bbar_im.transpose(0, 2, 1)[:, None]
    win_im = pj_re[:, :, None, :] * bbar_im.transpose(0, 2, 1)[:, None] \
        + pj_im[:, :, None, :] * bbar_re.transpose(0, 2, 1)[:, None]
    win_re = win_re.reshape(g, t * h, p)
    win_im = win_im.reshape(g, t * h, p)
    slot = (jnp.arange(g) % 2)[:, None, None]
    zero = jnp.zeros_like(win_re)
    w_in = jnp.concatenate([jnp.where(slot == 0, win_re, zero), jnp.where(slot == 1, win_re, zero),
                            jnp.where(slot == 0, win_im, zero), jnp.where(slot == 1, win_im, zero)], axis=-1)

    wo_re = ca_re[:, 1:].transpose(0, 3, 1, 2).reshape(g, p, t * h)
    wo_im = -ca_im[:, 1:].transpose(0, 3, 1, 2).reshape(g, p, t * h)
    zero = jnp.zeros_like(wo_re)
    w_out_re = jnp.concatenate([jnp.where(slot == 0, wo_re, zero), jnp.where(slot == 1, wo_re, zero)], axis=1)
    w_out_im = jnp.concatenate([jnp.where(slot == 0, wo_im, zero), jnp.where(slot == 1, wo_im, zero)], axis=1)

    kk = (t * jnp.arange(2 * SUBLANES, dtype=F32))[:, None, None]
    cmag = jnp.exp(kk * (lr * dt)[None])
    a_chunk_re = (cmag * jnp.cos(kk * (li * dt)[None])).reshape(2 * SUBLANES, g * p)
    a_chunk_im = (cmag * jnp.sin(kk * (li * dt)[None])).reshape(2 * SUBLANES, g * p)
    return (m_loc.astype(BF16), w_in.astype(BF16), w_out_re.astype(BF16), w_out_im.astype(BF16),
            a_chunk_re, a_chunk_im)


def _s5_state_in_kernel(u_ref, w_ref, sre_ref, sim_ref):
    s = (jnp.dot(u_ref[0], w_ref[0], preferred_element_type=F32)
         + jnp.dot(u_ref[1], w_ref[1], preferred_element_type=F32))
    sre_ref[...] = s[:, :LANES]
    sim_ref[...] = s[:, LANES:]


def _s5_state_in(u_chunks, w_in):
    g, r, k = u_chunks.shape
    return pl.pallas_call(
        _s5_state_in_kernel,
        grid=(g // 2,),
        in_specs=[pl.BlockSpec((2, r, k), lambda i: (i, 0, 0)),
                  pl.BlockSpec((2, k, 2 * LANES), lambda i: (i, 0, 0))],
        out_specs=[pl.BlockSpec((r, LANES), lambda i: (0, i)),
                   pl.BlockSpec((r, LANES), lambda i: (0, i))],
        out_shape=[jax.ShapeDtypeStruct((r, g * SSM_STATE), F32)] * 2,
        compiler_params=_cparams("parallel"),
        name="s5_state_in",
    )(u_chunks, w_in)


def _s5_recurrence_kernel(sre_ref, sim_ref, pre_ref, pim_ref, hre_ref, him_ref, *, bsz, n_chunks):
    tn = sre_ref.shape[1]
    shape = (SUBLANES, tn)
    row = lax.broadcasted_iota(jnp.int32, shape, 0)
    pw_re = pre_ref[0:SUBLANES, :]
    pw_im = pim_ref[0:SUBLANES, :]
    a_tile_re = jnp.broadcast_to(pre_ref[SUBLANES:SUBLANES + 1, :], shape)
    a_tile_im = jnp.broadcast_to(pim_ref[SUBLANES:SUBLANES + 1, :], shape)

    def tile_step(it, carry):
        out = []
        for b in range(bsz):
            h_re, h_im = carry[2 * b], carry[2 * b + 1]
            rows = pl.ds(pl.multiple_of(b * n_chunks + it * SUBLANES, SUBLANES), SUBLANES)
            y_re, y_im = sre_ref[rows, :], sim_ref[rows, :]
            for d in (1, 2, 4):
                m_re = jnp.broadcast_to(pre_ref[d:d + 1, :], shape)
                m_im = jnp.broadcast_to(pim_ref[d:d + 1, :], shape)
                s_re = jnp.where(row >= d, pltpu.roll(y_re, d, axis=0), 0.0)
                s_im = jnp.where(row >= d, pltpu.roll(y_im, d, axis=0), 0.0)
                y_re, y_im = (y_re + m_re * s_re - m_im * s_im,
                              y_im + m_re * s_im + m_im * s_re)
            e_re = jnp.where(row >= 1, pltpu.roll(y_re, 1, axis=0), 0.0)
            e_im = jnp.where(row >= 1, pltpu.roll(y_im, 1, axis=0), 0.0)
            hre_ref[rows, :] = pw_re * h_re - pw_im * h_im + e_re
            him_ref[rows, :] = pw_re * h_im + pw_im * h_re + e_im
            last_re = jnp.broadcast_to(y_re[SUBLANES - 1:SUBLANES, :], shape)
            last_im = jnp.broadcast_to(y_im[SUBLANES - 1:SUBLANES, :], shape)
            out.append(a_tile_re * h_re - a_tile_im * h_im + last_re)
            out.append(a_tile_re * h_im + a_tile_im * h_re + last_im)
        return tuple(out)

    zero = jnp.zeros(shape, F32)
    lax.fori_loop(0, n_chunks // SUBLANES, tile_step, (zero,) * (2 * bsz))


def _s5_recurrence(s_re, s_im, pw_re, pw_im, *, bsz, tn):
    r, n = s_re.shape
    tn = min(tn, n)
    spec = pl.BlockSpec((r, tn), lambda j: (0, j))
    pspec = pl.BlockSpec((pw_re.shape[0], tn), lambda j: (0, j))
    return pl.pallas_call(
        functools.partial(_s5_recurrence_kernel, bsz=bsz, n_chunks=r // bsz),
        grid=(n // tn,),
        in_specs=[spec, spec, pspec, pspec],
        out_specs=[spec, spec],
        out_shape=[jax.ShapeDtypeStruct((r, n), F32)] * 2,
        compiler_params=_cparams("parallel"),
        name="s5_recurrence",
    )(s_re, s_im, pw_re, pw_im)


def _s5_out_kernel(u_ref, m_ref, hre_ref, him_ref, wre_ref, wim_ref, y_ref):
    h_re = hre_ref[...].astype(BF16)
    h_im = him_ref[...].astype(BF16)
    for s in range(2):
        y = jnp.dot(u_ref[s], m_ref[s], preferred_element_type=F32)
        y += jnp.dot(h_re, wre_ref[s], preferred_element_type=F32)
        y += jnp.dot(h_im, wim_ref[s], preferred_element_type=F32)
        y_ref[s] = y.astype(y_ref.dtype)


def _s5_out(u_chunks, m_loc, h_re, h_im, w_out_re, w_out_im):
    g, r, k = u_chunks.shape
    return pl.pallas_call(
        _s5_out_kernel,
        grid=(g // 2,),
        in_specs=[pl.BlockSpec((2, r, k), lambda i: (i, 0, 0)),
                  pl.BlockSpec((2, k, k), lambda i: (i, 0, 0)),
                  pl.BlockSpec((r, LANES), lambda i: (0, i)),
                  pl.BlockSpec((r, LANES), lambda i: (0, i)),
                  pl.BlockSpec((2, LANES, k), lambda i: (i, 0, 0)),
                  pl.BlockSpec((2, LANES, k), lambda i: (i, 0, 0))],
        out_specs=pl.BlockSpec((2, r, k), lambda i: (i, 0, 0)),
        out_shape=jax.ShapeDtypeStruct((g, r, k), BF16),
        compiler_params=_cparams("parallel"),
        name="s5_out",
    )(u_chunks, m_loc, h_re, h_im, w_out_re, w_out_im)


def _s5_glu_kernel(y_ref, gate_ref, w_ref, b_ref, o_ref):
    y = jax.nn.gelu(y_ref[...].astype(F32))
    z = jnp.dot(y.astype(BF16), w_ref[...], preferred_element_type=F32) + b_ref[...]
    y = y * jax.nn.sigmoid(z)
    o_ref[...] = (y * jax.nn.silu(gate_ref[...].astype(F32))).astype(o_ref.dtype)


def _s5_glu(y, proj, w_glu, b_glu, *, gate_col_block, tm):
    m, w = y.shape
    tm = min(tm, m)
    return pl.pallas_call(
        _s5_glu_kernel,
        grid=(m // tm,),
        in_specs=[pl.BlockSpec((tm, w), lambda i: (i, 0)),
                  pl.BlockSpec((tm, w), lambda i: (i, gate_col_block)),
                  pl.BlockSpec((w, w), lambda i: (0, 0)),
                  pl.BlockSpec((1, w), lambda i: (0, 0))],
        out_specs=pl.BlockSpec((tm, w), lambda i: (i, 0)),
        out_shape=jax.ShapeDtypeStruct((m, w), BF16),
        compiler_params=_cparams("parallel"),
        name="s5_glu",
    )(y, proj, w_glu, b_glu.reshape(1, w))


def _out_proj_kernel(ys_ref, yd_ref, yx_ref, w_ref, x_ref, g_ref, o_ref):
    ws, wd = ys_ref.shape[1], yd_ref.shape[1]
    mix = jnp.dot(ys_ref[...], w_ref[0:ws, :], preferred_element_type=F32)
    mix += jnp.dot(yd_ref[...], w_ref[ws:ws + wd, :], preferred_element_type=F32)
    mix += jnp.dot(yx_ref[...], w_ref[ws + wd:, :], preferred_element_type=F32)
    var = jnp.mean(mix * mix, axis=-1, keepdims=True)
    o_ref[...] = x_ref[...] + mix * lax.rsqrt(var + NORM_EPS) * g_ref[...]


def _out_proj(y_s, y_d, y_x, w_out, x, g_post, *, tm):
    m, d = x.shape
    tm = min(tm, m)
    row = lambda i: (i, 0)
    const = lambda i: (0, 0)
    return pl.pallas_call(
        _out_proj_kernel,
        grid=(m // tm,),
        in_specs=[pl.BlockSpec((tm, y_s.shape[1]), row),
                  pl.BlockSpec((tm, y_d.shape[1]), row),
                  pl.BlockSpec((tm, y_x.shape[1]), row),
                  pl.BlockSpec(w_out.shape, const),
                  pl.BlockSpec((tm, d), row),
                  pl.BlockSpec((1, d), const)],
        out_specs=pl.BlockSpec((tm, d), row),
        out_shape=jax.ShapeDtypeStruct((m, d), F32),
        compiler_params=_cparams("parallel"),
        name="out_proj",
    )(y_s, y_d, y_x, w_out, x, g_post.reshape(1, d))


def kernel(x, mem, positions, norm_pre, norm_post, norm_mem, w_in, w_out, w_mem_kv, ssm_a_re, ssm_a_im, ssm_log_dt, ssm_b_re, ssm_b_im, ssm_c_re, ssm_c_im, ssm_d, w_glu, b_glu, diff_lq1, diff_lk1, diff_lq2, diff_lk2, diff_subln):
    bsz, seq, d_model = x.shape
    mem_tokens = mem.shape[1]
    depth = w_in.shape[0]
    m = bsz * seq
    ssm_width = ssm_a_re.shape[1] * SSM_GROUP
    n_groups = ssm_width // SSM_GROUP
    diff_width = DIFF_HEADS * 2 * DIFF_HEAD_DIM
    xattn_width = XATTN_HEADS * XATTN_HEAD_DIM
    n_chunks = seq // CHUNK
    col_gs = ssm_width
    col_qd = 2 * ssm_width
    col_kd = col_qd + diff_width
    col_vd = col_kd + diff_width
    col_gd = col_vd + diff_width
    col_qx = col_gd + diff_width
    col_gx = col_qx + xattn_width
    assert col_kd == col_qd + diff_width and col_qd % diff_width == 0

    inv = ROPE_THETA ** (-jnp.arange(0, DIFF_HEAD_DIM, 2, dtype=F32) / DIFF_HEAD_DIM)
    ang = positions.astype(F32).reshape(m, 1) * inv
    cos, sin = jnp.cos(ang), jnp.sin(ang)
    cos_t = jnp.concatenate([cos, cos, cos, cos], axis=-1)
    sin_t = jnp.concatenate([-sin, sin, -sin, sin], axis=-1)

    xf = x.reshape(m, d_model)
    mem_f = mem.reshape(bsz * mem_tokens, d_model)
    for l in range(depth):
        lambda_init = 0.8 - 0.6 * math.exp(-0.3 * l)
        proj = _norm_matmul(xf, norm_pre[l], w_in[l].astype(BF16), tm=512, tn=1024, name="in_proj")

        m_loc, w_sin, w_so_re, w_so_im, a_re, a_im = _s5_weights(
            ssm_a_re[l], ssm_a_im[l], ssm_log_dt[l], ssm_b_re[l], ssm_b_im[l],
            ssm_c_re[l], ssm_c_im[l], ssm_d[l])
        u_chunks = proj[:, :ssm_width].reshape(bsz, n_chunks, CHUNK, n_groups, SSM_GROUP)
        u_chunks = u_chunks.transpose(3, 0, 1, 2, 4).reshape(n_groups, bsz * n_chunks, CHUNK * SSM_GROUP)
        s_re, s_im = _s5_state_in(u_chunks, w_sin)
        h_re, h_im = _s5_recurrence(s_re, s_im, a_re, a_im, bsz=bsz, tn=1024)
        y_chunks = _s5_out(u_chunks, m_loc, h_re, h_im, w_so_re, w_so_im)
        y_scan = y_chunks.reshape(n_groups, bsz, n_chunks, CHUNK, SSM_GROUP)
        y_scan = y_scan.transpose(1, 2, 3, 0, 4).reshape(m, ssm_width)
        y_s = _s5_glu(y_scan, proj, w_glu[l].astype(BF16), b_glu[l], gate_col_block=col_gs // ssm_width, tm=1024)

        qk = _rope(proj, cos_t, sin_t, q_col_block=col_qd // diff_width, tm=1024)
        lam = (jnp.exp(jnp.sum(diff_lq1[l].astype(F32) * diff_lk1[l].astype(F32)))
               - jnp.exp(jnp.sum(diff_lq2[l].astype(F32) * diff_lk2[l].astype(F32))) + lambda_init)
        lam_vec = jnp.full((1, LANES), lam, F32)
        sg_vec = (diff_subln[l].astype(F32) * (1.0 - lambda_init)).reshape(1, LANES)
        y_d = _diff_attn(qk, proj, lam_vec, sg_vec, bsz=bsz, seq=seq,
                         v_col=col_vd // LANES, g_col=col_gd // LANES, tq=512)

        mem_kv = _norm_matmul(mem_f, norm_mem[l], w_mem_kv[l].astype(BF16), tm=512, tn=1024, name="mem_kv")
        y_x = _mem_attn(proj, mem_kv, bsz=bsz, seq=seq, mem_tokens=mem_tokens,
                        q_col_block=col_qx // xattn_width, g_col_block=col_gx // xattn_width, tq=1024)

        xf = _out_proj(y_s, y_d, y_x, w_out[l].astype(BF16), xf, norm_post[l], tm=512)
    return xf.reshape(bsz, seq, d_model)
```

```python
import functools
import math

import jax
import jax.numpy as jnp
import numpy as np
from jax import lax
from jax.experimental import pallas as pl
from jax.experimental.pallas import tpu as pltpu

F32 = jnp.float32
BF16 = jnp.bfloat16

SSM_GROUP = 16
SSM_STATE = 64
CHUNK = 16
DIFF_HEADS = 4
DIFF_HEAD_DIM = 64
XATTN_HEADS = 4
XATTN_HEAD_DIM = 128
ROPE_THETA = 10000.0
NORM_EPS = 1e-6
MASK_VALUE = -1e30
LANES = 128
SUBLANES = 8
BF16_SUBLANES = 16
VMEM_LIMIT = 56 * 1024 * 1024


def _cparams(*sem):
    return pltpu.CompilerParams(dimension_semantics=sem, vmem_limit_bytes=VMEM_LIMIT)


def _norm_matmul_kernel(x_ref, g_ref, w_ref, o_ref, h_ref):
    @pl.when(pl.program_id(1) == 0)
    def _():
        x = x_ref[...]
        var = jnp.mean(x * x, axis=-1, keepdims=True)
        h_ref[...] = (x * lax.rsqrt(var + NORM_EPS) * g_ref[...]).astype(BF16)

    o_ref[...] = jnp.dot(h_ref[...], w_ref[...], preferred_element_type=F32).astype(o_ref.dtype)


def _norm_matmul(x, g, w, *, tm, tn, name):
    m, d = x.shape
    n = w.shape[1]
    tm, tn = min(tm, m), min(tn, n)
    return pl.pallas_call(
        _norm_matmul_kernel,
        grid=(m // tm, n // tn),
        in_specs=[pl.BlockSpec((tm, d), lambda i, j: (i, 0)),
                  pl.BlockSpec((1, d), lambda i, j: (0, 0)),
                  pl.BlockSpec((d, tn), lambda i, j: (0, j))],
        out_specs=pl.BlockSpec((tm, tn), lambda i, j: (i, j)),
        out_shape=jax.ShapeDtypeStruct((m, n), BF16),
        scratch_shapes=[pltpu.VMEM((tm, d), BF16)],
        compiler_params=_cparams("parallel", "arbitrary"),
        name=name,
    )(x, g.reshape(1, d), w)


def _rope_kernel(q_ref, k_ref, cos_ref, sin_ref, qo_ref, ko_ref, *, q_scale):
    cos = cos_ref[...]
    sin = sin_ref[...]
    lane = lax.broadcasted_iota(jnp.int32, cos.shape, 1)
    first_half = (lane % DIFF_HEAD_DIM) < (DIFF_HEAD_DIM // 2)
    for x_ref, o_ref, mult in ((q_ref, qo_ref, q_scale), (k_ref, ko_ref, 1.0)):
        for h in range(DIFF_HEADS):
            x = x_ref[:, h * LANES:(h + 1) * LANES].astype(F32)
            partner = jnp.where(first_half,
                                pltpu.roll(x, LANES - DIFF_HEAD_DIM // 2, axis=1),
                                pltpu.roll(x, DIFF_HEAD_DIM // 2, axis=1))
            o_ref[:, h * LANES:(h + 1) * LANES] = ((x * cos + partner * sin) * mult).astype(o_ref.dtype)


def _rope(proj, cos_t, sin_t, *, q_col_block, tm):
    m = proj.shape[0]
    tm = min(tm, m)
    w = DIFF_HEADS * LANES
    q_scale = DIFF_HEAD_DIM ** -0.5 * math.log2(math.e)
    return pl.pallas_call(
        functools.partial(_rope_kernel, q_scale=q_scale),
        grid=(m // tm,),
        in_specs=[pl.BlockSpec((tm, w), lambda i: (i, q_col_block)),
                  pl.BlockSpec((tm, w), lambda i: (i, q_col_block + 1)),
                  pl.BlockSpec((tm, LANES), lambda i: (i, 0)),
                  pl.BlockSpec((tm, LANES), lambda i: (i, 0))],
        out_specs=[pl.BlockSpec((tm, w), lambda i: (i, 0))] * 2,
        out_shape=[jax.ShapeDtypeStruct((m, w), BF16)] * 2,
        compiler_params=_cparams("parallel"),
        name="rope",
    )(proj, proj, cos_t, sin_t)


def _diff_attn_kernel(q_ref, k_ref, vt_ref, g_ref, lam_ref, sg_ref, o_ref, acc1, acc2, *, tq):
    qi = pl.program_id(2)
    q = q_ref[...]
    lane = lax.broadcasted_iota(jnp.int32, q.shape, 1)
    q_maps = (jnp.where(lane < DIFF_HEAD_DIM, q, jnp.zeros_like(q)),
              jnp.where(lane >= DIFF_HEAD_DIM, q, jnp.zeros_like(q)))
    accs = (acc1, acc2)
    acc1[...] = jnp.zeros_like(acc1)
    acc2[...] = jnp.zeros_like(acc2)

    def block(j, ms, masked):
        off = pl.multiple_of(j * tq, tq)
        k = k_ref[pl.ds(off, tq), :]
        vt = vt_ref[0, :, pl.ds(off, tq)]
        out = []
        for qm, m_old, acc in zip(q_maps, ms, accs):
            s = lax.dot_general(k, qm, (((1,), (1,)), ((), ())), preferred_element_type=F32)
            if masked:
                key = lax.broadcasted_iota(jnp.int32, s.shape, 0)
                qry = lax.broadcasted_iota(jnp.int32, s.shape, 1)
                s = jnp.where(key <= qry, s, MASK_VALUE)
            m_new = jnp.maximum(m_old, jnp.max(s, axis=0, keepdims=True))
            p = jnp.exp2(s - m_new)
            acc[...] = acc[...] * jnp.exp2(m_old - m_new) + jnp.dot(vt, p.astype(BF16),
                                                                    preferred_element_type=F32)
            out.append(m_new)
        return tuple(out)

    m0 = jnp.full((1, tq), MASK_VALUE, F32)
    ms = lax.fori_loop(0, qi, lambda j, ms: block(j, ms, False), (m0, m0))
    block(qi, ms, True)

    d = 2 * DIFF_HEAD_DIM
    a1, a2 = acc1[...], acc2[...]
    o_t = a1[:d] / a1[d:d + 1] - lam_ref[0] * (a2[:d] / a2[d:d + 1])
    o = o_t.T
    var = jnp.mean(o * o, axis=-1, keepdims=True)
    o = o * lax.rsqrt(var + NORM_EPS) * sg_ref[...]
    o_ref[...] = (o * jax.nn.silu(g_ref[...].astype(F32))).astype(o_ref.dtype)


def _diff_attn(q_rot, k_rot, vt_ext, proj, lam, sg_vec, *, bsz, seq, g_col, tq):
    tq = min(tq, seq)
    nq = seq // tq
    v_rows = vt_ext.shape[1]
    return pl.pallas_call(
        functools.partial(_diff_attn_kernel, tq=tq),
        grid=(bsz, DIFF_HEADS, nq),
        in_specs=[
            pl.BlockSpec((tq, LANES), lambda b, h, i: (b * nq + i, h)),
            pl.BlockSpec((seq, LANES), lambda b, h, i: (b, h)),
            pl.BlockSpec((1, v_rows, seq), lambda b, h, i: (b * DIFF_HEADS + h, 0, 0)),
            pl.BlockSpec((tq, LANES), lambda b, h, i: (b * nq + i, g_col + h)),
            pl.BlockSpec(memory_space=pltpu.SMEM),
            pl.BlockSpec((1, LANES), lambda b, h, i: (0, 0)),
        ],
        out_specs=pl.BlockSpec((tq, LANES), lambda b, h, i: (b * nq + i, h)),
        out_shape=jax.ShapeDtypeStruct((bsz * seq, DIFF_HEADS * LANES), BF16),
        scratch_shapes=[pltpu.VMEM((v_rows, tq), F32), pltpu.VMEM((v_rows, tq), F32)],
        compiler_params=_cparams("parallel", "parallel", "arbitrary"),
        name="diff_attn",
    )(q_rot, k_rot, vt_ext, proj, lam, sg_vec)


def _mem_attn_kernel(q_ref, g_ref, mk_ref, mv_ref, o_ref):
    scale = XATTN_HEAD_DIM ** -0.5
    for h in range(XATTN_HEADS):
        sl = slice(h * XATTN_HEAD_DIM, (h + 1) * XATTN_HEAD_DIM)
        s = lax.dot_general(q_ref[:, sl], mk_ref[:, sl], (((1,), (1,)), ((), ())),
                            preferred_element_type=F32) * scale
        p = jnp.exp(s - jnp.max(s, axis=-1, keepdims=True))
        p = p / jnp.sum(p, axis=-1, keepdims=True)
        o = jnp.dot(p.astype(BF16), mv_ref[:, sl], preferred_element_type=F32)
        o_ref[:, sl] = (o * jax.nn.silu(g_ref[:, sl].astype(F32))).astype(o_ref.dtype)


def _mem_attn(proj, mem_kv, *, bsz, seq, mem_tokens, q_col_block, g_col_block, tq):
    tq = min(tq, seq)
    nq = seq // tq
    w = XATTN_HEADS * XATTN_HEAD_DIM
    return pl.pallas_call(
        _mem_attn_kernel,
        grid=(bsz, nq),
        in_specs=[pl.BlockSpec((tq, w), lambda b, i: (b * nq + i, q_col_block)),
                  pl.BlockSpec((tq, w), lambda b, i: (b * nq + i, g_col_block)),
                  pl.BlockSpec((mem_tokens, w), lambda b, i: (b, 0)),
                  pl.BlockSpec((mem_tokens, w), lambda b, i: (b, 1))],
        out_specs=pl.BlockSpec((tq, w), lambda b, i: (b * nq + i, 0)),
        out_shape=jax.ShapeDtypeStruct((bsz * seq, w), BF16),
        compiler_params=_cparams("parallel", "parallel"),
        name="mem_attn",
    )(proj, proj, mem_kv, mem_kv)


def _s5_taps_kernel(lhs_ref, rhs_ref, o_ref):
    o_ref[0] = jnp.dot(lhs_ref[0], rhs_ref[0], preferred_element_type=F32,
                       precision=lax.Precision.HIGHEST)


def _s5_taps(lhs, rhs):
    g, r, k = lhs.shape
    n = rhs.shape[2]
    return pl.pallas_call(
        _s5_taps_kernel,
        grid=(g,),
        in_specs=[pl.BlockSpec((1, r, k), lambda i: (i, 0, 0)),
                  pl.BlockSpec((1, k, n), lambda i: (i, 0, 0))],
        out_specs=pl.BlockSpec((1, r, n), lambda i: (i, 0, 0)),
        out_shape=jax.ShapeDtypeStruct((g, r, n), F32),
        compiler_params=_cparams("parallel"),
        name="s5_taps",
    )(lhs, rhs)


def _s5_weights(a_re, a_im, log_dt, b_re, b_im, c_re, c_im, d_skip):
    g = a_re.shape[0]
    p, h, t = SSM_STATE, SSM_GROUP, CHUNK
    dt = jnp.exp(log_dt.astype(F32))[:, None]
    lr, li = a_re.astype(F32), a_im.astype(F32)
    mag = jnp.exp(lr * dt)
    abar_re, abar_im = mag * jnp.cos(li * dt), mag * jnp.sin(li * dt)
    den = lr * lr + li * li
    nr, ni = abar_re - 1.0, abar_im
    z_re = (nr * lr + ni * li) / den
    z_im = (ni * lr - nr * li) / den
    br, bi = b_re.astype(F32), b_im.astype(F32)
    bbar_re = z_re[..., None] * br - z_im[..., None] * bi
    bbar_im = z_re[..., None] * bi + z_im[..., None] * br
    cr, ci = c_re.astype(F32), c_im.astype(F32)

    tau = jnp.arange(t + 1, dtype=F32)[None, :, None]
    pmag = jnp.exp(tau * (lr * dt)[:, None, :])
    pw_re = pmag * jnp.cos(tau * (li * dt)[:, None, :])
    pw_im = pmag * jnp.sin(tau * (li * dt)[:, None, :])

    ca_re = cr[:, None] * pw_re[:, :, None, :] - ci[:, None] * pw_im[:, :, None, :]
    ca_im = cr[:, None] * pw_im[:, :, None, :] + ci[:, None] * pw_re[:, :, None, :]

    lhs = jnp.concatenate([ca_re[:, :t], ca_im[:, :t]], axis=-1).reshape(g, t * h, 2 * p)
    rhs = jnp.concatenate([bbar_re, -bbar_im], axis=1)
    taps = _s5_taps(lhs, rhs).reshape(g, t, h, h)
    taps = taps.at[:, 0].add(jax.vmap(jnp.diag)(d_skip.astype(F32)))

    idx = jnp.arange(t)
    lag = idx[None, :] - idx[:, None]
    toep = jnp.where((lag >= 0)[None, :, :, None, None], taps[:, jnp.clip(lag, 0, t - 1)], 0.0)
    m_loc = toep.transpose(0, 1, 4, 2, 3).reshape(g, t * h, t * h)

    pj_re, pj_im = pw_re[:, t - 1 - idx], pw_im[:, t - 1 - idx]
    win_re = pj_re[:, :, None, :] * bbar_re.transpose(0, 2, 1)[:, None] \
        - pj_im[:, :, None, :] * bbar_im.transpose(0, 2, 1)[:, None]
    win_im = pj_re[:, :, None, :] * bbar_im.transpose(0, 2, 1)[:, None] \
        + pj_im[:, :, None, :] * bbar_re.transpose(0, 2, 1)[:, None]
    win_re = win_re.reshape(g, t * h, p)
    win_im = win_im.reshape(g, t * h, p)
    slot = (jnp.arange(g) % 2)[:, None, None]
    zero = jnp.zeros_like(win_re)
    w_in = jnp.concatenate([jnp.where(slot == 0, win_re, zero), jnp.where(slot == 1, win_re, zero),
                            jnp.where(slot == 0, win_im, zero), jnp.where(slot == 1, win_im, zero)], axis=-1)

    wo_re = ca_re[:, 1:].transpose(0, 3, 1, 2).reshape(g, p, t * h)
    wo_im = -ca_im[:, 1:].transpose(0, 3, 1, 2).reshape(g, p, t * h)
    zero = jnp.zeros_like(wo_re)
    w_out_re = jnp.concatenate([jnp.where(slot == 0, wo_re, zero), jnp.where(slot == 1, wo_re, zero)], axis=1)
    w_out_im = jnp.concatenate([jnp.where(slot == 0, wo_im, zero), jnp.where(slot == 1, wo_im, zero)], axis=1)

    kk = (t * jnp.arange(2 * SUBLANES, dtype=F32))[:, None, None]
    cmag = jnp.exp(kk * (lr * dt)[None])
    a_chunk_re = (cmag * jnp.cos(kk * (li * dt)[None])).reshape(2 * SUBLANES, g * p)
    a_chunk_im = (cmag * jnp.sin(kk * (li * dt)[None])).reshape(2 * SUBLANES, g * p)
    return (m_loc.astype(BF16), w_in.astype(BF16), w_out_re.astype(BF16), w_out_im.astype(BF16),
            a_chunk_re, a_chunk_im)


def _s5_state_in_kernel(u_ref, w_ref, sre_ref, sim_ref):
    s = (jnp.dot(u_ref[0], w_ref[0], preferred_element_type=F32)
         + jnp.dot(u_ref[1], w_ref[1], preferred_element_type=F32))
    sre_ref[...] = s[:, :LANES]
    sim_ref[...] = s[:, LANES:]


def _s5_state_in(u_chunks, w_in):
    g, r, k = u_chunks.shape
    return pl.pallas_call(
        _s5_state_in_kernel,
        grid=(g // 2,),
        in_specs=[pl.BlockSpec((2, r, k), lambda i: (i, 0, 0)),
                  pl.BlockSpec((2, k, 2 * LANES), lambda i: (i, 0, 0))],
        out_specs=[pl.BlockSpec((r, LANES), lambda i: (0, i)),
                   pl.BlockSpec((r, LANES), lambda i: (0, i))],
        out_shape=[jax.ShapeDtypeStruct((r, g * SSM_STATE), F32)] * 2,
        compiler_params=_cparams("parallel"),
        name="s5_state_in",
    )(u_chunks, w_in)


def _s5_recurrence_kernel(sre_ref, sim_ref, pre_ref, pim_ref, hre_ref, him_ref, *, bsz, n_chunks):
    tn = sre_ref.shape[1]
    shape = (SUBLANES, tn)
    row = lax.broadcasted_iota(jnp.int32, shape, 0)
    pw_re = pre_ref[0:SUBLANES, :]
    pw_im = pim_ref[0:SUBLANES, :]
    a_tile_re = jnp.broadcast_to(pre_ref[SUBLANES:SUBLANES + 1, :], shape)
    a_tile_im = jnp.broadcast_to(pim_ref[SUBLANES:SUBLANES + 1, :], shape)

    def tile_step(it, carry):
        out = []
        for b in range(bsz):
            h_re, h_im = carry[2 * b], carry[2 * b + 1]
            rows = pl.ds(pl.multiple_of(b * n_chunks + it * SUBLANES, SUBLANES), SUBLANES)
            y_re, y_im = sre_ref[rows, :], sim_ref[rows, :]
            for d in (1, 2, 4):
                m_re = jnp.broadcast_to(pre_ref[d:d + 1, :], shape)
                m_im = jnp.broadcast_to(pim_ref[d:d + 1, :], shape)
                s_re = jnp.where(row >= d, pltpu.roll(y_re, d, axis=0), 0.0)
                s_im = jnp.where(row >= d, pltpu.roll(y_im, d, axis=0), 0.0)
                y_re, y_im = (y_re + m_re * s_re - m_im * s_im,
                              y_im + m_re * s_im + m_im * s_re)
            e_re = jnp.where(row >= 1, pltpu.roll(y_re, 1, axis=0), 0.0)
            e_im = jnp.where(row >= 1, pltpu.roll(y_im, 1, axis=0), 0.0)
            hre_ref[rows, :] = pw_re * h_re - pw_im * h_im + e_re
            him_ref[rows, :] = pw_re * h_im + pw_im * h_re + e_im
            last_re = jnp.broadcast_to(y_re[SUBLANES - 1:SUBLANES, :], shape)
            last_im = jnp.broadcast_to(y_im[SUBLANES - 1:SUBLANES, :], shape)
            out.append(a_tile_re * h_re - a_tile_im * h_im + last_re)
            out.append(a_tile_re * h_im + a_tile_im * h_re + last_im)
        return tuple(out)

    zero = jnp.zeros(shape, F32)
    lax.fori_loop(0, n_chunks // SUBLANES, tile_step, (zero,) * (2 * bsz))


def _s5_recurrence(s_re, s_im, pw_re, pw_im, *, bsz, tn):
    r, n = s_re.shape
    tn = min(tn, n)
    spec = pl.BlockSpec((r, tn), lambda j: (0, j))
    pspec = pl.BlockSpec((pw_re.shape[0], tn), lambda j: (0, j))
    return pl.pallas_call(
        functools.partial(_s5_recurrence_kernel, bsz=bsz, n_chunks=r // bsz),
        grid=(n // tn,),
        in_specs=[spec, spec, pspec, pspec],
        out_specs=[spec, spec],
        out_shape=[jax.ShapeDtypeStruct((r, n), F32)] * 2,
        compiler_params=_cparams("parallel"),
        name="s5_recurrence",
    )(s_re, s_im, pw_re, pw_im)


def _s5_out_kernel(u_ref, m_ref, hre_ref, him_ref, wre_ref, wim_ref, y_ref):
    h_re = hre_ref[...].astype(BF16)
    h_im = him_ref[...].astype(BF16)
    for s in range(2):
        y = jnp.dot(u_ref[s], m_ref[s], preferred_element_type=F32)
        y += jnp.dot(h_re, wre_ref[s], preferred_element_type=F32)
        y += jnp.dot(h_im, wim_ref[s], preferred_element_type=F32)
        y_ref[s] = y.astype(y_ref.dtype)


def _s5_out(u_chunks, m_loc, h_re, h_im, w_out_re, w_out_im):
    g, r, k = u_chunks.shape
    return pl.pallas_call(
        _s5_out_kernel,
        grid=(g // 2,),
        in_specs=[pl.BlockSpec((2, r, k), lambda i: (i, 0, 0)),
                  pl.BlockSpec((2, k, k), lambda i: (i, 0, 0)),
                  pl.BlockSpec((r, LANES), lambda i: (0, i)),
                  pl.BlockSpec((r, LANES), lambda i: (0, i)),
                  pl.BlockSpec((2, LANES, k), lambda i: (i, 0, 0)),
                  pl.BlockSpec((2, LANES, k), lambda i: (i, 0, 0))],
        out_specs=pl.BlockSpec((2, r, k), lambda i: (i, 0, 0)),
        out_shape=jax.ShapeDtypeStruct((g, r, k), BF16),
        compiler_params=_cparams("parallel"),
        name="s5_out",
    )(u_chunks, m_loc, h_re, h_im, w_out_re, w_out_im)


def _s5_glu_kernel(y_ref, gate_ref, w_ref, b_ref, o_ref):
    y = jax.nn.gelu(y_ref[...].astype(F32))
    z = jnp.dot(y.astype(BF16), w_ref[...], preferred_element_type=F32) + b_ref[...]
    y = y * jax.nn.sigmoid(z)
    o_ref[...] = (y * jax.nn.silu(gate_ref[...].astype(F32))).astype(o_ref.dtype)


def _s5_glu(y, proj, w_glu, b_glu, *, gate_col_block, tm):
    m, w = y.shape
    tm = min(tm, m)
    return pl.pallas_call(
        _s5_glu_kernel,
        grid=(m // tm,),
        in_specs=[pl.BlockSpec((tm, w), lambda i: (i, 0)),
                  pl.BlockSpec((tm, w), lambda i: (i, gate_col_block)),
                  pl.BlockSpec((w, w), lambda i: (0, 0)),
                  pl.BlockSpec((1, w), lambda i: (0, 0))],
        out_specs=pl.BlockSpec((tm, w), lambda i: (i, 0)),
        out_shape=jax.ShapeDtypeStruct((m, w), BF16),
        compiler_params=_cparams("parallel"),
        name="s5_glu",
    )(y, proj, w_glu, b_glu.reshape(1, w))


def _out_proj_kernel(ys_ref, yd_ref, yx_ref, w_ref, x_ref, g_ref, o_ref):
    ws, wd = ys_ref.shape[1], yd_ref.shape[1]
    mix = jnp.dot(ys_ref[...], w_ref[0:ws, :], preferred_element_type=F32)
    mix += jnp.dot(yd_ref[...], w_ref[ws:ws + wd, :], preferred_element_type=F32)
    mix += jnp.dot(yx_ref[...], w_ref[ws + wd:, :], preferred_element_type=F32)
    var = jnp.mean(mix * mix, axis=-1, keepdims=True)
    o_ref[...] = x_ref[...] + mix * lax.rsqrt(var + NORM_EPS) * g_ref[...]


def _out_proj(y_s, y_d, y_x, w_out, x, g_post, *, tm):
    m, d = x.shape
    tm = min(tm, m)
    row = lambda i: (i, 0)
    const = lambda i: (0, 0)
    return pl.pallas_call(
        _out_proj_kernel,
        grid=(m // tm,),
        in_specs=[pl.BlockSpec((tm, y_s.shape[1]), row),
                  pl.BlockSpec((tm, y_d.shape[1]), row),
                  pl.BlockSpec((tm, y_x.shape[1]), row),
                  pl.BlockSpec(w_out.shape, const),
                  pl.BlockSpec((tm, d), row),
                  pl.BlockSpec((1, d), const)],
        out_specs=pl.BlockSpec((tm, d), row),
        out_shape=jax.ShapeDtypeStruct((m, d), F32),
        compiler_params=_cparams("parallel"),
        name="out_proj",
    )(y_s, y_d, y_x, w_out, x, g_post.reshape(1, d))


def kernel(x, mem, positions, norm_pre, norm_post, norm_mem, w_in, w_out, w_mem_kv, ssm_a_re, ssm_a_im, ssm_log_dt, ssm_b_re, ssm_b_im, ssm_c_re, ssm_c_im, ssm_d, w_glu, b_glu, diff_lq1, diff_lk1, diff_lq2, diff_lk2, diff_subln):
    bsz, seq, d_model = x.shape
    mem_tokens = mem.shape[1]
    depth = w_in.shape[0]
    m = bsz * seq
    ssm_width = ssm_a_re.shape[1] * SSM_GROUP
    n_groups = ssm_width // SSM_GROUP
    diff_width = DIFF_HEADS * 2 * DIFF_HEAD_DIM
    xattn_width = XATTN_HEADS * XATTN_HEAD_DIM
    n_chunks = seq // CHUNK
    col_gs = ssm_width
    col_qd = 2 * ssm_width
    col_kd = col_qd + diff_width
    col_vd = col_kd + diff_width
    col_gd = col_vd + diff_width
    col_qx = col_gd + diff_width
    col_gx = col_qx + xattn_width
    assert col_kd == col_qd + diff_width and col_qd % diff_width == 0

    inv = ROPE_THETA ** (-jnp.arange(0, DIFF_HEAD_DIM, 2, dtype=F32) / DIFF_HEAD_DIM)
    ang = positions.astype(F32).reshape(m, 1) * inv
    cos, sin = jnp.cos(ang), jnp.sin(ang)
    cos_t = jnp.concatenate([cos, cos, cos, cos], axis=-1)
    sin_t = jnp.concatenate([-sin, sin, -sin, sin], axis=-1)

    xf = x.reshape(m, d_model)
    mem_f = mem.reshape(bsz * mem_tokens, d_model)
    for l in range(depth):
        lambda_init = 0.8 - 0.6 * math.exp(-0.3 * l)
        proj = _norm_matmul(xf, norm_pre[l], w_in[l].astype(BF16), tm=512, tn=1024, name="in_proj")

        m_loc, w_sin, w_so_re, w_so_im, a_re, a_im = _s5_weights(
            ssm_a_re[l], ssm_a_im[l], ssm_log_dt[l], ssm_b_re[l], ssm_b_im[l],
            ssm_c_re[l], ssm_c_im[l], ssm_d[l])
        u_chunks = proj[:, :ssm_width].reshape(bsz, n_chunks, CHUNK, n_groups, SSM_GROUP)
        u_chunks = u_chunks.transpose(3, 0, 1, 2, 4).reshape(n_groups, bsz * n_chunks, CHUNK * SSM_GROUP)
        s_re, s_im = _s5_state_in(u_chunks, w_sin)
        h_re, h_im = _s5_recurrence(s_re, s_im, a_re, a_im, bsz=bsz, tn=1024)
        y_chunks = _s5_out(u_chunks, m_loc, h_re, h_im, w_so_re, w_so_im)
        y_scan = y_chunks.reshape(n_groups, bsz, n_chunks, CHUNK, SSM_GROUP)
        y_scan = y_scan.transpose(1, 2, 3, 0, 4).reshape(m, ssm_width)
        y_s = _s5_glu(y_scan, proj, w_glu[l].astype(BF16), b_glu[l], gate_col_block=col_gs // ssm_width, tm=1024)

        q_rot, k_rot = _rope(proj, cos_t, sin_t, q_col_block=col_qd // diff_width, tm=1024)
        lam = (jnp.exp(jnp.sum(diff_lq1[l].astype(F32) * diff_lk1[l].astype(F32)))
               - jnp.exp(jnp.sum(diff_lq2[l].astype(F32) * diff_lk2[l].astype(F32))) + lambda_init)
        sg_vec = (diff_subln[l].astype(F32) * (1.0 - lambda_init)).reshape(1, LANES)
        v_t = proj[:, col_vd:col_vd + diff_width].reshape(bsz, seq, DIFF_HEADS, 2 * DIFF_HEAD_DIM)
        v_t = v_t.transpose(0, 2, 3, 1)
        vt_ext = jnp.concatenate([v_t, jnp.ones((bsz, DIFF_HEADS, BF16_SUBLANES, seq), BF16)], axis=2)
        vt_ext = vt_ext.reshape(bsz * DIFF_HEADS, 2 * DIFF_HEAD_DIM + BF16_SUBLANES, seq)
        y_d = _diff_attn(q_rot, k_rot, vt_ext, proj, lam.reshape(1), sg_vec, bsz=bsz, seq=seq,
                         g_col=col_gd // LANES, tq=512)

        mem_kv = _norm_matmul(mem_f, norm_mem[l], w_mem_kv[l].astype(BF16), tm=512, tn=1024, name="mem_kv")
        y_x = _mem_attn(proj, mem_kv, bsz=bsz, seq=seq, mem_tokens=mem_tokens,
                        q_col_block=col_qx // xattn_width, g_col_block=col_gx // xattn_width, tq=1024)

        xf = _out_proj(y_s, y_d, y_x, w_out[l].astype(BF16), xf, norm_post[l], tm=512)
    return xf.reshape(bsz, seq, d_model)
```

```python
import functools
import math

import jax
import jax.numpy as jnp
import numpy as np
from jax import lax
from jax.experimental import pallas as pl
from jax.experimental.pallas import tpu as pltpu

F32 = jnp.float32
BF16 = jnp.bfloat16

SSM_GROUP = 16
SSM_STATE = 64
CHUNK = 16
DIFF_HEADS = 4
DIFF_HEAD_DIM = 64
XATTN_HEADS = 4
XATTN_HEAD_DIM = 128
ROPE_THETA = 10000.0
NORM_EPS = 1e-6
MASK_VALUE = -1e30
LANES = 128
SUBLANES = 8
BF16_SUBLANES = 16
GROUPS_PER_TILE = LANES // SSM_GROUP
VMEM_LIMIT = 56 * 1024 * 1024


def _cparams(*sem):
    return pltpu.CompilerParams(dimension_semantics=sem, vmem_limit_bytes=VMEM_LIMIT)


def _norm_matmul_kernel(x_ref, g_ref, w_ref, o_ref, h_ref):
    @pl.when(pl.program_id(1) == 0)
    def _():
        x = x_ref[...]
        var = jnp.mean(x * x, axis=-1, keepdims=True)
        h_ref[...] = (x * lax.rsqrt(var + NORM_EPS) * g_ref[...]).astype(BF16)

    o_ref[...] = jnp.dot(h_ref[...], w_ref[...], preferred_element_type=F32).astype(o_ref.dtype)


def _norm_matmul(x, g, w, *, tm, tn, name):
    m, d = x.shape
    n = w.shape[1]
    tm, tn = min(tm, m), min(tn, n)
    return pl.pallas_call(
        _norm_matmul_kernel,
        grid=(m // tm, n // tn),
        in_specs=[pl.BlockSpec((tm, d), lambda i, j: (i, 0)),
                  pl.BlockSpec((1, d), lambda i, j: (0, 0)),
                  pl.BlockSpec((d, tn), lambda i, j: (0, j))],
        out_specs=pl.BlockSpec((tm, tn), lambda i, j: (i, j)),
        out_shape=jax.ShapeDtypeStruct((m, n), BF16),
        scratch_shapes=[pltpu.VMEM((tm, d), BF16)],
        compiler_params=_cparams("parallel", "arbitrary"),
        name=name,
    )(x, g.reshape(1, d), w)


def _chunk_major_perm(rows):
    rb = rows // CHUNK
    src = (np.arange(rows) % rb) * CHUNK + np.arange(rows) // rb
    return jnp.asarray(np.eye(rows, dtype=np.float32)[src], BF16)


def _in_proj_kernel(x_ref, g_ref, w_ref, p_ref, o3_ref, o_ref, hp_ref, hn_ref, *, n_perm):
    j = pl.program_id(1)
    rb = x_ref.shape[0] // CHUNK

    @pl.when(j == 0)
    def _():
        x = x_ref[...]
        var = jnp.mean(x * x, axis=-1, keepdims=True)
        hn = (x * lax.rsqrt(var + NORM_EPS) * g_ref[...]).astype(BF16)
        hn_ref[...] = hn
        hp_ref[...] = jnp.dot(p_ref[...], hn, preferred_element_type=F32).astype(BF16)

    @pl.when(j < n_perm)
    def _():
        res = jnp.dot(hp_ref[...], w_ref[...], preferred_element_type=F32)
        for i in range(CHUNK):
            o3_ref[i] = res[i * rb:(i + 1) * rb].astype(o3_ref.dtype)

    @pl.when(j >= n_perm)
    def _():
        o_ref[...] = jnp.dot(hn_ref[...], w_ref[...], preferred_element_type=F32).astype(o_ref.dtype)


def _in_proj(x, g, w, *, n_perm_cols, tm, tn):
    m, d = x.shape
    n = w.shape[1]
    tm = min(tm, m)
    n_perm = n_perm_cols // tn
    rb = tm // CHUNK
    return pl.pallas_call(
        functools.partial(_in_proj_kernel, n_perm=n_perm),
        grid=(m // tm, n // tn),
        in_specs=[pl.BlockSpec((tm, d), lambda i, j: (i, 0)),
                  pl.BlockSpec((1, d), lambda i, j: (0, 0)),
                  pl.BlockSpec((d, tn), lambda i, j: (0, j)),
                  pl.BlockSpec((tm, tm), lambda i, j: (0, 0))],
        out_specs=[pl.BlockSpec((CHUNK, rb, tn), lambda i, j: (0, i, jnp.minimum(j, n_perm - 1))),
                   pl.BlockSpec((tm, tn), lambda i, j: (i, jnp.maximum(j - n_perm, 0)))],
        out_shape=[jax.ShapeDtypeStruct((CHUNK, m // CHUNK, n_perm_cols), BF16),
                   jax.ShapeDtypeStruct((m, n - n_perm_cols), BF16)],
        scratch_shapes=[pltpu.VMEM((tm, d), BF16), pltpu.VMEM((tm, d), BF16)],
        compiler_params=_cparams("parallel", "arbitrary"),
        name="in_proj",
    )(x, g.reshape(1, d), w, _chunk_major_perm(tm))


def _rope_kernel(q_ref, k_ref, cos_ref, sin_ref, qo_ref, ko_ref, *, q_scale):
    cos = cos_ref[...]
    sin = sin_ref[...]
    lane = lax.broadcasted_iota(jnp.int32, cos.shape, 1)
    first_half = (lane % DIFF_HEAD_DIM) < (DIFF_HEAD_DIM // 2)
    for x_ref, o_ref, mult in ((q_ref, qo_ref, q_scale), (k_ref, ko_ref, 1.0)):
        for h in range(DIFF_HEADS):
            x = x_ref[:, h * LANES:(h + 1) * LANES].astype(F32)
            partner = jnp.where(first_half,
                                pltpu.roll(x, LANES - DIFF_HEAD_DIM // 2, axis=1),
                                pltpu.roll(x, DIFF_HEAD_DIM // 2, axis=1))
            o_ref[:, h * LANES:(h + 1) * LANES] = ((x * cos + partner * sin) * mult).astype(o_ref.dtype)


def _rope(proj, cos_t, sin_t, *, q_col_block, tm):
    m = proj.shape[0]
    tm = min(tm, m)
    w = DIFF_HEADS * LANES
    q_scale = DIFF_HEAD_DIM ** -0.5 * math.log2(math.e)
    return pl.pallas_call(
        functools.partial(_rope_kernel, q_scale=q_scale),
        grid=(m // tm,),
        in_specs=[pl.BlockSpec((tm, w), lambda i: (i, q_col_block)),
                  pl.BlockSpec((tm, w), lambda i: (i, q_col_block + 1)),
                  pl.BlockSpec((tm, LANES), lambda i: (i, 0)),
                  pl.BlockSpec((tm, LANES), lambda i: (i, 0))],
        out_specs=[pl.BlockSpec((tm, w), lambda i: (i, 0))] * 2,
        out_shape=[jax.ShapeDtypeStruct((m, w), BF16)] * 2,
        compiler_params=_cparams("parallel"),
        name="rope",
    )(proj, proj, cos_t, sin_t)


def _diff_attn_kernel(q_ref, k_ref, vt_ref, g_ref, lam_ref, sg_ref, o_ref, acc1, acc2, *, tq):
    qi = pl.program_id(2)
    q = q_ref[...]
    lane = lax.broadcasted_iota(jnp.int32, q.shape, 1)
    q_maps = (jnp.where(lane < DIFF_HEAD_DIM, q, jnp.zeros_like(q)),
              jnp.where(lane >= DIFF_HEAD_DIM, q, jnp.zeros_like(q)))
    accs = (acc1, acc2)
    acc1[...] = jnp.zeros_like(acc1)
    acc2[...] = jnp.zeros_like(acc2)

    def block(j, ms, masked):
        off = pl.multiple_of(j * tq, tq)
        k = k_ref[pl.ds(off, tq), :]
        vt = vt_ref[0, :, pl.ds(off, tq)]
        out = []
        for qm, m_old, acc in zip(q_maps, ms, accs):
            s = lax.dot_general(k, qm, (((1,), (1,)), ((), ())), preferred_element_type=F32)
            if masked:
                key = lax.broadcasted_iota(jnp.int32, s.shape, 0)
                qry = lax.broadcasted_iota(jnp.int32, s.shape, 1)
                s = jnp.where(key <= qry, s, MASK_VALUE)
            m_new = jnp.maximum(m_old, jnp.max(s, axis=0, keepdims=True))
            p = jnp.exp2(s - m_new)
            acc[...] = acc[...] * jnp.exp2(m_old - m_new) + jnp.dot(vt, p.astype(BF16),
                                                                    preferred_element_type=F32)
            out.append(m_new)
        return tuple(out)

    m0 = jnp.full((1, tq), MASK_VALUE, F32)
    ms = lax.fori_loop(0, qi, lambda j, ms: block(j, ms, False), (m0, m0))
    block(qi, ms, True)

    d = 2 * DIFF_HEAD_DIM
    a1, a2 = acc1[...], acc2[...]
    o_t = a1[:d] / a1[d:d + 1] - lam_ref[0] * (a2[:d] / a2[d:d + 1])
    o = o_t.T
    var = jnp.mean(o * o, axis=-1, keepdims=True)
    o = o * lax.rsqrt(var + NORM_EPS) * sg_ref[...]
    o_ref[...] = (o * jax.nn.silu(g_ref[...].astype(F32))).astype(o_ref.dtype)


def _diff_attn(q_rot, k_rot, vt_ext, proj, lam, sg_vec, *, bsz, seq, g_col, tq):
    tq = min(tq, seq)
    nq = seq // tq
    v_rows = vt_ext.shape[1]
    return pl.pallas_call(
        functools.partial(_diff_attn_kernel, tq=tq),
        grid=(bsz, DIFF_HEADS, nq),
        in_specs=[
            pl.BlockSpec((tq, LANES), lambda b, h, i: (b * nq + i, h)),
            pl.BlockSpec((seq, LANES), lambda b, h, i: (b, h)),
            pl.BlockSpec((1, v_rows, seq), lambda b, h, i: (b * DIFF_HEADS + h, 0, 0)),
            pl.BlockSpec((tq, LANES), lambda b, h, i: (b * nq + i, g_col + h)),
            pl.BlockSpec(memory_space=pltpu.SMEM),
            pl.BlockSpec((1, LANES), lambda b, h, i: (0, 0)),
        ],
        out_specs=pl.BlockSpec((tq, LANES), lambda b, h, i: (b * nq + i, h)),
        out_shape=jax.ShapeDtypeStruct((bsz * seq, DIFF_HEADS * LANES), BF16),
        scratch_shapes=[pltpu.VMEM((v_rows, tq), F32), pltpu.VMEM((v_rows, tq), F32)],
        compiler_params=_cparams("parallel", "parallel", "arbitrary"),
        name="diff_attn",
    )(q_rot, k_rot, vt_ext, proj, lam, sg_vec)


def _mem_attn_kernel(q_ref, g_ref, mk_ref, mv_ref, o_ref):
    scale = XATTN_HEAD_DIM ** -0.5
    for h in range(XATTN_HEADS):
        sl = slice(h * XATTN_HEAD_DIM, (h + 1) * XATTN_HEAD_DIM)
        s = lax.dot_general(q_ref[:, sl], mk_ref[:, sl], (((1,), (1,)), ((), ())),
                            preferred_element_type=F32) * scale
        p = jnp.exp(s - jnp.max(s, axis=-1, keepdims=True))
        p = p / jnp.sum(p, axis=-1, keepdims=True)
        o = jnp.dot(p.astype(BF16), mv_ref[:, sl], preferred_element_type=F32)
        o_ref[:, sl] = (o * jax.nn.silu(g_ref[:, sl].astype(F32))).astype(o_ref.dtype)


def _mem_attn(proj, mem_kv, *, bsz, seq, mem_tokens, q_col_block, g_col_block, tq):
    tq = min(tq, seq)
    nq = seq // tq
    w = XATTN_HEADS * XATTN_HEAD_DIM
    return pl.pallas_call(
        _mem_attn_kernel,
        grid=(bsz, nq),
        in_specs=[pl.BlockSpec((tq, w), lambda b, i: (b * nq + i, q_col_block)),
                  pl.BlockSpec((tq, w), lambda b, i: (b * nq + i, g_col_block)),
                  pl.BlockSpec((mem_tokens, w), lambda b, i: (b, 0)),
                  pl.BlockSpec((mem_tokens, w), lambda b, i: (b, 1))],
        out_specs=pl.BlockSpec((tq, w), lambda b, i: (b * nq + i, 0)),
        out_shape=jax.ShapeDtypeStruct((bsz * seq, w), BF16),
        compiler_params=_cparams("parallel", "parallel"),
        name="mem_attn",
    )(proj, proj, mem_kv, mem_kv)


def _s5_taps_kernel(lhs_ref, rhs_ref, o_ref):
    o_ref[0] = jnp.dot(lhs_ref[0], rhs_ref[0], preferred_element_type=F32,
                       precision=lax.Precision.HIGHEST)


def _s5_taps(lhs, rhs):
    g, r, k = lhs.shape
    n = rhs.shape[2]
    return pl.pallas_call(
        _s5_taps_kernel,
        grid=(g,),
        in_specs=[pl.BlockSpec((1, r, k), lambda i: (i, 0, 0)),
                  pl.BlockSpec((1, k, n), lambda i: (i, 0, 0))],
        out_specs=pl.BlockSpec((1, r, n), lambda i: (i, 0, 0)),
        out_shape=jax.ShapeDtypeStruct((g, r, n), F32),
        compiler_params=_cparams("parallel"),
        name="s5_taps",
    )(lhs, rhs)


def _s5_weights(a_re, a_im, log_dt, b_re, b_im, c_re, c_im, d_skip):
    g = a_re.shape[0]
    p, h, t = SSM_STATE, SSM_GROUP, CHUNK
    dt = jnp.exp(log_dt.astype(F32))[:, None]
    lr, li = a_re.astype(F32), a_im.astype(F32)
    mag = jnp.exp(lr * dt)
    abar_re, abar_im = mag * jnp.cos(li * dt), mag * jnp.sin(li * dt)
    den = lr * lr + li * li
    nr, ni = abar_re - 1.0, abar_im
    z_re = (nr * lr + ni * li) / den
    z_im = (ni * lr - nr * li) / den
    br, bi = b_re.astype(F32), b_im.astype(F32)
    bbar_re = z_re[..., None] * br - z_im[..., None] * bi
    bbar_im = z_re[..., None] * bi + z_im[..., None] * br
    cr, ci = c_re.astype(F32), c_im.astype(F32)

    tau = jnp.arange(t + 1, dtype=F32)[None, :, None]
    pmag = jnp.exp(tau * (lr * dt)[:, None, :])
    pw_re = pmag * jnp.cos(tau * (li * dt)[:, None, :])
    pw_im = pmag * jnp.sin(tau * (li * dt)[:, None, :])

    ca_re = cr[:, None] * pw_re[:, :, None, :] - ci[:, None] * pw_im[:, :, None, :]
    ca_im = cr[:, None] * pw_im[:, :, None, :] + ci[:, None] * pw_re[:, :, None, :]

    lhs = jnp.concatenate([ca_re[:, :t], ca_im[:, :t]], axis=-1).reshape(g, t * h, 2 * p)
    rhs = jnp.concatenate([bbar_re, -bbar_im], axis=1)
    taps = _s5_taps(lhs, rhs).reshape(g, t, h, h)
    taps = taps.at[:, 0].add(jax.vmap(jnp.diag)(d_skip.astype(F32)))

    gt = GROUPS_PER_TILE
    nt = g // gt
    eye = jnp.eye(gt, dtype=F32)

    idx = jnp.arange(t)
    lag = idx[None, :] - idx[:, None]
    toep = jnp.where((lag >= 0)[None, :, :, None, None], taps[:, jnp.clip(lag, 0, t - 1)], 0.0)
    m_loc = jnp.einsum('qgjiab,gk->qjgbika', toep.reshape(nt, gt, t, t, h, h), eye)
    m_loc = m_loc.reshape(nt, t * gt * h, t * gt * h)

    pj_re, pj_im = pw_re[:, t - 1 - idx], pw_im[:, t - 1 - idx]
    bt_re, bt_im = bbar_re.transpose(0, 2, 1)[:, None], bbar_im.transpose(0, 2, 1)[:, None]
    win_re = pj_re[:, :, None, :] * bt_re - pj_im[:, :, None, :] * bt_im
    win_im = pj_re[:, :, None, :] * bt_im + pj_im[:, :, None, :] * bt_re
    win = jnp.stack([win_re, win_im], axis=1).reshape(nt, gt, 2, t, h, p)
    w_in = jnp.einsum('qgcjap,gk->qjgackp', win, eye).reshape(nt, t * gt * h, 2 * gt * p)

    wo = jnp.stack([ca_re[:, 1:], -ca_im[:, 1:]], axis=1).reshape(nt, gt, 2, t, h, p)
    w_out = jnp.einsum('qgciap,gk->qcgpika', wo, eye).reshape(nt, 2 * gt * p, t * gt * h)

    kk = (t * jnp.arange(2 * SUBLANES, dtype=F32))[:, None, None]
    cmag = jnp.exp(kk * (lr * dt)[None])
    a_chunk_re = (cmag * jnp.cos(kk * (li * dt)[None])).reshape(2 * SUBLANES, g * p)
    a_chunk_im = (cmag * jnp.sin(kk * (li * dt)[None])).reshape(2 * SUBLANES, g * p)
    return m_loc.astype(BF16), w_in.astype(BF16), w_out.astype(BF16), a_chunk_re, a_chunk_im


def _chunk_lhs(u_ref):
    return jnp.concatenate([u_ref[j] for j in range(CHUNK)], axis=1)


def _s5_state_in_kernel(u_ref, w_ref, sre_ref, sim_ref):
    s = jnp.dot(_chunk_lhs(u_ref), w_ref[0], preferred_element_type=F32)
    half = s.shape[1] // 2
    sre_ref[...] = s[:, :half]
    sim_ref[...] = s[:, half:]


def _s5_state_in(us3, w_in, *, n_state, rb):
    t, r, _ = us3.shape
    nt, k, n = w_in.shape
    rb = min(rb, r)
    return pl.pallas_call(
        _s5_state_in_kernel,
        grid=(nt, r // rb),
        in_specs=[pl.BlockSpec((t, rb, LANES), lambda q, i: (0, i, q)),
                  pl.BlockSpec((1, k, n), lambda q, i: (q, 0, 0))],
        out_specs=[pl.BlockSpec((rb, n // 2), lambda q, i: (i, q)),
                   pl.BlockSpec((rb, n // 2), lambda q, i: (i, q))],
        out_shape=[jax.ShapeDtypeStruct((r, n_state), F32)] * 2,
        compiler_params=_cparams("parallel", "arbitrary"),
        name="s5_state_in",
    )(us3, w_in)


def _s5_recurrence_kernel(sre_ref, sim_ref, pre_ref, pim_ref, hre_ref, him_ref, *, bsz, n_chunks):
    tn = sre_ref.shape[1]
    shape = (SUBLANES, tn)
    row = lax.broadcasted_iota(jnp.int32, shape, 0)
    pw_re = pre_ref[0:SUBLANES, :]
    pw_im = pim_ref[0:SUBLANES, :]
    a_tile_re = jnp.broadcast_to(pre_ref[SUBLANES:SUBLANES + 1, :], shape)
    a_tile_im = jnp.broadcast_to(pim_ref[SUBLANES:SUBLANES + 1, :], shape)

    def tile_step(it, carry):
        out = []
        for b in range(bsz):
            h_re, h_im = carry[2 * b], carry[2 * b + 1]
            rows = pl.ds(pl.multiple_of(b * n_chunks + it * SUBLANES, SUBLANES), SUBLANES)
            y_re, y_im = sre_ref[rows, :], sim_ref[rows, :]
            for d in (1, 2, 4):
                m_re = jnp.broadcast_to(pre_ref[d:d + 1, :], shape)
                m_im = jnp.broadcast_to(pim_ref[d:d + 1, :], shape)
                s_re = jnp.where(row >= d, pltpu.roll(y_re, d, axis=0), 0.0)
                s_im = jnp.where(row >= d, pltpu.roll(y_im, d, axis=0), 0.0)
                y_re, y_im = (y_re + m_re * s_re - m_im * s_im,
                              y_im + m_re * s_im + m_im * s_re)
            e_re = jnp.where(row >= 1, pltpu.roll(y_re, 1, axis=0), 0.0)
            e_im = jnp.where(row >= 1, pltpu.roll(y_im, 1, axis=0), 0.0)
            hre_ref[rows, :] = pw_re * h_re - pw_im * h_im + e_re
            him_ref[rows, :] = pw_re * h_im + pw_im * h_re + e_im
            last_re = jnp.broadcast_to(y_re[SUBLANES - 1:SUBLANES, :], shape)
            last_im = jnp.broadcast_to(y_im[SUBLANES - 1:SUBLANES, :], shape)
            out.append(a_tile_re * h_re - a_tile_im * h_im + last_re)
            out.append(a_tile_re * h_im + a_tile_im * h_re + last_im)
        return tuple(out)

    zero = jnp.zeros(shape, F32)
    lax.fori_loop(0, n_chunks // SUBLANES, tile_step, (zero,) * (2 * bsz))


def _s5_recurrence(s_re, s_im, pw_re, pw_im, *, bsz, tn):
    r, n = s_re.shape
    tn = min(tn, n)
    spec = pl.BlockSpec((r, tn), lambda j: (0, j))
    pspec = pl.BlockSpec((pw_re.shape[0], tn), lambda j: (0, j))
    return pl.pallas_call(
        functools.partial(_s5_recurrence_kernel, bsz=bsz, n_chunks=r // bsz),
        grid=(n // tn,),
        in_specs=[spec, spec, pspec, pspec],
        out_specs=[spec, spec],
        out_shape=[jax.ShapeDtypeStruct((r, n), F32)] * 2,
        compiler_params=_cparams("parallel"),
        name="s5_recurrence",
    )(s_re, s_im, pw_re, pw_im)


def _s5_out_kernel(u_ref, m_ref, hre_ref, him_ref, w_ref, y_ref):
    h = jnp.concatenate([hre_ref[...], him_ref[...]], axis=1).astype(BF16)
    y = jnp.dot(_chunk_lhs(u_ref), m_ref[0], preferred_element_type=F32)
    y += jnp.dot(h, w_ref[0], preferred_element_type=F32)
    for i in range(CHUNK):
        y_ref[i] = y[:, i * LANES:(i + 1) * LANES].astype(y_ref.dtype)


def _s5_out(us3, m_loc, h_re, h_im, w_out, *, width, rb):
    t, r, _ = us3.shape
    nt, k, _ = m_loc.shape
    ks = w_out.shape[1]
    rb = min(rb, r)
    return pl.pallas_call(
        _s5_out_kernel,
        grid=(nt, r // rb),
        in_specs=[pl.BlockSpec((t, rb, LANES), lambda q, i: (0, i, q)),
                  pl.BlockSpec((1, k, k), lambda q, i: (q, 0, 0)),
                  pl.BlockSpec((rb, ks // 2), lambda q, i: (i, q)),
                  pl.BlockSpec((rb, ks // 2), lambda q, i: (i, q)),
                  pl.BlockSpec((1, ks, k), lambda q, i: (q, 0, 0))],
        out_specs=pl.BlockSpec((t, rb, LANES), lambda q, i: (0, i, q)),
        out_shape=jax.ShapeDtypeStruct((t, r, width), BF16),
        compiler_params=_cparams("parallel", "arbitrary"),
        name="s5_out",
    )(us3, m_loc, h_re, h_im, w_out)


def _s5_glu_kernel(y_ref, gate_ref, w_ref, b_ref, pt_ref, o_ref):
    t, rb, w = y_ref.shape
    y = jax.nn.gelu(y_ref[...].reshape(t * rb, w).astype(F32))
    z = jnp.dot(y.astype(BF16), w_ref[...], preferred_element_type=F32) + b_ref[...]
    y = y * jax.nn.sigmoid(z)
    y = (y * jax.nn.silu(gate_ref[...].reshape(t * rb, w).astype(F32))).astype(BF16)
    o_ref[...] = jnp.dot(pt_ref[...], y, preferred_element_type=F32).astype(o_ref.dtype)


def _s5_glu(y3, us3, w_glu, b_glu, *, rb):
    t, r, w = y3.shape
    rb = min(rb, r)
    rows = t * rb
    return pl.pallas_call(
        _s5_glu_kernel,
        grid=(r // rb,),
        in_specs=[pl.BlockSpec((t, rb, w), lambda i: (0, i, 0)),
                  pl.BlockSpec((t, rb, w), lambda i: (0, i, 1)),
                  pl.BlockSpec((w, w), lambda i: (0, 0)),
                  pl.BlockSpec((1, w), lambda i: (0, 0)),
                  pl.BlockSpec((rows, rows), lambda i: (0, 0))],
        out_specs=pl.BlockSpec((rows, w), lambda i: (i, 0)),
        out_shape=jax.ShapeDtypeStruct((t * r, w), BF16),
        compiler_params=_cparams("parallel"),
        name="s5_glu",
    )(y3, us3, w_glu, b_glu.reshape(1, w), _chunk_major_perm(rows).T)


def _out_proj_kernel(ys_ref, yd_ref, yx_ref, w_ref, x_ref, g_ref, o_ref):
    ws, wd = ys_ref.shape[1], yd_ref.shape[1]
    mix = jnp.dot(ys_ref[...], w_ref[0:ws, :], preferred_element_type=F32)
    mix += jnp.dot(yd_ref[...], w_ref[ws:ws + wd, :], preferred_element_type=F32)
    mix += jnp.dot(yx_ref[...], w_ref[ws + wd:, :], preferred_element_type=F32)
    var = jnp.mean(mix * mix, axis=-1, keepdims=True)
    o_ref[...] = x_ref[...] + mix * lax.rsqrt(var + NORM_EPS) * g_ref[...]


def _out_proj(y_s, y_d, y_x, w_out, x, g_post, *, tm):
    m, d = x.shape
    tm = min(tm, m)
    row = lambda i: (i, 0)
    const = lambda i: (0, 0)
    return pl.pallas_call(
        _out_proj_kernel,
        grid=(m // tm,),
        in_specs=[pl.BlockSpec((tm, y_s.shape[1]), row),
                  pl.BlockSpec((tm, y_d.shape[1]), row),
                  pl.BlockSpec((tm, y_x.shape[1]), row),
                  pl.BlockSpec(w_out.shape, const),
                  pl.BlockSpec((tm, d), row),
                  pl.BlockSpec((1, d), const)],
        out_specs=pl.BlockSpec((tm, d), row),
        out_shape=jax.ShapeDtypeStruct((m, d), F32),
        compiler_params=_cparams("parallel"),
        name="out_proj",
    )(y_s, y_d, y_x, w_out, x, g_post.reshape(1, d))


def kernel(x, mem, positions, norm_pre, norm_post, norm_mem, w_in, w_out, w_mem_kv, ssm_a_re, ssm_a_im, ssm_log_dt, ssm_b_re, ssm_b_im, ssm_c_re, ssm_c_im, ssm_d, w_glu, b_glu, diff_lq1, diff_lk1, diff_lq2, diff_lk2, diff_subln):
    bsz, seq, d_model = x.shape
    mem_tokens = mem.shape[1]
    depth = w_in.shape[0]
    m = bsz * seq
    ssm_width = ssm_a_re.shape[1] * SSM_GROUP
    n_groups = ssm_width // SSM_GROUP
    diff_width = DIFF_HEADS * 2 * DIFF_HEAD_DIM
    xattn_width = XATTN_HEADS * XATTN_HEAD_DIM
    n_chunks = seq // CHUNK
    n_s5_cols = 2 * ssm_width
    col_qd = 0
    col_kd = col_qd + diff_width
    col_vd = col_kd + diff_width
    col_gd = col_vd + diff_width
    col_qx = col_gd + diff_width
    col_gx = col_qx + xattn_width
    assert col_kd == col_qd + diff_width and col_qd % diff_width == 0

    inv = ROPE_THETA ** (-jnp.arange(0, DIFF_HEAD_DIM, 2, dtype=F32) / DIFF_HEAD_DIM)
    ang = positions.astype(F32).reshape(m, 1) * inv
    cos, sin = jnp.cos(ang), jnp.sin(ang)
    cos_t = jnp.concatenate([cos, cos, cos, cos], axis=-1)
    sin_t = jnp.concatenate([-sin, sin, -sin, sin], axis=-1)

    xf = x.reshape(m, d_model)
    mem_f = mem.reshape(bsz * mem_tokens, d_model)
    for l in range(depth):
        lambda_init = 0.8 - 0.6 * math.exp(-0.3 * l)
        us3, proj = _in_proj(xf, norm_pre[l], w_in[l].astype(BF16), n_perm_cols=n_s5_cols, tm=512, tn=1024)

        m_loc, w_sin, w_sout, a_re, a_im = _s5_weights(
            ssm_a_re[l], ssm_a_im[l], ssm_log_dt[l], ssm_b_re[l], ssm_b_im[l],
            ssm_c_re[l], ssm_c_im[l], ssm_d[l])
        s_re, s_im = _s5_state_in(us3, w_sin, n_state=n_groups * SSM_STATE, rb=512)
        h_re, h_im = _s5_recurrence(s_re, s_im, a_re, a_im, bsz=bsz, tn=1024)
        y3 = _s5_out(us3, m_loc, h_re, h_im, w_sout, width=ssm_width, rb=512)
        y_s = _s5_glu(y3, us3, w_glu[l].astype(BF16), b_glu[l], rb=64)

        q_rot, k_rot = _rope(proj, cos_t, sin_t, q_col_block=col_qd // diff_width, tm=1024)
        lam = (jnp.exp(jnp.sum(diff_lq1[l].astype(F32) * diff_lk1[l].astype(F32)))
               - jnp.exp(jnp.sum(diff_lq2[l].astype(F32) * diff_lk2[l].astype(F32))) + lambda_init)
        sg_vec = (diff_subln[l].astype(F32) * (1.0 - lambda_init)).reshape(1, LANES)
        v_t = proj[:, col_vd:col_vd + diff_width].reshape(bsz, seq, DIFF_HEADS, 2 * DIFF_HEAD_DIM)
        v_t = v_t.transpose(0, 2, 3, 1)
        vt_ext = jnp.concatenate([v_t, jnp.ones((bsz, DIFF_HEADS, BF16_SUBLANES, seq), BF16)], axis=2)
        vt_ext = vt_ext.reshape(bsz * DIFF_HEADS, 2 * DIFF_HEAD_DIM + BF16_SUBLANES, seq)
        y_d = _diff_attn(q_rot, k_rot, vt_ext, proj, lam.reshape(1), sg_vec, bsz=bsz, seq=seq,
                         g_col=col_gd // LANES, tq=512)

        mem_kv = _norm_matmul(mem_f, norm_mem[l], w_mem_kv[l].astype(BF16), tm=512, tn=1024, name="mem_kv")
        y_x = _mem_attn(proj, mem_kv, bsz=bsz, seq=seq, mem_tokens=mem_tokens,
                        q_col_block=col_qx // xattn_width, g_col_block=col_gx // xattn_width, tq=1024)

        xf = _out_proj(y_s, y_d, y_x, w_out[l].astype(BF16), xf, norm_post[l], tm=512)
    return xf.reshape(bsz, seq, d_model)
```

```python
import functools
import math

import jax
import jax.numpy as jnp
import numpy as np
from jax import lax
from jax.experimental import pallas as pl
from jax.experimental.pallas import tpu as pltpu

F32 = jnp.float32
BF16 = jnp.bfloat16

SSM_GROUP = 16
SSM_STATE = 64
CHUNK = 16
DIFF_HEADS = 4
DIFF_HEAD_DIM = 64
XATTN_HEADS = 4
XATTN_HEAD_DIM = 128
ROPE_THETA = 10000.0
NORM_EPS = 1e-6
MASK_VALUE = -1e30
LANES = 128
SUBLANES = 8
BF16_SUBLANES = 16
GROUPS_PER_TILE = LANES // SSM_GROUP
VMEM_LIMIT = 56 * 1024 * 1024


def _cparams(*sem):
    return pltpu.CompilerParams(dimension_semantics=sem, vmem_limit_bytes=VMEM_LIMIT)


def _norm_matmul_kernel(x_ref, g_ref, w_ref, o_ref, h_ref):
    @pl.when(pl.program_id(1) == 0)
    def _():
        x = x_ref[...]
        var = jnp.mean(x * x, axis=-1, keepdims=True)
        h_ref[...] = (x * lax.rsqrt(var + NORM_EPS) * g_ref[...]).astype(BF16)

    o_ref[...] = jnp.dot(h_ref[...], w_ref[...], preferred_element_type=F32).astype(o_ref.dtype)


def _norm_matmul(x, g, w, *, tm, tn, name):
    m, d = x.shape
    n = w.shape[1]
    tm, tn = min(tm, m), min(tn, n)
    return pl.pallas_call(
        _norm_matmul_kernel,
        grid=(m // tm, n // tn),
        in_specs=[pl.BlockSpec((tm, d), lambda i, j: (i, 0)),
                  pl.BlockSpec((1, d), lambda i, j: (0, 0)),
                  pl.BlockSpec((d, tn), lambda i, j: (0, j))],
        out_specs=pl.BlockSpec((tm, tn), lambda i, j: (i, j)),
        out_shape=jax.ShapeDtypeStruct((m, n), BF16),
        scratch_shapes=[pltpu.VMEM((tm, d), BF16)],
        compiler_params=_cparams("parallel", "arbitrary"),
        name=name,
    )(x, g.reshape(1, d), w)


def _chunk_major_perm(rows):
    rb = rows // CHUNK
    src = (np.arange(rows) % rb) * CHUNK + np.arange(rows) // rb
    return jnp.asarray(np.eye(rows, dtype=np.float32)[src], BF16)


def _in_proj_kernel(x_ref, g_ref, w_ref, p_ref, o3_ref, o_ref, hp_ref, hn_ref, *, n_perm):
    j = pl.program_id(1)
    rb = x_ref.shape[0] // CHUNK

    @pl.when(j == 0)
    def _():
        x = x_ref[...]
        var = jnp.mean(x * x, axis=-1, keepdims=True)
        hn = (x * lax.rsqrt(var + NORM_EPS) * g_ref[...]).astype(BF16)
        hn_ref[...] = hn
        hp_ref[...] = jnp.dot(p_ref[...], hn, preferred_element_type=F32).astype(BF16)

    @pl.when(j < n_perm)
    def _():
        res = jnp.dot(hp_ref[...], w_ref[...], preferred_element_type=F32)
        for i in range(CHUNK):
            o3_ref[i] = res[i * rb:(i + 1) * rb].astype(o3_ref.dtype)

    @pl.when(j >= n_perm)
    def _():
        o_ref[...] = jnp.dot(hn_ref[...], w_ref[...], preferred_element_type=F32).astype(o_ref.dtype)


def _in_proj(x, g, w, *, n_perm_cols, tm, tn):
    m, d = x.shape
    n = w.shape[1]
    tm = min(tm, m)
    n_perm = n_perm_cols // tn
    rb = tm // CHUNK
    return pl.pallas_call(
        functools.partial(_in_proj_kernel, n_perm=n_perm),
        grid=(m // tm, n // tn),
        in_specs=[pl.BlockSpec((tm, d), lambda i, j: (i, 0)),
                  pl.BlockSpec((1, d), lambda i, j: (0, 0)),
                  pl.BlockSpec((d, tn), lambda i, j: (0, j)),
                  pl.BlockSpec((tm, tm), lambda i, j: (0, 0))],
        out_specs=[pl.BlockSpec((CHUNK, rb, tn), lambda i, j: (0, i, jnp.minimum(j, n_perm - 1))),
                   pl.BlockSpec((tm, tn), lambda i, j: (i, jnp.maximum(j - n_perm, 0)))],
        out_shape=[jax.ShapeDtypeStruct((CHUNK, m // CHUNK, n_perm_cols), BF16),
                   jax.ShapeDtypeStruct((m, n - n_perm_cols), BF16)],
        scratch_shapes=[pltpu.VMEM((tm, d), BF16), pltpu.VMEM((tm, d), BF16)],
        compiler_params=_cparams("parallel", "arbitrary"),
        name="in_proj",
    )(x, g.reshape(1, d), w, _chunk_major_perm(tm))


def _rope_kernel(q_ref, k_ref, cos_ref, sin_ref, qo_ref, ko_ref, *, q_scale):
    cos = cos_ref[...]
    sin = sin_ref[...]
    lane = lax.broadcasted_iota(jnp.int32, cos.shape, 1)
    first_half = (lane % DIFF_HEAD_DIM) < (DIFF_HEAD_DIM // 2)
    for x_ref, o_ref, mult in ((q_ref, qo_ref, q_scale), (k_ref, ko_ref, 1.0)):
        for h in range(DIFF_HEADS):
            x = x_ref[:, h * LANES:(h + 1) * LANES].astype(F32)
            partner = jnp.where(first_half,
                                pltpu.roll(x, LANES - DIFF_HEAD_DIM // 2, axis=1),
                                pltpu.roll(x, DIFF_HEAD_DIM // 2, axis=1))
            o_ref[:, h * LANES:(h + 1) * LANES] = ((x * cos + partner * sin) * mult).astype(o_ref.dtype)


def _rope(proj, cos_t, sin_t, *, q_col_block, tm):
    m = proj.shape[0]
    tm = min(tm, m)
    w = DIFF_HEADS * LANES
    q_scale = DIFF_HEAD_DIM ** -0.5 * math.log2(math.e)
    return pl.pallas_call(
        functools.partial(_rope_kernel, q_scale=q_scale),
        grid=(m // tm,),
        in_specs=[pl.BlockSpec((tm, w), lambda i: (i, q_col_block)),
                  pl.BlockSpec((tm, w), lambda i: (i, q_col_block + 1)),
                  pl.BlockSpec((tm, LANES), lambda i: (i, 0)),
                  pl.BlockSpec((tm, LANES), lambda i: (i, 0))],
        out_specs=[pl.BlockSpec((tm, w), lambda i: (i, 0))] * 2,
        out_shape=[jax.ShapeDtypeStruct((m, w), BF16)] * 2,
        compiler_params=_cparams("parallel"),
        name="rope",
    )(proj, proj, cos_t, sin_t)


def _diff_attn_kernel(q_ref, k_ref, vt_ref, g_ref, lam_ref, sg_ref, o_ref, acc1, acc2, *, tq):
    qi = pl.program_id(2)
    q = q_ref[...]
    lane = lax.broadcasted_iota(jnp.int32, q.shape, 1)
    q_maps = (jnp.where(lane < DIFF_HEAD_DIM, q, jnp.zeros_like(q)),
              jnp.where(lane >= DIFF_HEAD_DIM, q, jnp.zeros_like(q)))
    accs = (acc1, acc2)
    acc1[...] = jnp.zeros_like(acc1)
    acc2[...] = jnp.zeros_like(acc2)

    def block(j, ms, masked):
        off = pl.multiple_of(j * tq, tq)
        k = k_ref[pl.ds(off, tq), :]
        vt = vt_ref[0, :, pl.ds(off, tq)]
        out = []
        for qm, m_old, acc in zip(q_maps, ms, accs):
            s = lax.dot_general(k, qm, (((1,), (1,)), ((), ())), preferred_element_type=F32)
            if masked:
                key = lax.broadcasted_iota(jnp.int32, s.shape, 0)
                qry = lax.broadcasted_iota(jnp.int32, s.shape, 1)
                s = jnp.where(key <= qry, s, MASK_VALUE)
            m_new = jnp.maximum(m_old, jnp.max(s, axis=0, keepdims=True))
            p = jnp.exp2(s - m_new)
            acc[...] = acc[...] * jnp.exp2(m_old - m_new) + jnp.dot(vt, p.astype(BF16),
                                                                    preferred_element_type=F32)
            out.append(m_new)
        return tuple(out)

    m0 = jnp.full((1, tq), MASK_VALUE, F32)
    ms = lax.fori_loop(0, qi, lambda j, ms: block(j, ms, False), (m0, m0))
    block(qi, ms, True)

    d = 2 * DIFF_HEAD_DIM
    a1, a2 = acc1[...], acc2[...]
    o_t = a1[:d] / a1[d:d + 1] - lam_ref[0] * (a2[:d] / a2[d:d + 1])
    o = o_t.T
    var = jnp.mean(o * o, axis=-1, keepdims=True)
    o = o * lax.rsqrt(var + NORM_EPS) * sg_ref[...]
    o_ref[...] = (o * jax.nn.silu(g_ref[...].astype(F32))).astype(o_ref.dtype)


def _diff_attn(q_rot, k_rot, vt_ext, proj, lam, sg_vec, *, bsz, seq, g_col, tq):
    tq = min(tq, seq)
    nq = seq // tq
    v_rows = vt_ext.shape[1]
    return pl.pallas_call(
        functools.partial(_diff_attn_kernel, tq=tq),
        grid=(bsz, DIFF_HEADS, nq),
        in_specs=[
            pl.BlockSpec((tq, LANES), lambda b, h, i: (b * nq + i, h)),
            pl.BlockSpec((seq, LANES), lambda b, h, i: (b, h)),
            pl.BlockSpec((1, v_rows, seq), lambda b, h, i: (b * DIFF_HEADS + h, 0, 0)),
            pl.BlockSpec((tq, LANES), lambda b, h, i: (b * nq + i, g_col + h)),
            pl.BlockSpec(memory_space=pltpu.SMEM),
            pl.BlockSpec((1, LANES), lambda b, h, i: (0, 0)),
        ],
        out_specs=pl.BlockSpec((tq, LANES), lambda b, h, i: (b * nq + i, h)),
        out_shape=jax.ShapeDtypeStruct((bsz * seq, DIFF_HEADS * LANES), BF16),
        scratch_shapes=[pltpu.VMEM((v_rows, tq), F32), pltpu.VMEM((v_rows, tq), F32)],
        compiler_params=_cparams("parallel", "parallel", "arbitrary"),
        name="diff_attn",
    )(q_rot, k_rot, vt_ext, proj, lam, sg_vec)


def _mem_attn_kernel(q_ref, g_ref, mk_ref, mv_ref, o_ref):
    scale = XATTN_HEAD_DIM ** -0.5
    for h in range(XATTN_HEADS):
        sl = slice(h * XATTN_HEAD_DIM, (h + 1) * XATTN_HEAD_DIM)
        s = lax.dot_general(q_ref[:, sl], mk_ref[:, sl], (((1,), (1,)), ((), ())),
                            preferred_element_type=F32) * scale
        p = jnp.exp(s - jnp.max(s, axis=-1, keepdims=True))
        p = p / jnp.sum(p, axis=-1, keepdims=True)
        o = jnp.dot(p.astype(BF16), mv_ref[:, sl], preferred_element_type=F32)
        o_ref[:, sl] = (o * jax.nn.silu(g_ref[:, sl].astype(F32))).astype(o_ref.dtype)


def _mem_attn(proj, mem_kv, *, bsz, seq, mem_tokens, q_col_block, g_col_block, tq):
    tq = min(tq, seq)
    nq = seq // tq
    w = XATTN_HEADS * XATTN_HEAD_DIM
    return pl.pallas_call(
        _mem_attn_kernel,
        grid=(bsz, nq),
        in_specs=[pl.BlockSpec((tq, w), lambda b, i: (b * nq + i, q_col_block)),
                  pl.BlockSpec((tq, w), lambda b, i: (b * nq + i, g_col_block)),
                  pl.BlockSpec((mem_tokens, w), lambda b, i: (b, 0)),
                  pl.BlockSpec((mem_tokens, w), lambda b, i: (b, 1))],
        out_specs=pl.BlockSpec((tq, w), lambda b, i: (b * nq + i, 0)),
        out_shape=jax.ShapeDtypeStruct((bsz * seq, w), BF16),
        compiler_params=_cparams("parallel", "parallel"),
        name="mem_attn",
    )(proj, proj, mem_kv, mem_kv)


def _s5_taps_kernel(lhs_ref, rhs_ref, o_ref):
    o_ref[0] = jnp.dot(lhs_ref[0], rhs_ref[0], preferred_element_type=F32,
                       precision=lax.Precision.HIGHEST)


def _s5_taps(lhs, rhs):
    g, r, k = lhs.shape
    n = rhs.shape[2]
    return pl.pallas_call(
        _s5_taps_kernel,
        grid=(g,),
        in_specs=[pl.BlockSpec((1, r, k), lambda i: (i, 0, 0)),
                  pl.BlockSpec((1, k, n), lambda i: (i, 0, 0))],
        out_specs=pl.BlockSpec((1, r, n), lambda i: (i, 0, 0)),
        out_shape=jax.ShapeDtypeStruct((g, r, n), F32),
        compiler_params=_cparams("parallel"),
        name="s5_taps",
    )(lhs, rhs)


def _s5_weights(a_re, a_im, log_dt, b_re, b_im, c_re, c_im, d_skip):
    g = a_re.shape[0]
    p, h, t = SSM_STATE, SSM_GROUP, CHUNK
    dt = jnp.exp(log_dt.astype(F32))[:, None]
    lr, li = a_re.astype(F32), a_im.astype(F32)
    mag = jnp.exp(lr * dt)
    abar_re, abar_im = mag * jnp.cos(li * dt), mag * jnp.sin(li * dt)
    den = lr * lr + li * li
    nr, ni = abar_re - 1.0, abar_im
    z_re = (nr * lr + ni * li) / den
    z_im = (ni * lr - nr * li) / den
    br, bi = b_re.astype(F32), b_im.astype(F32)
    bbar_re = z_re[..., None] * br - z_im[..., None] * bi
    bbar_im = z_re[..., None] * bi + z_im[..., None] * br
    cr, ci = c_re.astype(F32), c_im.astype(F32)

    tau = jnp.arange(t + 1, dtype=F32)[None, :, None]
    pmag = jnp.exp(tau * (lr * dt)[:, None, :])
    pw_re = pmag * jnp.cos(tau * (li * dt)[:, None, :])
    pw_im = pmag * jnp.sin(tau * (li * dt)[:, None, :])

    ca_re = cr[:, None] * pw_re[:, :, None, :] - ci[:, None] * pw_im[:, :, None, :]
    ca_im = cr[:, None] * pw_im[:, :, None, :] + ci[:, None] * pw_re[:, :, None, :]

    lhs = jnp.concatenate([ca_re[:, :t], ca_im[:, :t]], axis=-1).reshape(g, t * h, 2 * p)
    rhs = jnp.concatenate([bbar_re, -bbar_im], axis=1)
    taps = _s5_taps(lhs, rhs).reshape(g, t, h, h)
    taps = taps.at[:, 0].add(jax.vmap(jnp.diag)(d_skip.astype(F32)))

    gt = GROUPS_PER_TILE
    nt = g // gt
    idx = jnp.arange(t)

    k_cat = taps.transpose(0, 3, 1, 2).reshape(nt, gt * h, t * h)

    pj_re, pj_im = pw_re[:, t - 1 - idx], pw_im[:, t - 1 - idx]
    bt_re, bt_im = bbar_re.transpose(0, 2, 1)[:, None], bbar_im.transpose(0, 2, 1)[:, None]
    win_re = pj_re[:, :, None, :] * bt_re - pj_im[:, :, None, :] * bt_im
    win_im = pj_re[:, :, None, :] * bt_im + pj_im[:, :, None, :] * bt_re
    win = jnp.stack([win_re, win_im], axis=3).reshape(nt, gt, t, h, 2, p)
    win_cat = win.transpose(0, 2, 1, 3, 4, 5).reshape(nt, t * gt * h, 2 * p)

    wo = jnp.stack([ca_re[:, 1:], -ca_im[:, 1:]], axis=1).reshape(nt, gt, 2, t, h, p)
    wo_cat = wo.transpose(0, 2, 1, 5, 3, 4).reshape(nt, 2 * gt * p, t * h)

    kk = (t * jnp.arange(2 * SUBLANES, dtype=F32))[:, None, None]
    cmag = jnp.exp(kk * (lr * dt)[None])
    a_chunk_re = (cmag * jnp.cos(kk * (li * dt)[None])).reshape(2 * SUBLANES, g * p)
    a_chunk_im = (cmag * jnp.sin(kk * (li * dt)[None])).reshape(2 * SUBLANES, g * p)
    return k_cat.astype(BF16), win_cat.astype(BF16), wo_cat.astype(BF16), a_chunk_re, a_chunk_im


def _lane_repeat(blocks, width, copies):
    r = np.kron(np.eye(blocks, dtype=np.float32),
                np.kron(np.ones((1, copies), np.float32), np.eye(width, dtype=np.float32)))
    return jnp.asarray(r, BF16)


def _group_of(index, period, size):
    assert period & (period - 1) == 0 and size & (size - 1) == 0
    return lax.shift_right_logical(index & (period - 1), size.bit_length() - 1)


def _expand_block_diag(compact, rep, row_group, col_group):
    e = jnp.dot(compact, rep, preferred_element_type=F32)
    r = lax.broadcasted_iota(jnp.int32, e.shape, 0)
    c = lax.broadcasted_iota(jnp.int32, e.shape, 1)
    return jnp.where(row_group(r) == col_group(c), e, 0.0).astype(BF16)


def _chunk_lhs(u_ref):
    return jnp.concatenate([u_ref[j] for j in range(CHUNK)], axis=1)


def _s5_state_in_kernel(u_ref, wc_ref, rep_ref, sre_ref, sim_ref, w_scr):
    @pl.when(pl.program_id(1) == 0)
    def _():
        w_scr[...] = _expand_block_diag(
            wc_ref[0], rep_ref[...],
            lambda r: _group_of(r, LANES, SSM_GROUP),
            lambda c: _group_of(c, GROUPS_PER_TILE * SSM_STATE, SSM_STATE))

    s = jnp.dot(_chunk_lhs(u_ref), w_scr[...], preferred_element_type=F32)
    half = s.shape[1] // 2
    sre_ref[...] = s[:, :half]
    sim_ref[...] = s[:, half:]


def _s5_state_in(us3, win_cat, *, rb):
    t, r, _ = us3.shape
    nt, k, nc = win_cat.shape
    n = nc * GROUPS_PER_TILE
    rb = min(rb, r)
    rep = _lane_repeat(2, SSM_STATE, GROUPS_PER_TILE)
    return pl.pallas_call(
        _s5_state_in_kernel,
        grid=(nt, r // rb),
        in_specs=[pl.BlockSpec((t, rb, LANES), lambda q, i: (0, i, q)),
                  pl.BlockSpec((1, k, nc), lambda q, i: (q, 0, 0)),
                  pl.BlockSpec(rep.shape, lambda q, i: (0, 0))],
        out_specs=[pl.BlockSpec((rb, n // 2), lambda q, i: (i, q)),
                   pl.BlockSpec((rb, n // 2), lambda q, i: (i, q))],
        out_shape=[jax.ShapeDtypeStruct((r, nt * n // 2), F32)] * 2,
        scratch_shapes=[pltpu.VMEM((k, n), BF16)],
        compiler_params=_cparams("parallel", "arbitrary"),
        name="s5_state_in",
    )(us3, win_cat, rep)


def _s5_recurrence_kernel(sre_ref, sim_ref, pre_ref, pim_ref, hre_ref, him_ref, *, bsz, n_chunks):
    tn = sre_ref.shape[1]
    shape = (SUBLANES, tn)
    row = lax.broadcasted_iota(jnp.int32, shape, 0)
    pw_re = pre_ref[0:SUBLANES, :]
    pw_im = pim_ref[0:SUBLANES, :]
    a_tile_re = jnp.broadcast_to(pre_ref[SUBLANES:SUBLANES + 1, :], shape)
    a_tile_im = jnp.broadcast_to(pim_ref[SUBLANES:SUBLANES + 1, :], shape)

    def tile_step(it, carry):
        out = []
        for b in range(bsz):
            h_re, h_im = carry[2 * b], carry[2 * b + 1]
            rows = pl.ds(pl.multiple_of(b * n_chunks + it * SUBLANES, SUBLANES), SUBLANES)
            y_re, y_im = sre_ref[rows, :], sim_ref[rows, :]
            for d in (1, 2, 4):
                m_re = jnp.broadcast_to(pre_ref[d:d + 1, :], shape)
                m_im = jnp.broadcast_to(pim_ref[d:d + 1, :], shape)
                s_re = jnp.where(row >= d, pltpu.roll(y_re, d, axis=0), 0.0)
                s_im = jnp.where(row >= d, pltpu.roll(y_im, d, axis=0), 0.0)
                y_re, y_im = (y_re + m_re * s_re - m_im * s_im,
                              y_im + m_re * s_im + m_im * s_re)
            e_re = jnp.where(row >= 1, pltpu.roll(y_re, 1, axis=0), 0.0)
            e_im = jnp.where(row >= 1, pltpu.roll(y_im, 1, axis=0), 0.0)
            hre_ref[rows, :] = pw_re * h_re - pw_im * h_im + e_re
            him_ref[rows, :] = pw_re * h_im + pw_im * h_re + e_im
            last_re = jnp.broadcast_to(y_re[SUBLANES - 1:SUBLANES, :], shape)
            last_im = jnp.broadcast_to(y_im[SUBLANES - 1:SUBLANES, :], shape)
            out.append(a_tile_re * h_re - a_tile_im * h_im + last_re)
            out.append(a_tile_re * h_im + a_tile_im * h_re + last_im)
        return tuple(out)

    zero = jnp.zeros(shape, F32)
    lax.fori_loop(0, n_chunks // SUBLANES, tile_step, (zero,) * (2 * bsz))


def _s5_recurrence(s_re, s_im, pw_re, pw_im, *, bsz, tn):
    r, n = s_re.shape
    tn = min(tn, n)
    spec = pl.BlockSpec((r, tn), lambda j: (0, j))
    pspec = pl.BlockSpec((pw_re.shape[0], tn), lambda j: (0, j))
    return pl.pallas_call(
        functools.partial(_s5_recurrence_kernel, bsz=bsz, n_chunks=r // bsz),
        grid=(n // tn,),
        in_specs=[spec, spec, pspec, pspec],
        out_specs=[spec, spec],
        out_shape=[jax.ShapeDtypeStruct((r, n), F32)] * 2,
        compiler_params=_cparams("parallel"),
        name="s5_recurrence",
    )(s_re, s_im, pw_re, pw_im)


def _s5_out_kernel(u_ref, kc_ref, hre_ref, him_ref, wc_ref, rep_ref, y_ref, t_scr, w_scr):
    @pl.when(pl.program_id(1) == 0)
    def _():
        rep = rep_ref[...]
        col_group = lambda c: _group_of(c, LANES, SSM_GROUP)
        taps = _expand_block_diag(kc_ref[0], rep, lambda r: _group_of(r, LANES, SSM_GROUP), col_group)
        t_scr[...] = jnp.zeros_like(t_scr)
        for j in range(CHUNK):
            t_scr[j * LANES:(j + 1) * LANES, j * LANES:] = taps[:, :(CHUNK - j) * LANES]
        w_scr[...] = _expand_block_diag(
            wc_ref[0], rep, lambda r: _group_of(r, GROUPS_PER_TILE * SSM_STATE, SSM_STATE), col_group)

    h = jnp.concatenate([hre_ref[...], him_ref[...]], axis=1).astype(BF16)
    y = jnp.dot(_chunk_lhs(u_ref), t_scr[...], preferred_element_type=F32)
    y += jnp.dot(h, w_scr[...], preferred_element_type=F32)
    for i in range(CHUNK):
        y_ref[i] = y[:, i * LANES:(i + 1) * LANES].astype(y_ref.dtype)


def _s5_out(us3, k_cat, h_re, h_im, wo_cat, *, rb):
    t, r, _ = us3.shape
    nt, ks, kc = wo_cat.shape
    k = t * LANES
    rb = min(rb, r)
    rep = _lane_repeat(t, SSM_GROUP, GROUPS_PER_TILE)
    return pl.pallas_call(
        _s5_out_kernel,
        grid=(nt, r // rb),
        in_specs=[pl.BlockSpec((t, rb, LANES), lambda q, i: (0, i, q)),
                  pl.BlockSpec((1, LANES, kc), lambda q, i: (q, 0, 0)),
                  pl.BlockSpec((rb, ks // 2), lambda q, i: (i, q)),
                  pl.BlockSpec((rb, ks // 2), lambda q, i: (i, q)),
                  pl.BlockSpec((1, ks, kc), lambda q, i: (q, 0, 0)),
                  pl.BlockSpec(rep.shape, lambda q, i: (0, 0))],
        out_specs=pl.BlockSpec((t, rb, LANES), lambda q, i: (0, i, q)),
        out_shape=jax.ShapeDtypeStruct((t, r, nt * LANES), BF16),
        scratch_shapes=[pltpu.VMEM((k, k), BF16), pltpu.VMEM((ks, k), BF16)],
        compiler_params=_cparams("parallel", "arbitrary"),
        name="s5_out",
    )(us3, k_cat, h_re, h_im, wo_cat, rep)


def _s5_glu_kernel(y_ref, gate_ref, w_ref, b_ref, pt_ref, o_ref):
    t, rb, w = y_ref.shape
    y = jax.nn.gelu(y_ref[...].reshape(t * rb, w).astype(F32))
    z = jnp.dot(y.astype(BF16), w_ref[...], preferred_element_type=F32) + b_ref[...]
    y = y * jax.nn.sigmoid(z)
    y = (y * jax.nn.silu(gate_ref[...].reshape(t * rb, w).astype(F32))).astype(BF16)
    o_ref[...] = jnp.dot(pt_ref[...], y, preferred_element_type=F32).astype(o_ref.dtype)


def _s5_glu(y3, us3, w_glu, b_glu, *, rb):
    t, r, w = y3.shape
    rb = min(rb, r)
    rows = t * rb
    return pl.pallas_call(
        _s5_glu_kernel,
        grid=(r // rb,),
        in_specs=[pl.BlockSpec((t, rb, w), lambda i: (0, i, 0)),
                  pl.BlockSpec((t, rb, w), lambda i: (0, i, 1)),
                  pl.BlockSpec((w, w), lambda i: (0, 0)),
                  pl.BlockSpec((1, w), lambda i: (0, 0)),
                  pl.BlockSpec((rows, rows), lambda i: (0, 0))],
        out_specs=pl.BlockSpec((rows, w), lambda i: (i, 0)),
        out_shape=jax.ShapeDtypeStruct((t * r, w), BF16),
        compiler_params=_cparams("parallel"),
        name="s5_glu",
    )(y3, us3, w_glu, b_glu.reshape(1, w), _chunk_major_perm(rows).T)


def _out_proj_kernel(ys_ref, yd_ref, yx_ref, w_ref, x_ref, g_ref, o_ref):
    ws, wd = ys_ref.shape[1], yd_ref.shape[1]
    mix = jnp.dot(ys_ref[...], w_ref[0:ws, :], preferred_element_type=F32)
    mix += jnp.dot(yd_ref[...], w_ref[ws:ws + wd, :], preferred_element_type=F32)
    mix += jnp.dot(yx_ref[...], w_ref[ws + wd:, :], preferred_element_type=F32)
    var = jnp.mean(mix * mix, axis=-1, keepdims=True)
    o_ref[...] = x_ref[...] + mix * lax.rsqrt(var + NORM_EPS) * g_ref[...]


def _out_proj(y_s, y_d, y_x, w_out, x, g_post, *, tm):
    m, d = x.shape
    tm = min(tm, m)
    row = lambda i: (i, 0)
    const = lambda i: (0, 0)
    return pl.pallas_call(
        _out_proj_kernel,
        grid=(m // tm,),
        in_specs=[pl.BlockSpec((tm, y_s.shape[1]), row),
                  pl.BlockSpec((tm, y_d.shape[1]), row),
                  pl.BlockSpec((tm, y_x.shape[1]), row),
                  pl.BlockSpec(w_out.shape, const),
                  pl.BlockSpec((tm, d), row),
                  pl.BlockSpec((1, d), const)],
        out_specs=pl.BlockSpec((tm, d), row),
        out_shape=jax.ShapeDtypeStruct((m, d), F32),
        compiler_params=_cparams("parallel"),
        name="out_proj",
    )(y_s, y_d, y_x, w_out, x, g_post.reshape(1, d))


def kernel(x, mem, positions, norm_pre, norm_post, norm_mem, w_in, w_out, w_mem_kv, ssm_a_re, ssm_a_im, ssm_log_dt, ssm_b_re, ssm_b_im, ssm_c_re, ssm_c_im, ssm_d, w_glu, b_glu, diff_lq1, diff_lk1, diff_lq2, diff_lk2, diff_subln):
    bsz, seq, d_model = x.shape
    mem_tokens = mem.shape[1]
    depth = w_in.shape[0]
    m = bsz * seq
    ssm_width = ssm_a_re.shape[1] * SSM_GROUP
    n_groups = ssm_width // SSM_GROUP
    diff_width = DIFF_HEADS * 2 * DIFF_HEAD_DIM
    xattn_width = XATTN_HEADS * XATTN_HEAD_DIM
    n_chunks = seq // CHUNK
    n_s5_cols = 2 * ssm_width
    col_qd = 0
    col_kd = col_qd + diff_width
    col_vd = col_kd + diff_width
    col_gd = col_vd + diff_width
    col_qx = col_gd + diff_width
    col_gx = col_qx + xattn_width
    assert col_kd == col_qd + diff_width and col_qd % diff_width == 0

    inv = ROPE_THETA ** (-jnp.arange(0, DIFF_HEAD_DIM, 2, dtype=F32) / DIFF_HEAD_DIM)
    ang = positions.astype(F32).reshape(m, 1) * inv
    cos, sin = jnp.cos(ang), jnp.sin(ang)
    cos_t = jnp.concatenate([cos, cos, cos, cos], axis=-1)
    sin_t = jnp.concatenate([-sin, sin, -sin, sin], axis=-1)

    xf = x.reshape(m, d_model)
    mem_f = mem.reshape(bsz * mem_tokens, d_model)
    for l in range(depth):
        lambda_init = 0.8 - 0.6 * math.exp(-0.3 * l)
        us3, proj = _in_proj(xf, norm_pre[l], w_in[l].astype(BF16), n_perm_cols=n_s5_cols, tm=512, tn=1024)

        k_cat, win_cat, wo_cat, a_re, a_im = _s5_weights(
            ssm_a_re[l], ssm_a_im[l], ssm_log_dt[l], ssm_b_re[l], ssm_b_im[l],
            ssm_c_re[l], ssm_c_im[l], ssm_d[l])
        s_re, s_im = _s5_state_in(us3, win_cat, rb=512)
        h_re, h_im = _s5_recurrence(s_re, s_im, a_re, a_im, bsz=bsz, tn=1024)
        y3 = _s5_out(us3, k_cat, h_re, h_im, wo_cat, rb=512)
        y_s = _s5_glu(y3, us3, w_glu[l].astype(BF16), b_glu[l], rb=64)

        q_rot, k_rot = _rope(proj, cos_t, sin_t, q_col_block=col_qd // diff_width, tm=1024)
        lam = (jnp.exp(jnp.sum(diff_lq1[l].astype(F32) * diff_lk1[l].astype(F32)))
               - jnp.exp(jnp.sum(diff_lq2[l].astype(F32) * diff_lk2[l].astype(F32))) + lambda_init)
        sg_vec = (diff_subln[l].astype(F32) * (1.0 - lambda_init)).reshape(1, LANES)
        v_t = proj[:, col_vd:col_vd + diff_width].reshape(bsz, seq, DIFF_HEADS, 2 * DIFF_HEAD_DIM)
        v_t = v_t.transpose(0, 2, 3, 1)
        vt_ext = jnp.concatenate([v_t, jnp.ones((bsz, DIFF_HEADS, BF16_SUBLANES, seq), BF16)], axis=2)
        vt_ext = vt_ext.reshape(bsz * DIFF_HEADS, 2 * DIFF_HEAD_DIM + BF16_SUBLANES, seq)
        y_d = _diff_attn(q_rot, k_rot, vt_ext, proj, lam.reshape(1), sg_vec, bsz=bsz, seq=seq,
                         g_col=col_gd // LANES, tq=512)

        mem_kv = _norm_matmul(mem_f, norm_mem[l], w_mem_kv[l].astype(BF16), tm=512, tn=1024, name="mem_kv")
        y_x = _mem_attn(proj, mem_kv, bsz=bsz, seq=seq, mem_tokens=mem_tokens,
                        q_col_block=col_qx // xattn_width, g_col_block=col_gx // xattn_width, tq=1024)

        xf = _out_proj(y_s, y_d, y_x, w_out[l].astype(BF16), xf, norm_post[l], tm=512)
    return xf.reshape(bsz, seq, d_model)
```

```python
import functools
import math

import jax
import jax.numpy as jnp
import numpy as np
from jax import lax
from jax.experimental import pallas as pl
from jax.experimental.pallas import tpu as pltpu

F32 = jnp.float32
BF16 = jnp.bfloat16

SSM_GROUP = 16
SSM_STATE = 64
CHUNK = 16
DIFF_HEADS = 4
DIFF_HEAD_DIM = 64
XATTN_HEADS = 4
XATTN_HEAD_DIM = 128
ROPE_THETA = 10000.0
NORM_EPS = 1e-6
MASK_VALUE = -1e30
LANES = 128
SUBLANES = 8
BF16_SUBLANES = 16
GROUPS_PER_TILE = LANES // SSM_GROUP
VMEM_LIMIT = 56 * 1024 * 1024


def _cparams(*sem):
    return pltpu.CompilerParams(dimension_semantics=sem, vmem_limit_bytes=VMEM_LIMIT)


def _norm_matmul_kernel(x_ref, g_ref, w_ref, o_ref, h_ref):
    @pl.when(pl.program_id(1) == 0)
    def _():
        x = x_ref[...]
        var = jnp.mean(x * x, axis=-1, keepdims=True)
        h_ref[...] = (x * lax.rsqrt(var + NORM_EPS) * g_ref[...]).astype(BF16)

    o_ref[...] = jnp.dot(h_ref[...], w_ref[...], preferred_element_type=F32).astype(o_ref.dtype)


def _norm_matmul(x, g, w, *, tm, tn, name):
    m, d = x.shape
    n = w.shape[1]
    tm, tn = min(tm, m), min(tn, n)
    return pl.pallas_call(
        _norm_matmul_kernel,
        grid=(m // tm, n // tn),
        in_specs=[pl.BlockSpec((tm, d), lambda i, j: (i, 0)),
                  pl.BlockSpec((1, d), lambda i, j: (0, 0)),
                  pl.BlockSpec((d, tn), lambda i, j: (0, j))],
        out_specs=pl.BlockSpec((tm, tn), lambda i, j: (i, j)),
        out_shape=jax.ShapeDtypeStruct((m, n), BF16),
        scratch_shapes=[pltpu.VMEM((tm, d), BF16)],
        compiler_params=_cparams("parallel", "arbitrary"),
        name=name,
    )(x, g.reshape(1, d), w)


def _chunk_major_perm(rows):
    rb = rows // CHUNK
    src = (np.arange(rows) % rb) * CHUNK + np.arange(rows) // rb
    return jnp.asarray(np.eye(rows, dtype=np.float32)[src], BF16)


def _in_proj_kernel(x_ref, g_ref, w_ref, p_ref, o3_ref, o_ref, hp_ref, hn_ref, *, n_perm):
    j = pl.program_id(1)
    rb = x_ref.shape[0] // CHUNK

    @pl.when(j == 0)
    def _():
        x = x_ref[...]
        var = jnp.mean(x * x, axis=-1, keepdims=True)
        hn = (x * lax.rsqrt(var + NORM_EPS) * g_ref[...]).astype(BF16)
        hn_ref[...] = hn
        hp_ref[...] = jnp.dot(p_ref[...], hn, preferred_element_type=F32).astype(BF16)

    @pl.when(j < n_perm)
    def _():
        res = jnp.dot(hp_ref[...], w_ref[...], preferred_element_type=F32)
        for i in range(CHUNK):
            o3_ref[i] = res[i * rb:(i + 1) * rb].astype(o3_ref.dtype)

    @pl.when(j >= n_perm)
    def _():
        o_ref[...] = jnp.dot(hn_ref[...], w_ref[...], preferred_element_type=F32).astype(o_ref.dtype)


def _in_proj(x, g, w, *, n_perm_cols, tm, tn):
    m, d = x.shape
    n = w.shape[1]
    tm = min(tm, m)
    n_perm = n_perm_cols // tn
    rb = tm // CHUNK
    return pl.pallas_call(
        functools.partial(_in_proj_kernel, n_perm=n_perm),
        grid=(m // tm, n // tn),
        in_specs=[pl.BlockSpec((tm, d), lambda i, j: (i, 0)),
                  pl.BlockSpec((1, d), lambda i, j: (0, 0)),
                  pl.BlockSpec((d, tn), lambda i, j: (0, j)),
                  pl.BlockSpec((tm, tm), lambda i, j: (0, 0))],
        out_specs=[pl.BlockSpec((CHUNK, rb, tn), lambda i, j: (0, i, jnp.minimum(j, n_perm - 1))),
                   pl.BlockSpec((tm, tn), lambda i, j: (i, jnp.maximum(j - n_perm, 0)))],
        out_shape=[jax.ShapeDtypeStruct((CHUNK, m // CHUNK, n_perm_cols), BF16),
                   jax.ShapeDtypeStruct((m, n - n_perm_cols), BF16)],
        scratch_shapes=[pltpu.VMEM((tm, d), BF16), pltpu.VMEM((tm, d), BF16)],
        compiler_params=_cparams("parallel", "arbitrary"),
        name="in_proj",
    )(x, g.reshape(1, d), w, _chunk_major_perm(tm))


def _rope_kernel(q_ref, k_ref, cos_ref, sin_ref, qo_ref, ko_ref, *, q_scale):
    cos = cos_ref[...]
    sin = sin_ref[...]
    lane = lax.broadcasted_iota(jnp.int32, cos.shape, 1)
    first_half = (lane % DIFF_HEAD_DIM) < (DIFF_HEAD_DIM // 2)
    for x_ref, o_ref, mult in ((q_ref, qo_ref, q_scale), (k_ref, ko_ref, 1.0)):
        for h in range(DIFF_HEADS):
            x = x_ref[:, h * LANES:(h + 1) * LANES].astype(F32)
            partner = jnp.where(first_half,
                                pltpu.roll(x, LANES - DIFF_HEAD_DIM // 2, axis=1),
                                pltpu.roll(x, DIFF_HEAD_DIM // 2, axis=1))
            o_ref[:, h * LANES:(h + 1) * LANES] = ((x * cos + partner * sin) * mult).astype(o_ref.dtype)


def _rope(proj, cos_t, sin_t, *, q_col_block, tm):
    m = proj.shape[0]
    tm = min(tm, m)
    w = DIFF_HEADS * LANES
    q_scale = DIFF_HEAD_DIM ** -0.5 * math.log2(math.e)
    return pl.pallas_call(
        functools.partial(_rope_kernel, q_scale=q_scale),
        grid=(m // tm,),
        in_specs=[pl.BlockSpec((tm, w), lambda i: (i, q_col_block)),
                  pl.BlockSpec((tm, w), lambda i: (i, q_col_block + 1)),
                  pl.BlockSpec((tm, LANES), lambda i: (i, 0)),
                  pl.BlockSpec((tm, LANES), lambda i: (i, 0))],
        out_specs=[pl.BlockSpec((tm, w), lambda i: (i, 0))] * 2,
        out_shape=[jax.ShapeDtypeStruct((m, w), BF16)] * 2,
        compiler_params=_cparams("parallel"),
        name="rope",
    )(proj, proj, cos_t, sin_t)


def _diff_attn_kernel(q_ref, k_ref, vt_ref, g_ref, lam_ref, sg_ref, o_ref, acc1, acc2, s_a, s_b, *, tq):
    qi = pl.program_id(2)
    q = q_ref[...]
    lane = lax.broadcasted_iota(jnp.int32, q.shape, 1)
    q_maps = (jnp.where(lane < DIFF_HEAD_DIM, q, jnp.zeros_like(q)),
              jnp.where(lane >= DIFF_HEAD_DIM, q, jnp.zeros_like(q)))
    accs = (acc1, acc2)
    acc1[...] = jnp.zeros_like(acc1)
    acc2[...] = jnp.zeros_like(acc2)

    def scores_into(s_ref, j):
        k = k_ref[pl.ds(pl.multiple_of(j * tq, tq), tq), :]
        for mi, qm in enumerate(q_maps):
            s_ref[mi] = lax.dot_general(k, qm, (((1,), (1,)), ((), ())),
                                        preferred_element_type=F32)

    def consume(s_ref, j, ms, masked):
        vt = vt_ref[0, :, pl.ds(pl.multiple_of(j * tq, tq), tq)]
        out = []
        for mi, (m_old, acc) in enumerate(zip(ms, accs)):
            s = s_ref[mi]
            if masked:
                key = lax.broadcasted_iota(jnp.int32, s.shape, 0)
                qry = lax.broadcasted_iota(jnp.int32, s.shape, 1)
                s = jnp.where(key <= qry, s, MASK_VALUE)
            m_new = jnp.maximum(m_old, jnp.max(s, axis=0, keepdims=True))
            p = jnp.exp2(s - m_new)
            acc[...] = acc[...] * jnp.exp2(m_old - m_new) + jnp.dot(vt, p.astype(BF16),
                                                                    preferred_element_type=F32)
            out.append(m_new)
        return tuple(out)

    def finish():
        d = 2 * DIFF_HEAD_DIM
        a1, a2 = acc1[...], acc2[...]
        o_t = a1[:d] / a1[d:d + 1] - lam_ref[0] * (a2[:d] / a2[d:d + 1])
        o = o_t.T
        var = jnp.mean(o * o, axis=-1, keepdims=True)
        o = o * lax.rsqrt(var + NORM_EPS) * sg_ref[...]
        o_ref[...] = (o * jax.nn.silu(g_ref[...].astype(F32))).astype(o_ref.dtype)

    scores_into(s_a, 0)

    def two_blocks(i, ms):
        scores_into(s_b, 2 * i + 1)
        ms = consume(s_a, 2 * i, ms, False)
        scores_into(s_a, 2 * i + 2)
        return consume(s_b, 2 * i + 1, ms, False)

    m0 = jnp.full((1, tq), MASK_VALUE, F32)
    ms = lax.fori_loop(0, qi // 2, two_blocks, (m0, m0))

    @pl.when(qi % 2 == 0)
    def _():
        consume(s_a, qi, ms, True)
        finish()

    @pl.when(qi % 2 == 1)
    def _():
        scores_into(s_b, qi)
        consume(s_b, qi, consume(s_a, qi - 1, ms, False), True)
        finish()


def _diff_attn(q_rot, k_rot, vt_ext, proj, lam, sg_vec, *, bsz, seq, g_col, tq):
    tq = min(tq, seq)
    nq = seq // tq
    v_rows = vt_ext.shape[1]
    return pl.pallas_call(
        functools.partial(_diff_attn_kernel, tq=tq),
        grid=(bsz, DIFF_HEADS, nq),
        in_specs=[
            pl.BlockSpec((tq, LANES), lambda b, h, i: (b * nq + i, h)),
            pl.BlockSpec((seq, LANES), lambda b, h, i: (b, h)),
            pl.BlockSpec((1, v_rows, seq), lambda b, h, i: (b * DIFF_HEADS + h, 0, 0)),
            pl.BlockSpec((tq, LANES), lambda b, h, i: (b * nq + i, g_col + h)),
            pl.BlockSpec(memory_space=pltpu.SMEM),
            pl.BlockSpec((1, LANES), lambda b, h, i: (0, 0)),
        ],
        out_specs=pl.BlockSpec((tq, LANES), lambda b, h, i: (b * nq + i, h)),
        out_shape=jax.ShapeDtypeStruct((bsz * seq, DIFF_HEADS * LANES), BF16),
        scratch_shapes=[pltpu.VMEM((v_rows, tq), F32), pltpu.VMEM((v_rows, tq), F32),
                        pltpu.VMEM((2, tq, tq), F32), pltpu.VMEM((2, tq, tq), F32)],
        compiler_params=_cparams("parallel", "parallel", "arbitrary"),
        name="diff_attn",
    )(q_rot, k_rot, vt_ext, proj, lam, sg_vec)


def _mem_attn_kernel(q_ref, g_ref, mk_ref, mv_ref, o_ref):
    scale = XATTN_HEAD_DIM ** -0.5
    for h in range(XATTN_HEADS):
        sl = slice(h * XATTN_HEAD_DIM, (h + 1) * XATTN_HEAD_DIM)
        s = lax.dot_general(q_ref[:, sl], mk_ref[:, sl], (((1,), (1,)), ((), ())),
                            preferred_element_type=F32) * scale
        p = jnp.exp(s - jnp.max(s, axis=-1, keepdims=True))
        p = p / jnp.sum(p, axis=-1, keepdims=True)
        o = jnp.dot(p.astype(BF16), mv_ref[:, sl], preferred_element_type=F32)
        o_ref[:, sl] = (o * jax.nn.silu(g_ref[:, sl].astype(F32))).astype(o_ref.dtype)


def _mem_attn(proj, mem_kv, *, bsz, seq, mem_tokens, q_col_block, g_col_block, tq):
    tq = min(tq, seq)
    nq = seq // tq
    w = XATTN_HEADS * XATTN_HEAD_DIM
    return pl.pallas_call(
        _mem_attn_kernel,
        grid=(bsz, nq),
        in_specs=[pl.BlockSpec((tq, w), lambda b, i: (b * nq + i, q_col_block)),
                  pl.BlockSpec((tq, w), lambda b, i: (b * nq + i, g_col_block)),
                  pl.BlockSpec((mem_tokens, w), lambda b, i: (b, 0)),
                  pl.BlockSpec((mem_tokens, w), lambda b, i: (b, 1))],
        out_specs=pl.BlockSpec((tq, w), lambda b, i: (b * nq + i, 0)),
        out_shape=jax.ShapeDtypeStruct((bsz * seq, w), BF16),
        compiler_params=_cparams("parallel", "parallel"),
        name="mem_attn",
    )(proj, proj, mem_kv, mem_kv)


def _s5_taps_kernel(lhs_ref, rhs_ref, o_ref):
    o_ref[0] = jnp.dot(lhs_ref[0], rhs_ref[0], preferred_element_type=F32,
                       precision=lax.Precision.HIGHEST)


def _s5_taps(lhs, rhs):
    g, r, k = lhs.shape
    n = rhs.shape[2]
    return pl.pallas_call(
        _s5_taps_kernel,
        grid=(g,),
        in_specs=[pl.BlockSpec((1, r, k), lambda i: (i, 0, 0)),
                  pl.BlockSpec((1, k, n), lambda i: (i, 0, 0))],
        out_specs=pl.BlockSpec((1, r, n), lambda i: (i, 0, 0)),
        out_shape=jax.ShapeDtypeStruct((g, r, n), F32),
        compiler_params=_cparams("parallel"),
        name="s5_taps",
    )(lhs, rhs)


def _s5_weights(a_re, a_im, log_dt, b_re, b_im, c_re, c_im, d_skip):
    g = a_re.shape[0]
    p, h, t = SSM_STATE, SSM_GROUP, CHUNK
    dt = jnp.exp(log_dt.astype(F32))[:, None]
    lr, li = a_re.astype(F32), a_im.astype(F32)
    mag = jnp.exp(lr * dt)
    abar_re, abar_im = mag * jnp.cos(li * dt), mag * jnp.sin(li * dt)
    den = lr * lr + li * li
    nr, ni = abar_re - 1.0, abar_im
    z_re = (nr * lr + ni * li) / den
    z_im = (ni * lr - nr * li) / den
    br, bi = b_re.astype(F32), b_im.astype(F32)
    bbar_re = z_re[..., None] * br - z_im[..., None] * bi
    bbar_im = z_re[..., None] * bi + z_im[..., None] * br
    cr, ci = c_re.astype(F32), c_im.astype(F32)

    tau = jnp.arange(t + 1, dtype=F32)[None, :, None]
    pmag = jnp.exp(tau * (lr * dt)[:, None, :])
    pw_re = pmag * jnp.cos(tau * (li * dt)[:, None, :])
    pw_im = pmag * jnp.sin(tau * (li * dt)[:, None, :])

    ca_re = cr[:, None] * pw_re[:, :, None, :] - ci[:, None] * pw_im[:, :, None, :]
    ca_im = cr[:, None] * pw_im[:, :, None, :] + ci[:, None] * pw_re[:, :, None, :]

    lhs = jnp.concatenate([ca_re[:, :t], ca_im[:, :t]], axis=-1).reshape(g, t * h, 2 * p)
    rhs = jnp.concatenate([bbar_re, -bbar_im], axis=1)
    taps = _s5_taps(lhs, rhs).reshape(g, t, h, h)
    taps = taps.at[:, 0].add(jax.vmap(jnp.diag)(d_skip.astype(F32)))

    gt = GROUPS_PER_TILE
    nt = g // gt
    idx = jnp.arange(t)

    k_cat = taps.transpose(0, 3, 1, 2).reshape(nt, gt * h, t * h)

    pj_re, pj_im = pw_re[:, t - 1 - idx], pw_im[:, t - 1 - idx]
    bt_re, bt_im = bbar_re.transpose(0, 2, 1)[:, None], bbar_im.transpose(0, 2, 1)[:, None]
    win_re = pj_re[:, :, None, :] * bt_re - pj_im[:, :, None, :] * bt_im
    win_im = pj_re[:, :, None, :] * bt_im + pj_im[:, :, None, :] * bt_re
    win = jnp.stack([win_re, win_im], axis=3).reshape(nt, gt, t, h, 2, p)
    win_cat = win.transpose(0, 2, 1, 3, 4, 5).reshape(nt, t * gt * h, 2 * p)

    wo = jnp.stack([ca_re[:, 1:], -ca_im[:, 1:]], axis=1).reshape(nt, gt, 2, t, h, p)
    wo_cat = wo.transpose(0, 2, 1, 5, 3, 4).reshape(nt, 2 * gt * p, t * h)

    kk = (t * jnp.arange(2 * SUBLANES, dtype=F32))[:, None, None]
    cmag = jnp.exp(kk * (lr * dt)[None])
    a_chunk_re = (cmag * jnp.cos(kk * (li * dt)[None])).reshape(2 * SUBLANES, g * p)
    a_chunk_im = (cmag * jnp.sin(kk * (li * dt)[None])).reshape(2 * SUBLANES, g * p)
    return k_cat.astype(BF16), win_cat.astype(BF16), wo_cat.astype(BF16), a_chunk_re, a_chunk_im


def _lane_repeat(blocks, width, copies):
    r = np.kron(np.eye(blocks, dtype=np.float32),
                np.kron(np.ones((1, copies), np.float32), np.eye(width, dtype=np.float32)))
    return jnp.asarray(r, BF16)


def _group_of(index, period, size):
    assert period & (period - 1) == 0 and size & (size - 1) == 0
    return lax.shift_right_logical(index & (period - 1), size.bit_length() - 1)


def _expand_block_diag(compact, rep, row_group, col_group):
    e = jnp.dot(compact, rep, preferred_element_type=F32)
    r = lax.broadcasted_iota(jnp.int32, e.shape, 0)
    c = lax.broadcasted_iota(jnp.int32, e.shape, 1)
    return jnp.where(row_group(r) == col_group(c), e, 0.0).astype(BF16)


def _chunk_lhs(u_ref):
    return jnp.concatenate([u_ref[j] for j in range(CHUNK)], axis=1)


def _s5_state_in_kernel(u_ref, wc_ref, rep_ref, sre_ref, sim_ref, w_scr):
    @pl.when(pl.program_id(1) == 0)
    def _():
        w_scr[...] = _expand_block_diag(
            wc_ref[0], rep_ref[...],
            lambda r: _group_of(r, LANES, SSM_GROUP),
            lambda c: _group_of(c, GROUPS_PER_TILE * SSM_STATE, SSM_STATE))

    s = jnp.dot(_chunk_lhs(u_ref), w_scr[...], preferred_element_type=F32)
    half = s.shape[1] // 2
    sre_ref[...] = s[:, :half]
    sim_ref[...] = s[:, half:]


def _s5_state_in(us3, win_cat, *, rb):
    t, r, _ = us3.shape
    nt, k, nc = win_cat.shape
    n = nc * GROUPS_PER_TILE
    rb = min(rb, r)
    rep = _lane_repeat(2, SSM_STATE, GROUPS_PER_TILE)
    return pl.pallas_call(
        _s5_state_in_kernel,
        grid=(nt, r // rb),
        in_specs=[pl.BlockSpec((t, rb, LANES), lambda q, i: (0, i, q)),
                  pl.BlockSpec((1, k, nc), lambda q, i: (q, 0, 0)),
                  pl.BlockSpec(rep.shape, lambda q, i: (0, 0))],
        out_specs=[pl.BlockSpec((rb, n // 2), lambda q, i: (i, q)),
                   pl.BlockSpec((rb, n // 2), lambda q, i: (i, q))],
        out_shape=[jax.ShapeDtypeStruct((r, nt * n // 2), F32)] * 2,
        scratch_shapes=[pltpu.VMEM((k, n), BF16)],
        compiler_params=_cparams("parallel", "arbitrary"),
        name="s5_state_in",
    )(us3, win_cat, rep)


def _s5_recurrence_kernel(sre_ref, sim_ref, pre_ref, pim_ref, hre_ref, him_ref, *, bsz, n_chunks):
    tn = sre_ref.shape[1]
    shape = (SUBLANES, tn)
    row = lax.broadcasted_iota(jnp.int32, shape, 0)
    pw_re = pre_ref[0:SUBLANES, :]
    pw_im = pim_ref[0:SUBLANES, :]
    a_tile_re = jnp.broadcast_to(pre_ref[SUBLANES:SUBLANES + 1, :], shape)
    a_tile_im = jnp.broadcast_to(pim_ref[SUBLANES:SUBLANES + 1, :], shape)

    def tile_step(it, carry):
        out = []
        for b in range(bsz):
            h_re, h_im = carry[2 * b], carry[2 * b + 1]
            rows = pl.ds(pl.multiple_of(b * n_chunks + it * SUBLANES, SUBLANES), SUBLANES)
            y_re, y_im = sre_ref[rows, :], sim_ref[rows, :]
            for d in (1, 2, 4):
                m_re = jnp.broadcast_to(pre_ref[d:d + 1, :], shape)
                m_im = jnp.broadcast_to(pim_ref[d:d + 1, :], shape)
                s_re = jnp.where(row >= d, pltpu.roll(y_re, d, axis=0), 0.0)
                s_im = jnp.where(row >= d, pltpu.roll(y_im, d, axis=0), 0.0)
                y_re, y_im = (y_re + m_re * s_re - m_im * s_im,
                              y_im + m_re * s_im + m_im * s_re)
            e_re = jnp.where(row >= 1, pltpu.roll(y_re, 1, axis=0), 0.0)
            e_im = jnp.where(row >= 1, pltpu.roll(y_im, 1, axis=0), 0.0)
            hre_ref[rows, :] = pw_re * h_re - pw_im * h_im + e_re
            him_ref[rows, :] = pw_re * h_im + pw_im * h_re + e_im
            last_re = jnp.broadcast_to(y_re[SUBLANES - 1:SUBLANES, :], shape)
            last_im = jnp.broadcast_to(y_im[SUBLANES - 1:SUBLANES, :], shape)
            out.append(a_tile_re * h_re - a_tile_im * h_im + last_re)
            out.append(a_tile_re * h_im + a_tile_im * h_re + last_im)
        return tuple(out)

    zero = jnp.zeros(shape, F32)
    lax.fori_loop(0, n_chunks // SUBLANES, tile_step, (zero,) * (2 * bsz))


def _s5_recurrence(s_re, s_im, pw_re, pw_im, *, bsz, tn):
    r, n = s_re.shape
    tn = min(tn, n)
    spec = pl.BlockSpec((r, tn), lambda j: (0, j))
    pspec = pl.BlockSpec((pw_re.shape[0], tn), lambda j: (0, j))
    return pl.pallas_call(
        functools.partial(_s5_recurrence_kernel, bsz=bsz, n_chunks=r // bsz),
        grid=(n // tn,),
        in_specs=[spec, spec, pspec, pspec],
        out_specs=[spec, spec],
        out_shape=[jax.ShapeDtypeStruct((r, n), F32)] * 2,
        compiler_params=_cparams("parallel"),
        name="s5_recurrence",
    )(s_re, s_im, pw_re, pw_im)


def _s5_out_kernel(u_ref, kc_ref, hre_ref, him_ref, wc_ref, rep_ref, y_ref, t_scr, w_scr):
    @pl.when(pl.program_id(1) == 0)
    def _():
        rep = rep_ref[...]
        col_group = lambda c: _group_of(c, LANES, SSM_GROUP)
        taps = _expand_block_diag(kc_ref[0], rep, lambda r: _group_of(r, LANES, SSM_GROUP), col_group)
        t_scr[...] = jnp.zeros_like(t_scr)
        for j in range(CHUNK):
            t_scr[j * LANES:(j + 1) * LANES, j * LANES:] = taps[:, :(CHUNK - j) * LANES]
        w_scr[...] = _expand_block_diag(
            wc_ref[0], rep, lambda r: _group_of(r, GROUPS_PER_TILE * SSM_STATE, SSM_STATE), col_group)

    h = jnp.concatenate([hre_ref[...], him_ref[...]], axis=1).astype(BF16)
    y = jnp.dot(_chunk_lhs(u_ref), t_scr[...], preferred_element_type=F32)
    y += jnp.dot(h, w_scr[...], preferred_element_type=F32)
    for i in range(CHUNK):
        y_ref[i] = y[:, i * LANES:(i + 1) * LANES].astype(y_ref.dtype)


def _s5_out(us3, k_cat, h_re, h_im, wo_cat, *, rb):
    t, r, _ = us3.shape
    nt, ks, kc = wo_cat.shape
    k = t * LANES
    rb = min(rb, r)
    rep = _lane_repeat(t, SSM_GROUP, GROUPS_PER_TILE)
    return pl.pallas_call(
        _s5_out_kernel,
        grid=(nt, r // rb),
        in_specs=[pl.BlockSpec((t, rb, LANES), lambda q, i: (0, i, q)),
                  pl.BlockSpec((1, LANES, kc), lambda q, i: (q, 0, 0)),
                  pl.BlockSpec((rb, ks // 2), lambda q, i: (i, q)),
                  pl.BlockSpec((rb, ks // 2), lambda q, i: (i, q)),
                  pl.BlockSpec((1, ks, kc), lambda q, i: (q, 0, 0)),
                  pl.BlockSpec(rep.shape, lambda q, i: (0, 0))],
        out_specs=pl.BlockSpec((t, rb, LANES), lambda q, i: (0, i, q)),
        out_shape=jax.ShapeDtypeStruct((t, r, nt * LANES), BF16),
        scratch_shapes=[pltpu.VMEM((k, k), BF16), pltpu.VMEM((ks, k), BF16)],
        compiler_params=_cparams("parallel", "arbitrary"),
        name="s5_out",
    )(us3, k_cat, h_re, h_im, wo_cat, rep)


def _s5_glu_kernel(y_ref, gate_ref, w_ref, b_ref, pt_ref, o_ref):
    t, rb, w = y_ref.shape
    y = jax.nn.gelu(y_ref[...].reshape(t * rb, w).astype(F32))
    z = jnp.dot(y.astype(BF16), w_ref[...], preferred_element_type=F32) + b_ref[...]
    y = y * jax.nn.sigmoid(z)
    y = (y * jax.nn.silu(gate_ref[...].reshape(t * rb, w).astype(F32))).astype(BF16)
    o_ref[...] = jnp.dot(pt_ref[...], y, preferred_element_type=F32).astype(o_ref.dtype)


def _s5_glu(y3, us3, w_glu, b_glu, *, rb):
    t, r, w = y3.shape
    rb = min(rb, r)
    rows = t * rb
    return pl.pallas_call(
        _s5_glu_kernel,
        grid=(r // rb,),
        in_specs=[pl.BlockSpec((t, rb, w), lambda i: (0, i, 0)),
                  pl.BlockSpec((t, rb, w), lambda i: (0, i, 1)),
                  pl.BlockSpec((w, w), lambda i: (0, 0)),
                  pl.BlockSpec((1, w), lambda i: (0, 0)),
                  pl.BlockSpec((rows, rows), lambda i: (0, 0))],
        out_specs=pl.BlockSpec((rows, w), lambda i: (i, 0)),
        out_shape=jax.ShapeDtypeStruct((t * r, w), BF16),
        compiler_params=_cparams("parallel"),
        name="s5_glu",
    )(y3, us3, w_glu, b_glu.reshape(1, w), _chunk_major_perm(rows).T)


def _out_proj_kernel(ys_ref, yd_ref, yx_ref, w_ref, x_ref, g_ref, o_ref):
    ws, wd = ys_ref.shape[1], yd_ref.shape[1]
    mix = jnp.dot(ys_ref[...], w_ref[0:ws, :], preferred_element_type=F32)
    mix += jnp.dot(yd_ref[...], w_ref[ws:ws + wd, :], preferred_element_type=F32)
    mix += jnp.dot(yx_ref[...], w_ref[ws + wd:, :], preferred_element_type=F32)
    var = jnp.mean(mix * mix, axis=-1, keepdims=True)
    o_ref[...] = x_ref[...] + mix * lax.rsqrt(var + NORM_EPS) * g_ref[...]


def _out_proj(y_s, y_d, y_x, w_out, x, g_post, *, tm):
    m, d = x.shape
    tm = min(tm, m)
    row = lambda i: (i, 0)
    const = lambda i: (0, 0)
    return pl.pallas_call(
        _out_proj_kernel,
        grid=(m // tm,),
        in_specs=[pl.BlockSpec((tm, y_s.shape[1]), row),
                  pl.BlockSpec((tm, y_d.shape[1]), row),
                  pl.BlockSpec((tm, y_x.shape[1]), row),
                  pl.BlockSpec(w_out.shape, const),
                  pl.BlockSpec((tm, d), row),
                  pl.BlockSpec((1, d), const)],
        out_specs=pl.BlockSpec((tm, d), row),
        out_shape=jax.ShapeDtypeStruct((m, d), F32),
        compiler_params=_cparams("parallel"),
        name="out_proj",
    )(y_s, y_d, y_x, w_out, x, g_post.reshape(1, d))


def kernel(x, mem, positions, norm_pre, norm_post, norm_mem, w_in, w_out, w_mem_kv, ssm_a_re, ssm_a_im, ssm_log_dt, ssm_b_re, ssm_b_im, ssm_c_re, ssm_c_im, ssm_d, w_glu, b_glu, diff_lq1, diff_lk1, diff_lq2, diff_lk2, diff_subln):
    bsz, seq, d_model = x.shape
    mem_tokens = mem.shape[1]
    depth = w_in.shape[0]
    m = bsz * seq
    ssm_width = ssm_a_re.shape[1] * SSM_GROUP
    n_groups = ssm_width // SSM_GROUP
    diff_width = DIFF_HEADS * 2 * DIFF_HEAD_DIM
    xattn_width = XATTN_HEADS * XATTN_HEAD_DIM
    n_chunks = seq // CHUNK
    n_s5_cols = 2 * ssm_width
    col_qd = 0
    col_kd = col_qd + diff_width
    col_vd = col_kd + diff_width
    col_gd = col_vd + diff_width
    col_qx = col_gd + diff_width
    col_gx = col_qx + xattn_width
    assert col_kd == col_qd + diff_width and col_qd % diff_width == 0

    inv = ROPE_THETA ** (-jnp.arange(0, DIFF_HEAD_DIM, 2, dtype=F32) / DIFF_HEAD_DIM)
    ang = positions.astype(F32).reshape(m, 1) * inv
    cos, sin = jnp.cos(ang), jnp.sin(ang)
    cos_t = jnp.concatenate([cos, cos, cos, cos], axis=-1)
    sin_t = jnp.concatenate([-sin, sin, -sin, sin], axis=-1)

    xf = x.reshape(m, d_model)
    mem_f = mem.reshape(bsz * mem_tokens, d_model)
    for l in range(depth):
        lambda_init = 0.8 - 0.6 * math.exp(-0.3 * l)
        us3, proj = _in_proj(xf, norm_pre[l], w_in[l].astype(BF16), n_perm_cols=n_s5_cols, tm=512, tn=1024)

        k_cat, win_cat, wo_cat, a_re, a_im = _s5_weights(
            ssm_a_re[l], ssm_a_im[l], ssm_log_dt[l], ssm_b_re[l], ssm_b_im[l],
            ssm_c_re[l], ssm_c_im[l], ssm_d[l])
        s_re, s_im = _s5_state_in(us3, win_cat, rb=512)
        h_re, h_im = _s5_recurrence(s_re, s_im, a_re, a_im, bsz=bsz, tn=1024)
        y3 = _s5_out(us3, k_cat, h_re, h_im, wo_cat, rb=512)
        y_s = _s5_glu(y3, us3, w_glu[l].astype(BF16), b_glu[l], rb=64)

        q_rot, k_rot = _rope(proj, cos_t, sin_t, q_col_block=col_qd // diff_width, tm=1024)
        lam = (jnp.exp(jnp.sum(diff_lq1[l].astype(F32) * diff_lk1[l].astype(F32)))
               - jnp.exp(jnp.sum(diff_lq2[l].astype(F32) * diff_lk2[l].astype(F32))) + lambda_init)
        sg_vec = (diff_subln[l].astype(F32) * (1.0 - lambda_init)).reshape(1, LANES)
        v_t = proj[:, col_vd:col_vd + diff_width].reshape(bsz, seq, DIFF_HEADS, 2 * DIFF_HEAD_DIM)
        v_t = v_t.transpose(0, 2, 3, 1)
        vt_ext = jnp.concatenate([v_t, jnp.ones((bsz, DIFF_HEADS, BF16_SUBLANES, seq), BF16)], axis=2)
        vt_ext = vt_ext.reshape(bsz * DIFF_HEADS, 2 * DIFF_HEAD_DIM + BF16_SUBLANES, seq)
        y_d = _diff_attn(q_rot, k_rot, vt_ext, proj, lam.reshape(1), sg_vec, bsz=bsz, seq=seq,
                         g_col=col_gd // LANES, tq=512)

        mem_kv = _norm_matmul(mem_f, norm_mem[l], w_mem_kv[l].astype(BF16), tm=512, tn=1024, name="mem_kv")
        y_x = _mem_attn(proj, mem_kv, bsz=bsz, seq=seq, mem_tokens=mem_tokens,
                        q_col_block=col_qx // xattn_width, g_col_block=col_gx // xattn_width, tq=1024)

        xf = _out_proj(y_s, y_d, y_x, w_out[l].astype(BF16), xf, norm_post[l], tm=512)
    return xf.reshape(bsz, seq, d_model)
```

```python
import functools
import math

import jax
import jax.numpy as jnp
import numpy as np
from jax import lax
from jax.experimental import pallas as pl
from jax.experimental.pallas import tpu as pltpu

F32 = jnp.float32
BF16 = jnp.bfloat16

SSM_GROUP = 16
SSM_STATE = 64
CHUNK = 16
DIFF_HEADS = 4
DIFF_HEAD_DIM = 64
XATTN_HEADS = 4
XATTN_HEAD_DIM = 128
ROPE_THETA = 10000.0
NORM_EPS = 1e-6
MASK_VALUE = -1e30
LANES = 128
SUBLANES = 8
BF16_SUBLANES = 16
GROUPS_PER_TILE = LANES // SSM_GROUP
VMEM_LIMIT = 56 * 1024 * 1024


def _cparams(*sem):
    return pltpu.CompilerParams(dimension_semantics=sem, vmem_limit_bytes=VMEM_LIMIT)


def _norm_matmul_kernel(x_ref, g_ref, w_ref, o_ref, h_ref):
    @pl.when(pl.program_id(1) == 0)
    def _():
        x = x_ref[...]
        var = jnp.mean(x * x, axis=-1, keepdims=True)
        h_ref[...] = (x * lax.rsqrt(var + NORM_EPS) * g_ref[...]).astype(BF16)

    o_ref[...] = jnp.dot(h_ref[...], w_ref[...], preferred_element_type=F32).astype(o_ref.dtype)


def _norm_matmul(x, g, w, *, tm, tn, name):
    m, d = x.shape
    n = w.shape[1]
    tm, tn = min(tm, m), min(tn, n)
    return pl.pallas_call(
        _norm_matmul_kernel,
        grid=(m // tm, n // tn),
        in_specs=[pl.BlockSpec((tm, d), lambda i, j: (i, 0)),
                  pl.BlockSpec((1, d), lambda i, j: (0, 0)),
                  pl.BlockSpec((d, tn), lambda i, j: (0, j))],
        out_specs=pl.BlockSpec((tm, tn), lambda i, j: (i, j)),
        out_shape=jax.ShapeDtypeStruct((m, n), BF16),
        scratch_shapes=[pltpu.VMEM((tm, d), BF16)],
        compiler_params=_cparams("parallel", "arbitrary"),
        name=name,
    )(x, g.reshape(1, d), w)


def _chunk_major_perm(rows):
    rb = rows // CHUNK
    src = (np.arange(rows) % rb) * CHUNK + np.arange(rows) // rb
    return jnp.asarray(np.eye(rows, dtype=np.float32)[src], BF16)


def _in_proj_kernel(x_ref, g_ref, w_ref, p_ref, o3_ref, o_ref, hp_ref, hn_ref, *, n_perm):
    j = pl.program_id(1)
    tm = x_ref.shape[0]
    pr = p_ref.shape[0]
    rb = pr // CHUNK

    @pl.when(j == 0)
    def _():
        for s in range(tm // pr):
            x = x_ref[s * pr:(s + 1) * pr, :]
            var = jnp.mean(x * x, axis=-1, keepdims=True)
            hn = (x * lax.rsqrt(var + NORM_EPS) * g_ref[...]).astype(BF16)
            hn_ref[s * pr:(s + 1) * pr, :] = hn
            hp_ref[s * pr:(s + 1) * pr, :] = jnp.dot(p_ref[...], hn, preferred_element_type=F32).astype(BF16)

    @pl.when(j < n_perm)
    def _():
        res = jnp.dot(hp_ref[...], w_ref[...], preferred_element_type=F32)
        for s in range(tm // pr):
            for i in range(CHUNK):
                rows = slice(s * pr + i * rb, s * pr + (i + 1) * rb)
                o3_ref[i, s * rb:(s + 1) * rb, :] = res[rows].astype(o3_ref.dtype)

    @pl.when(j >= n_perm)
    def _():
        o_ref[...] = jnp.dot(hn_ref[...], w_ref[...], preferred_element_type=F32).astype(o_ref.dtype)


def _in_proj(x, g, w, *, n_perm_cols, tm, tn, perm_rows):
    m, d = x.shape
    n = w.shape[1]
    tm = min(tm, m)
    perm_rows = min(perm_rows, tm)
    n_perm = n_perm_cols // tn
    return pl.pallas_call(
        functools.partial(_in_proj_kernel, n_perm=n_perm),
        grid=(m // tm, n // tn),
        in_specs=[pl.BlockSpec((tm, d), lambda i, j: (i, 0)),
                  pl.BlockSpec((1, d), lambda i, j: (0, 0)),
                  pl.BlockSpec((d, tn), lambda i, j: (0, j)),
                  pl.BlockSpec((perm_rows, perm_rows), lambda i, j: (0, 0))],
        out_specs=[pl.BlockSpec((CHUNK, tm // CHUNK, tn), lambda i, j: (0, i, jnp.minimum(j, n_perm - 1))),
                   pl.BlockSpec((tm, tn), lambda i, j: (i, jnp.maximum(j - n_perm, 0)))],
        out_shape=[jax.ShapeDtypeStruct((CHUNK, m // CHUNK, n_perm_cols), BF16),
                   jax.ShapeDtypeStruct((m, n - n_perm_cols), BF16)],
        scratch_shapes=[pltpu.VMEM((tm, d), BF16), pltpu.VMEM((tm, d), BF16)],
        compiler_params=_cparams("parallel", "arbitrary"),
        name="in_proj",
    )(x, g.reshape(1, d), w, _chunk_major_perm(perm_rows))


def _rope_kernel(q_ref, k_ref, cos_ref, sin_ref, qo_ref, ko_ref, *, q_scale):
    cos = cos_ref[...]
    sin = sin_ref[...]
    lane = lax.broadcasted_iota(jnp.int32, cos.shape, 1)
    first_half = (lane % DIFF_HEAD_DIM) < (DIFF_HEAD_DIM // 2)
    for x_ref, o_ref, mult in ((q_ref, qo_ref, q_scale), (k_ref, ko_ref, 1.0)):
        for h in range(DIFF_HEADS):
            x = x_ref[:, h * LANES:(h + 1) * LANES].astype(F32)
            partner = jnp.where(first_half,
                                pltpu.roll(x, LANES - DIFF_HEAD_DIM // 2, axis=1),
                                pltpu.roll(x, DIFF_HEAD_DIM // 2, axis=1))
            o_ref[:, h * LANES:(h + 1) * LANES] = ((x * cos + partner * sin) * mult).astype(o_ref.dtype)


def _rope(proj, cos_t, sin_t, *, q_col_block, tm):
    m = proj.shape[0]
    tm = min(tm, m)
    w = DIFF_HEADS * LANES
    q_scale = DIFF_HEAD_DIM ** -0.5 * math.log2(math.e)
    return pl.pallas_call(
        functools.partial(_rope_kernel, q_scale=q_scale),
        grid=(m // tm,),
        in_specs=[pl.BlockSpec((tm, w), lambda i: (i, q_col_block)),
                  pl.BlockSpec((tm, w), lambda i: (i, q_col_block + 1)),
                  pl.BlockSpec((tm, LANES), lambda i: (i, 0)),
                  pl.BlockSpec((tm, LANES), lambda i: (i, 0))],
        out_specs=[pl.BlockSpec((tm, w), lambda i: (i, 0))] * 2,
        out_shape=[jax.ShapeDtypeStruct((m, w), BF16)] * 2,
        compiler_params=_cparams("parallel"),
        name="rope",
    )(proj, proj, cos_t, sin_t)


def _diff_attn_kernel(q_ref, k_ref, vt_ref, g_ref, lam_ref, sg_ref, o_ref, acc1, acc2, s_a, s_b, *, tq):
    qi = pl.program_id(2)
    q = q_ref[...]
    lane = lax.broadcasted_iota(jnp.int32, q.shape, 1)
    q_maps = (jnp.where(lane < DIFF_HEAD_DIM, q, jnp.zeros_like(q)),
              jnp.where(lane >= DIFF_HEAD_DIM, q, jnp.zeros_like(q)))
    accs = (acc1, acc2)
    acc1[...] = jnp.zeros_like(acc1)
    acc2[...] = jnp.zeros_like(acc2)

    def scores_into(s_ref, j):
        k = k_ref[pl.ds(pl.multiple_of(j * tq, tq), tq), :]
        for mi, qm in enumerate(q_maps):
            s_ref[mi] = lax.dot_general(k, qm, (((1,), (1,)), ((), ())),
                                        preferred_element_type=F32)

    def consume(s_ref, j, ms, masked):
        vt = vt_ref[0, :, pl.ds(pl.multiple_of(j * tq, tq), tq)]
        out = []
        for mi, (m_old, acc) in enumerate(zip(ms, accs)):
            s = s_ref[mi]
            if masked:
                key = lax.broadcasted_iota(jnp.int32, s.shape, 0)
                qry = lax.broadcasted_iota(jnp.int32, s.shape, 1)
                s = jnp.where(key <= qry, s, MASK_VALUE)
            m_new = jnp.maximum(m_old, jnp.max(s, axis=0, keepdims=True))
            p = jnp.exp2(s - m_new)
            acc[...] = acc[...] * jnp.exp2(m_old - m_new) + jnp.dot(vt, p.astype(BF16),
                                                                    preferred_element_type=F32)
            out.append(m_new)
        return tuple(out)

    def finish():
        d = 2 * DIFF_HEAD_DIM
        a1, a2 = acc1[...], acc2[...]
        o_t = a1[:d] / a1[d:d + 1] - lam_ref[0] * (a2[:d] / a2[d:d + 1])
        o = o_t.T
        var = jnp.mean(o * o, axis=-1, keepdims=True)
        o = o * lax.rsqrt(var + NORM_EPS) * sg_ref[...]
        o_ref[...] = (o * jax.nn.silu(g_ref[...].astype(F32))).astype(o_ref.dtype)

    scores_into(s_a, 0)

    def two_blocks(i, ms):
        scores_into(s_b, 2 * i + 1)
        ms = consume(s_a, 2 * i, ms, False)
        scores_into(s_a, 2 * i + 2)
        return consume(s_b, 2 * i + 1, ms, False)

    m0 = jnp.full((1, tq), MASK_VALUE, F32)
    ms = lax.fori_loop(0, qi // 2, two_blocks, (m0, m0))

    @pl.when(qi % 2 == 0)
    def _():
        consume(s_a, qi, ms, True)
        finish()

    @pl.when(qi % 2 == 1)
    def _():
        scores_into(s_b, qi)
        consume(s_b, qi, consume(s_a, qi - 1, ms, False), True)
        finish()


def _diff_attn(q_rot, k_rot, vt_ext, proj, lam, sg_vec, *, bsz, seq, g_col, tq):
    tq = min(tq, seq)
    nq = seq // tq
    v_rows = vt_ext.shape[1]
    return pl.pallas_call(
        functools.partial(_diff_attn_kernel, tq=tq),
        grid=(bsz, DIFF_HEADS, nq),
        in_specs=[
            pl.BlockSpec((tq, LANES), lambda b, h, i: (b * nq + i, h)),
            pl.BlockSpec((seq, LANES), lambda b, h, i: (b, h)),
            pl.BlockSpec((1, v_rows, seq), lambda b, h, i: (b * DIFF_HEADS + h, 0, 0)),
            pl.BlockSpec((tq, LANES), lambda b, h, i: (b * nq + i, g_col + h)),
            pl.BlockSpec(memory_space=pltpu.SMEM),
            pl.BlockSpec((1, LANES), lambda b, h, i: (0, 0)),
        ],
        out_specs=pl.BlockSpec((tq, LANES), lambda b, h, i: (b * nq + i, h)),
        out_shape=jax.ShapeDtypeStruct((bsz * seq, DIFF_HEADS * LANES), BF16),
        scratch_shapes=[pltpu.VMEM((v_rows, tq), F32), pltpu.VMEM((v_rows, tq), F32),
                        pltpu.VMEM((2, tq, tq), F32), pltpu.VMEM((2, tq, tq), F32)],
        compiler_params=_cparams("parallel", "parallel", "arbitrary"),
        name="diff_attn",
    )(q_rot, k_rot, vt_ext, proj, lam, sg_vec)


def _mem_attn_kernel(q_ref, g_ref, mk_ref, mv_ref, o_ref):
    scale = XATTN_HEAD_DIM ** -0.5
    for h in range(XATTN_HEADS):
        sl = slice(h * XATTN_HEAD_DIM, (h + 1) * XATTN_HEAD_DIM)
        s = lax.dot_general(q_ref[:, sl], mk_ref[:, sl], (((1,), (1,)), ((), ())),
                            preferred_element_type=F32) * scale
        p = jnp.exp(s - jnp.max(s, axis=-1, keepdims=True))
        p = p / jnp.sum(p, axis=-1, keepdims=True)
        o = jnp.dot(p.astype(BF16), mv_ref[:, sl], preferred_element_type=F32)
        o_ref[:, sl] = (o * jax.nn.silu(g_ref[:, sl].astype(F32))).astype(o_ref.dtype)


def _mem_attn(proj, mem_kv, *, bsz, seq, mem_tokens, q_col_block, g_col_block, tq):
    tq = min(tq, seq)
    nq = seq // tq
    w = XATTN_HEADS * XATTN_HEAD_DIM
    return pl.pallas_call(
        _mem_attn_kernel,
        grid=(bsz, nq),
        in_specs=[pl.BlockSpec((tq, w), lambda b, i: (b * nq + i, q_col_block)),
                  pl.BlockSpec((tq, w), lambda b, i: (b * nq + i, g_col_block)),
                  pl.BlockSpec((mem_tokens, w), lambda b, i: (b, 0)),
                  pl.BlockSpec((mem_tokens, w), lambda b, i: (b, 1))],
        out_specs=pl.BlockSpec((tq, w), lambda b, i: (b * nq + i, 0)),
        out_shape=jax.ShapeDtypeStruct((bsz * seq, w), BF16),
        compiler_params=_cparams("parallel", "parallel"),
        name="mem_attn",
    )(proj, proj, mem_kv, mem_kv)


def _s5_taps_kernel(lhs_ref, rhs_ref, o_ref):
    for s in range(lhs_ref.shape[0]):
        o_ref[s] = jnp.dot(lhs_ref[s], rhs_ref[s], preferred_element_type=F32,
                           precision=lax.Precision.HIGHEST)


def _s5_taps(lhs, rhs):
    g, r, k = lhs.shape
    n = rhs.shape[2]
    gb = GROUPS_PER_TILE
    return pl.pallas_call(
        _s5_taps_kernel,
        grid=(g // gb,),
        in_specs=[pl.BlockSpec((gb, r, k), lambda i: (i, 0, 0)),
                  pl.BlockSpec((gb, k, n), lambda i: (i, 0, 0))],
        out_specs=pl.BlockSpec((gb, r, n), lambda i: (i, 0, 0)),
        out_shape=jax.ShapeDtypeStruct((g, r, n), F32),
        compiler_params=_cparams("parallel"),
        name="s5_taps",
    )(lhs, rhs)


def _s5_weights(a_re, a_im, log_dt, b_re, b_im, c_re, c_im, d_skip):
    g = a_re.shape[0]
    p, h, t = SSM_STATE, SSM_GROUP, CHUNK
    dt = jnp.exp(log_dt.astype(F32))[:, None]
    lr, li = a_re.astype(F32), a_im.astype(F32)
    mag = jnp.exp(lr * dt)
    abar_re, abar_im = mag * jnp.cos(li * dt), mag * jnp.sin(li * dt)
    den = lr * lr + li * li
    nr, ni = abar_re - 1.0, abar_im
    z_re = (nr * lr + ni * li) / den
    z_im = (ni * lr - nr * li) / den
    br, bi = b_re.astype(F32), b_im.astype(F32)
    bbar_re = z_re[..., None] * br - z_im[..., None] * bi
    bbar_im = z_re[..., None] * bi + z_im[..., None] * br
    cr, ci = c_re.astype(F32), c_im.astype(F32)

    tau = jnp.arange(t + 1, dtype=F32)[None, :, None]
    pmag = jnp.exp(tau * (lr * dt)[:, None, :])
    pw_re = pmag * jnp.cos(tau * (li * dt)[:, None, :])
    pw_im = pmag * jnp.sin(tau * (li * dt)[:, None, :])

    ca_re = cr[:, None] * pw_re[:, :, None, :] - ci[:, None] * pw_im[:, :, None, :]
    ca_im = cr[:, None] * pw_im[:, :, None, :] + ci[:, None] * pw_re[:, :, None, :]

    lhs = jnp.concatenate([ca_re[:, :t], ca_im[:, :t]], axis=-1).reshape(g, t * h, 2 * p)
    rhs = jnp.concatenate([bbar_re, -bbar_im], axis=1)
    taps = _s5_taps(lhs, rhs).reshape(g, t, h, h)
    taps = taps.at[:, 0].add(jax.vmap(jnp.diag)(d_skip.astype(F32)))

    gt = GROUPS_PER_TILE
    nt = g // gt
    idx = jnp.arange(t)

    k_cat = taps.transpose(0, 3, 1, 2).reshape(nt, gt * h, t * h)

    pj_re, pj_im = pw_re[:, t - 1 - idx], pw_im[:, t - 1 - idx]
    bt_re, bt_im = bbar_re.transpose(0, 2, 1)[:, None], bbar_im.transpose(0, 2, 1)[:, None]
    win_re = pj_re[:, :, None, :] * bt_re - pj_im[:, :, None, :] * bt_im
    win_im = pj_re[:, :, None, :] * bt_im + pj_im[:, :, None, :] * bt_re
    win = jnp.stack([win_re, win_im], axis=3).reshape(nt, gt, t, h, 2, p)
    win_cat = win.transpose(0, 2, 1, 3, 4, 5).reshape(nt, t * gt * h, 2 * p)

    wo = jnp.stack([ca_re[:, 1:], -ca_im[:, 1:]], axis=1).reshape(nt, gt, 2, t, h, p)
    wo_cat = wo.transpose(0, 2, 1, 5, 3, 4).reshape(nt, 2 * gt * p, t * h)

    kk = (t * jnp.arange(2 * SUBLANES, dtype=F32))[:, None, None]
    cmag = jnp.exp(kk * (lr * dt)[None])
    a_chunk_re = (cmag * jnp.cos(kk * (li * dt)[None])).reshape(2 * SUBLANES, g * p)
    a_chunk_im = (cmag * jnp.sin(kk * (li * dt)[None])).reshape(2 * SUBLANES, g * p)
    return k_cat.astype(BF16), win_cat.astype(BF16), wo_cat.astype(BF16), a_chunk_re, a_chunk_im


def _lane_repeat(blocks, width, copies):
    r = np.kron(np.eye(blocks, dtype=np.float32),
                np.kron(np.ones((1, copies), np.float32), np.eye(width, dtype=np.float32)))
    return jnp.asarray(r, BF16)


def _group_of(index, period, size):
    assert period & (period - 1) == 0 and size & (size - 1) == 0
    return lax.shift_right_logical(index & (period - 1), size.bit_length() - 1)


def _expand_block_diag(compact, rep, row_group, col_group):
    e = jnp.dot(compact, rep, preferred_element_type=F32)
    r = lax.broadcasted_iota(jnp.int32, e.shape, 0)
    c = lax.broadcasted_iota(jnp.int32, e.shape, 1)
    return jnp.where(row_group(r) == col_group(c), e, 0.0).astype(BF16)


def _chunk_lhs(u_ref):
    return jnp.concatenate([u_ref[j] for j in range(CHUNK)], axis=1)


def _s5_state_in_kernel(u_ref, wc_ref, rep_ref, sre_ref, sim_ref, w_scr):
    @pl.when(pl.program_id(1) == 0)
    def _():
        w_scr[...] = _expand_block_diag(
            wc_ref[0], rep_ref[...],
            lambda r: _group_of(r, LANES, SSM_GROUP),
            lambda c: _group_of(c, GROUPS_PER_TILE * SSM_STATE, SSM_STATE))

    s = jnp.dot(_chunk_lhs(u_ref), w_scr[...], preferred_element_type=F32)
    half = s.shape[1] // 2
    sre_ref[...] = s[:, :half]
    sim_ref[...] = s[:, half:]


def _s5_state_in(us3, win_cat, *, rb):
    t, r, _ = us3.shape
    nt, k, nc = win_cat.shape
    n = nc * GROUPS_PER_TILE
    rb = min(rb, r)
    rep = _lane_repeat(2, SSM_STATE, GROUPS_PER_TILE)
    return pl.pallas_call(
        _s5_state_in_kernel,
        grid=(nt, r // rb),
        in_specs=[pl.BlockSpec((t, rb, LANES), lambda q, i: (0, i, q)),
                  pl.BlockSpec((1, k, nc), lambda q, i: (q, 0, 0)),
                  pl.BlockSpec(rep.shape, lambda q, i: (0, 0))],
        out_specs=[pl.BlockSpec((rb, n // 2), lambda q, i: (i, q)),
                   pl.BlockSpec((rb, n // 2), lambda q, i: (i, q))],
        out_shape=[jax.ShapeDtypeStruct((r, nt * n // 2), F32)] * 2,
        scratch_shapes=[pltpu.VMEM((k, n), BF16)],
        compiler_params=_cparams("parallel", "arbitrary"),
        name="s5_state_in",
    )(us3, win_cat, rep)


def _s5_recurrence_kernel(sre_ref, sim_ref, pre_ref, pim_ref, hre_ref, him_ref, *, bsz, n_chunks):
    tn = sre_ref.shape[1]
    shape = (SUBLANES, tn)
    row = lax.broadcasted_iota(jnp.int32, shape, 0)
    pw_re = pre_ref[0:SUBLANES, :]
    pw_im = pim_ref[0:SUBLANES, :]
    a_tile_re = jnp.broadcast_to(pre_ref[SUBLANES:SUBLANES + 1, :], shape)
    a_tile_im = jnp.broadcast_to(pim_ref[SUBLANES:SUBLANES + 1, :], shape)

    def tile_step(it, carry):
        out = []
        for b in range(bsz):
            h_re, h_im = carry[2 * b], carry[2 * b + 1]
            rows = pl.ds(pl.multiple_of(b * n_chunks + it * SUBLANES, SUBLANES), SUBLANES)
            y_re, y_im = sre_ref[rows, :], sim_ref[rows, :]
            for d in (1, 2, 4):
                m_re = jnp.broadcast_to(pre_ref[d:d + 1, :], shape)
                m_im = jnp.broadcast_to(pim_ref[d:d + 1, :], shape)
                s_re = jnp.where(row >= d, pltpu.roll(y_re, d, axis=0), 0.0)
                s_im = jnp.where(row >= d, pltpu.roll(y_im, d, axis=0), 0.0)
                y_re, y_im = (y_re + m_re * s_re - m_im * s_im,
                              y_im + m_re * s_im + m_im * s_re)
            e_re = jnp.where(row >= 1, pltpu.roll(y_re, 1, axis=0), 0.0)
            e_im = jnp.where(row >= 1, pltpu.roll(y_im, 1, axis=0), 0.0)
            hre_ref[rows, :] = pw_re * h_re - pw_im * h_im + e_re
            him_ref[rows, :] = pw_re * h_im + pw_im * h_re + e_im
            last_re = jnp.broadcast_to(y_re[SUBLANES - 1:SUBLANES, :], shape)
            last_im = jnp.broadcast_to(y_im[SUBLANES - 1:SUBLANES, :], shape)
            out.append(a_tile_re * h_re - a_tile_im * h_im + last_re)
            out.append(a_tile_re * h_im + a_tile_im * h_re + last_im)
        return tuple(out)

    zero = jnp.zeros(shape, F32)
    lax.fori_loop(0, n_chunks // SUBLANES, tile_step, (zero,) * (2 * bsz))


def _s5_recurrence(s_re, s_im, pw_re, pw_im, *, bsz, tn):
    r, n = s_re.shape
    tn = min(tn, n)
    spec = pl.BlockSpec((r, tn), lambda j: (0, j))
    pspec = pl.BlockSpec((pw_re.shape[0], tn), lambda j: (0, j))
    return pl.pallas_call(
        functools.partial(_s5_recurrence_kernel, bsz=bsz, n_chunks=r // bsz),
        grid=(n // tn,),
        in_specs=[spec, spec, pspec, pspec],
        out_specs=[spec, spec],
        out_shape=[jax.ShapeDtypeStruct((r, n), F32)] * 2,
        compiler_params=_cparams("parallel"),
        name="s5_recurrence",
    )(s_re, s_im, pw_re, pw_im)


def _s5_out_kernel(u_ref, kc_ref, hre_ref, him_ref, wc_ref, rep_ref, y_ref, t_scr, w_scr):
    @pl.when(pl.program_id(1) == 0)
    def _():
        rep = rep_ref[...]
        col_group = lambda c: _group_of(c, LANES, SSM_GROUP)
        taps = _expand_block_diag(kc_ref[0], rep, lambda r: _group_of(r, LANES, SSM_GROUP), col_group)
        t_scr[...] = jnp.zeros_like(t_scr)
        for j in range(CHUNK):
            t_scr[j * LANES:(j + 1) * LANES, j * LANES:] = taps[:, :(CHUNK - j) * LANES]
        w_scr[...] = _expand_block_diag(
            wc_ref[0], rep, lambda r: _group_of(r, GROUPS_PER_TILE * SSM_STATE, SSM_STATE), col_group)

    h = jnp.concatenate([hre_ref[...], him_ref[...]], axis=1).astype(BF16)
    y = jnp.dot(_chunk_lhs(u_ref), t_scr[...], preferred_element_type=F32)
    y += jnp.dot(h, w_scr[...], preferred_element_type=F32)
    for i in range(CHUNK):
        y_ref[i] = y[:, i * LANES:(i + 1) * LANES].astype(y_ref.dtype)


def _s5_out(us3, k_cat, h_re, h_im, wo_cat, *, rb):
    t, r, _ = us3.shape
    nt, ks, kc = wo_cat.shape
    k = t * LANES
    rb = min(rb, r)
    rep = _lane_repeat(t, SSM_GROUP, GROUPS_PER_TILE)
    return pl.pallas_call(
        _s5_out_kernel,
        grid=(nt, r // rb),
        in_specs=[pl.BlockSpec((t, rb, LANES), lambda q, i: (0, i, q)),
                  pl.BlockSpec((1, LANES, kc), lambda q, i: (q, 0, 0)),
                  pl.BlockSpec((rb, ks // 2), lambda q, i: (i, q)),
                  pl.BlockSpec((rb, ks // 2), lambda q, i: (i, q)),
                  pl.BlockSpec((1, ks, kc), lambda q, i: (q, 0, 0)),
                  pl.BlockSpec(rep.shape, lambda q, i: (0, 0))],
        out_specs=pl.BlockSpec((t, rb, LANES), lambda q, i: (0, i, q)),
        out_shape=jax.ShapeDtypeStruct((t, r, nt * LANES), BF16),
        scratch_shapes=[pltpu.VMEM((k, k), BF16), pltpu.VMEM((ks, k), BF16)],
        compiler_params=_cparams("parallel", "arbitrary"),
        name="s5_out",
    )(us3, k_cat, h_re, h_im, wo_cat, rep)


def _s5_glu_kernel(y_ref, gate_ref, w_ref, b_ref, pt_ref, o_ref):
    t, rb, w = y_ref.shape
    y = jax.nn.gelu(y_ref[...].reshape(t * rb, w).astype(F32))
    z = jnp.dot(y.astype(BF16), w_ref[...], preferred_element_type=F32) + b_ref[...]
    y = y * jax.nn.sigmoid(z)
    y = (y * jax.nn.silu(gate_ref[...].reshape(t * rb, w).astype(F32))).astype(BF16)
    o_ref[...] = jnp.dot(pt_ref[...], y, preferred_element_type=F32).astype(o_ref.dtype)


def _s5_glu(y3, us3, w_glu, b_glu, *, rb):
    t, r, w = y3.shape
    rb = min(rb, r)
    rows = t * rb
    return pl.pallas_call(
        _s5_glu_kernel,
        grid=(r // rb,),
        in_specs=[pl.BlockSpec((t, rb, w), lambda i: (0, i, 0)),
                  pl.BlockSpec((t, rb, w), lambda i: (0, i, 1)),
                  pl.BlockSpec((w, w), lambda i: (0, 0)),
                  pl.BlockSpec((1, w), lambda i: (0, 0)),
                  pl.BlockSpec((rows, rows), lambda i: (0, 0))],
        out_specs=pl.BlockSpec((rows, w), lambda i: (i, 0)),
        out_shape=jax.ShapeDtypeStruct((t * r, w), BF16),
        compiler_params=_cparams("parallel"),
        name="s5_glu",
    )(y3, us3, w_glu, b_glu.reshape(1, w), _chunk_major_perm(rows).T)


def _out_proj_kernel(ys_ref, yd_ref, yx_ref, w_ref, x_ref, g_ref, o_ref):
    ws, wd = ys_ref.shape[1], yd_ref.shape[1]
    mix = jnp.dot(ys_ref[...], w_ref[0:ws, :], preferred_element_type=F32)
    mix += jnp.dot(yd_ref[...], w_ref[ws:ws + wd, :], preferred_element_type=F32)
    mix += jnp.dot(yx_ref[...], w_ref[ws + wd:, :], preferred_element_type=F32)
    var = jnp.mean(mix * mix, axis=-1, keepdims=True)
    o_ref[...] = x_ref[...] + mix * lax.rsqrt(var + NORM_EPS) * g_ref[...]


def _out_proj(y_s, y_d, y_x, w_out, x, g_post, *, tm):
    m, d = x.shape
    tm = min(tm, m)
    row = lambda i: (i, 0)
    const = lambda i: (0, 0)
    return pl.pallas_call(
        _out_proj_kernel,
        grid=(m // tm,),
        in_specs=[pl.BlockSpec((tm, y_s.shape[1]), row),
                  pl.BlockSpec((tm, y_d.shape[1]), row),
                  pl.BlockSpec((tm, y_x.shape[1]), row),
                  pl.BlockSpec(w_out.shape, const),
                  pl.BlockSpec((tm, d), row),
                  pl.BlockSpec((1, d), const)],
        out_specs=pl.BlockSpec((tm, d), row),
        out_shape=jax.ShapeDtypeStruct((m, d), F32),
        compiler_params=_cparams("parallel"),
        name="out_proj",
    )(y_s, y_d, y_x, w_out, x, g_post.reshape(1, d))


def kernel(x, mem, positions, norm_pre, norm_post, norm_mem, w_in, w_out, w_mem_kv, ssm_a_re, ssm_a_im, ssm_log_dt, ssm_b_re, ssm_b_im, ssm_c_re, ssm_c_im, ssm_d, w_glu, b_glu, diff_lq1, diff_lk1, diff_lq2, diff_lk2, diff_subln):
    bsz, seq, d_model = x.shape
    mem_tokens = mem.shape[1]
    depth = w_in.shape[0]
    m = bsz * seq
    ssm_width = ssm_a_re.shape[1] * SSM_GROUP
    n_groups = ssm_width // SSM_GROUP
    diff_width = DIFF_HEADS * 2 * DIFF_HEAD_DIM
    xattn_width = XATTN_HEADS * XATTN_HEAD_DIM
    n_chunks = seq // CHUNK
    n_s5_cols = 2 * ssm_width
    col_qd = 0
    col_kd = col_qd + diff_width
    col_vd = col_kd + diff_width
    col_gd = col_vd + diff_width
    col_qx = col_gd + diff_width
    col_gx = col_qx + xattn_width
    assert col_kd == col_qd + diff_width and col_qd % diff_width == 0

    inv = ROPE_THETA ** (-jnp.arange(0, DIFF_HEAD_DIM, 2, dtype=F32) / DIFF_HEAD_DIM)
    ang = positions.astype(F32).reshape(m, 1) * inv
    cos, sin = jnp.cos(ang), jnp.sin(ang)
    cos_t = jnp.concatenate([cos, cos, cos, cos], axis=-1)
    sin_t = jnp.concatenate([-sin, sin, -sin, sin], axis=-1)

    xf = x.reshape(m, d_model)
    mem_f = mem.reshape(bsz * mem_tokens, d_model)
    for l in range(depth):
        lambda_init = 0.8 - 0.6 * math.exp(-0.3 * l)
        us3, proj = _in_proj(xf, norm_pre[l], w_in[l].astype(BF16), n_perm_cols=n_s5_cols, tm=1024, tn=512,
                              perm_rows=512)

        k_cat, win_cat, wo_cat, a_re, a_im = _s5_weights(
            ssm_a_re[l], ssm_a_im[l], ssm_log_dt[l], ssm_b_re[l], ssm_b_im[l],
            ssm_c_re[l], ssm_c_im[l], ssm_d[l])
        s_re, s_im = _s5_state_in(us3, win_cat, rb=512)
        h_re, h_im = _s5_recurrence(s_re, s_im, a_re, a_im, bsz=bsz, tn=1024)
        y3 = _s5_out(us3, k_cat, h_re, h_im, wo_cat, rb=512)
        y_s = _s5_glu(y3, us3, w_glu[l].astype(BF16), b_glu[l], rb=64)

        q_rot, k_rot = _rope(proj, cos_t, sin_t, q_col_block=col_qd // diff_width, tm=1024)
        lam = (jnp.exp(jnp.sum(diff_lq1[l].astype(F32) * diff_lk1[l].astype(F32)))
               - jnp.exp(jnp.sum(diff_lq2[l].astype(F32) * diff_lk2[l].astype(F32))) + lambda_init)
        sg_vec = (diff_subln[l].astype(F32) * (1.0 - lambda_init)).reshape(1, LANES)
        v_t = proj[:, col_vd:col_vd + diff_width].reshape(bsz, seq, DIFF_HEADS, 2 * DIFF_HEAD_DIM)
        v_t = v_t.transpose(0, 2, 3, 1)
        vt_ext = jnp.concatenate([v_t, jnp.ones((bsz, DIFF_HEADS, BF16_SUBLANES, seq), BF16)], axis=2)
        vt_ext = vt_ext.reshape(bsz * DIFF_HEADS, 2 * DIFF_HEAD_DIM + BF16_SUBLANES, seq)
        y_d = _diff_attn(q_rot, k_rot, vt_ext, proj, lam.reshape(1), sg_vec, bsz=bsz, seq=seq,
                         g_col=col_gd // LANES, tq=512)

        mem_kv = _norm_matmul(mem_f, norm_mem[l], w_mem_kv[l].astype(BF16), tm=512, tn=1024, name="mem_kv")
        y_x = _mem_attn(proj, mem_kv, bsz=bsz, seq=seq, mem_tokens=mem_tokens,
                        q_col_block=col_qx // xattn_width, g_col_block=col_gx // xattn_width, tq=1024)

        xf = _out_proj(y_s, y_d, y_x, w_out[l].astype(BF16), xf, norm_post[l], tm=256)
    return xf.reshape(bsz, seq, d_model)
```

```python
import functools
import math

import jax
import jax.numpy as jnp
import numpy as np
from jax import lax
from jax.experimental import pallas as pl
from jax.experimental.pallas import tpu as pltpu

F32 = jnp.float32
BF16 = jnp.bfloat16

SSM_GROUP = 16
SSM_STATE = 64
CHUNK = 16
DIFF_HEADS = 4
DIFF_HEAD_DIM = 64
XATTN_HEADS = 4
XATTN_HEAD_DIM = 128
ROPE_THETA = 10000.0
NORM_EPS = 1e-6
MASK_VALUE = -1e30
LANES = 128
SUBLANES = 8
BF16_SUBLANES = 16
GROUPS_PER_TILE = LANES // SSM_GROUP
VMEM_LIMIT = 56 * 1024 * 1024


def _cparams(*sem):
    return pltpu.CompilerParams(dimension_semantics=sem, vmem_limit_bytes=VMEM_LIMIT)


def _norm_matmul_kernel(x_ref, g_ref, w_ref, o_ref, h_ref):
    @pl.when(pl.program_id(1) == 0)
    def _():
        x = x_ref[...]
        var = jnp.mean(x * x, axis=-1, keepdims=True)
        h_ref[...] = (x * lax.rsqrt(var + NORM_EPS) * g_ref[...]).astype(BF16)

    o_ref[...] = jnp.dot(h_ref[...], w_ref[...], preferred_element_type=F32).astype(o_ref.dtype)


def _norm_matmul(x, g, w, *, tm, tn, name):
    m, d = x.shape
    n = w.shape[1]
    tm, tn = min(tm, m), min(tn, n)
    return pl.pallas_call(
        _norm_matmul_kernel,
        grid=(m // tm, n // tn),
        in_specs=[pl.BlockSpec((tm, d), lambda i, j: (i, 0)),
                  pl.BlockSpec((1, d), lambda i, j: (0, 0)),
                  pl.BlockSpec((d, tn), lambda i, j: (0, j))],
        out_specs=pl.BlockSpec((tm, tn), lambda i, j: (i, j)),
        out_shape=jax.ShapeDtypeStruct((m, n), BF16),
        scratch_shapes=[pltpu.VMEM((tm, d), BF16)],
        compiler_params=_cparams("parallel", "arbitrary"),
        name=name,
    )(x, g.reshape(1, d), w)


def _chunk_major_perm(rows):
    rb = rows // CHUNK
    src = (np.arange(rows) % rb) * CHUNK + np.arange(rows) // rb
    return jnp.asarray(np.eye(rows, dtype=np.float32)[src], BF16)


def _in_proj_kernel(x_ref, g_ref, w_ref, p_ref, o3_ref, o_ref, hp_ref, hn_ref, *, n_perm):
    j = pl.program_id(1)
    tm = x_ref.shape[0]
    pr = p_ref.shape[0]
    rb = pr // CHUNK

    @pl.when(j == 0)
    def _():
        for s in range(tm // pr):
            x = x_ref[s * pr:(s + 1) * pr, :]
            var = jnp.mean(x * x, axis=-1, keepdims=True)
            hn = (x * lax.rsqrt(var + NORM_EPS) * g_ref[...]).astype(BF16)
            hn_ref[s * pr:(s + 1) * pr, :] = hn
            hp_ref[s * pr:(s + 1) * pr, :] = jnp.dot(p_ref[...], hn, preferred_element_type=F32).astype(BF16)

    @pl.when(j < n_perm)
    def _():
        res = jnp.dot(hp_ref[...], w_ref[...], preferred_element_type=F32)
        for s in range(tm // pr):
            for i in range(CHUNK):
                rows = slice(s * pr + i * rb, s * pr + (i + 1) * rb)
                o3_ref[i, s * rb:(s + 1) * rb, :] = res[rows].astype(o3_ref.dtype)

    @pl.when(j >= n_perm)
    def _():
        o_ref[...] = jnp.dot(hn_ref[...], w_ref[...], preferred_element_type=F32).astype(o_ref.dtype)


def _in_proj(x, g, w, layer, *, n_perm_cols, tm, tn, perm_rows):
    m, d = x.shape
    n = w.shape[2]
    tm = min(tm, m)
    perm_rows = min(perm_rows, tm)
    n_perm = n_perm_cols // tn
    return pl.pallas_call(
        functools.partial(_in_proj_kernel, n_perm=n_perm),
        grid=(m // tm, n // tn),
        in_specs=[pl.BlockSpec((tm, d), lambda i, j: (i, 0)),
                  pl.BlockSpec((1, d), lambda i, j: (0, 0)),
                  pl.BlockSpec((None, d, tn), lambda i, j: (layer, 0, j)),
                  pl.BlockSpec((perm_rows, perm_rows), lambda i, j: (0, 0))],
        out_specs=[pl.BlockSpec((CHUNK, tm // CHUNK, tn), lambda i, j: (0, i, jnp.minimum(j, n_perm - 1))),
                   pl.BlockSpec((tm, tn), lambda i, j: (i, jnp.maximum(j - n_perm, 0)))],
        out_shape=[jax.ShapeDtypeStruct((CHUNK, m // CHUNK, n_perm_cols), BF16),
                   jax.ShapeDtypeStruct((m, n - n_perm_cols), BF16)],
        scratch_shapes=[pltpu.VMEM((tm, d), BF16), pltpu.VMEM((tm, d), BF16)],
        compiler_params=_cparams("parallel", "arbitrary"),
        name="in_proj",
    )(x, g.reshape(1, d), w, _chunk_major_perm(perm_rows))


def _rope_kernel(q_ref, k_ref, v_ref, cos_ref, sin_ref, qo_ref, ko_ref, vt_ref, *, q_scale):
    cos = cos_ref[...]
    sin = sin_ref[...]
    lane = lax.broadcasted_iota(jnp.int32, cos.shape, 1)
    first_half = (lane % DIFF_HEAD_DIM) < (DIFF_HEAD_DIM // 2)
    for x_ref, o_ref, mult in ((q_ref, qo_ref, q_scale), (k_ref, ko_ref, 1.0)):
        for h in range(DIFF_HEADS):
            x = x_ref[:, h * LANES:(h + 1) * LANES].astype(F32)
            partner = jnp.where(first_half,
                                pltpu.roll(x, LANES - DIFF_HEAD_DIM // 2, axis=1),
                                pltpu.roll(x, DIFF_HEAD_DIM // 2, axis=1))
            o_ref[:, h * LANES:(h + 1) * LANES] = ((x * cos + partner * sin) * mult).astype(o_ref.dtype)
    d = 2 * DIFF_HEAD_DIM
    for h in range(DIFF_HEADS):
        vt_ref[h, 0:d, :] = v_ref[:, h * d:(h + 1) * d].astype(F32).T.astype(vt_ref.dtype)
        vt_ref[h, d:, :] = jnp.ones((vt_ref.shape[1] - d, vt_ref.shape[2]), vt_ref.dtype)


def _rope(proj, cos_t, sin_t, *, bsz, seq, q_col_block, tm):
    m = proj.shape[0]
    tm = min(tm, seq)
    nb = seq // tm
    w = DIFF_HEADS * LANES
    v_rows = 2 * DIFF_HEAD_DIM + BF16_SUBLANES
    q_scale = DIFF_HEAD_DIM ** -0.5 * math.log2(math.e)
    return pl.pallas_call(
        functools.partial(_rope_kernel, q_scale=q_scale),
        grid=(m // tm,),
        in_specs=[pl.BlockSpec((tm, w), lambda i: (i, q_col_block)),
                  pl.BlockSpec((tm, w), lambda i: (i, q_col_block + 1)),
                  pl.BlockSpec((tm, w), lambda i: (i, q_col_block + 2)),
                  pl.BlockSpec((tm, LANES), lambda i: (i, 0)),
                  pl.BlockSpec((tm, LANES), lambda i: (i, 0))],
        out_specs=[pl.BlockSpec((tm, w), lambda i: (i, 0)),
                   pl.BlockSpec((tm, w), lambda i: (i, 0)),
                   pl.BlockSpec((DIFF_HEADS, v_rows, tm), lambda i: (i // nb, 0, i % nb))],
        out_shape=[jax.ShapeDtypeStruct((m, w), BF16),
                   jax.ShapeDtypeStruct((m, w), BF16),
                   jax.ShapeDtypeStruct((bsz * DIFF_HEADS, v_rows, seq), BF16)],
        compiler_params=_cparams("parallel"),
        name="rope",
    )(proj, proj, proj, cos_t, sin_t)


def _diff_attn_kernel(q_ref, k_ref, vt_ref, g_ref, lam_ref, sg_ref, o_ref, acc1, acc2, s_a, s_b, *, tq):
    qi = pl.program_id(2)
    q = q_ref[...]
    lane = lax.broadcasted_iota(jnp.int32, q.shape, 1)
    q_maps = (jnp.where(lane < DIFF_HEAD_DIM, q, jnp.zeros_like(q)),
              jnp.where(lane >= DIFF_HEAD_DIM, q, jnp.zeros_like(q)))
    accs = (acc1, acc2)
    acc1[...] = jnp.zeros_like(acc1)
    acc2[...] = jnp.zeros_like(acc2)

    def scores_into(s_ref, j):
        k = k_ref[pl.ds(pl.multiple_of(j * tq, tq), tq), :]
        for mi, qm in enumerate(q_maps):
            s_ref[mi] = lax.dot_general(k, qm, (((1,), (1,)), ((), ())),
                                        preferred_element_type=F32)

    def consume(s_ref, j, ms, masked):
        vt = vt_ref[0, :, pl.ds(pl.multiple_of(j * tq, tq), tq)]
        out = []
        for mi, (m_old, acc) in enumerate(zip(ms, accs)):
            s = s_ref[mi]
            if masked:
                key = lax.broadcasted_iota(jnp.int32, s.shape, 0)
                qry = lax.broadcasted_iota(jnp.int32, s.shape, 1)
                s = jnp.where(key <= qry, s, MASK_VALUE)
            m_new = jnp.maximum(m_old, jnp.max(s, axis=0, keepdims=True))
            p = jnp.exp2(s - m_new)
            acc[...] = acc[...] * jnp.exp2(m_old - m_new) + jnp.dot(vt, p.astype(BF16),
                                                                    preferred_element_type=F32)
            out.append(m_new)
        return tuple(out)

    def finish():
        d = 2 * DIFF_HEAD_DIM
        a1, a2 = acc1[...], acc2[...]
        o_t = a1[:d] / a1[d:d + 1] - lam_ref[0] * (a2[:d] / a2[d:d + 1])
        o = o_t.T
        var = jnp.mean(o * o, axis=-1, keepdims=True)
        o = o * lax.rsqrt(var + NORM_EPS) * sg_ref[...]
        o_ref[...] = (o * jax.nn.silu(g_ref[...].astype(F32))).astype(o_ref.dtype)

    scores_into(s_a, 0)

    def two_blocks(i, ms):
        scores_into(s_b, 2 * i + 1)
        ms = consume(s_a, 2 * i, ms, False)
        scores_into(s_a, 2 * i + 2)
        return consume(s_b, 2 * i + 1, ms, False)

    m0 = jnp.full((1, tq), MASK_VALUE, F32)
    ms = lax.fori_loop(0, qi // 2, two_blocks, (m0, m0))

    @pl.when(qi % 2 == 0)
    def _():
        consume(s_a, qi, ms, True)
        finish()

    @pl.when(qi % 2 == 1)
    def _():
        scores_into(s_b, qi)
        consume(s_b, qi, consume(s_a, qi - 1, ms, False), True)
        finish()


def _diff_attn(q_rot, k_rot, vt_ext, proj, lam, sg_vec, *, bsz, seq, g_col, tq):
    tq = min(tq, seq)
    nq = seq // tq
    v_rows = vt_ext.shape[1]
    return pl.pallas_call(
        functools.partial(_diff_attn_kernel, tq=tq),
        grid=(bsz, DIFF_HEADS, nq),
        in_specs=[
            pl.BlockSpec((tq, LANES), lambda b, h, i: (b * nq + i, h)),
            pl.BlockSpec((seq, LANES), lambda b, h, i: (b, h)),
            pl.BlockSpec((1, v_rows, seq), lambda b, h, i: (b * DIFF_HEADS + h, 0, 0)),
            pl.BlockSpec((tq, LANES), lambda b, h, i: (b * nq + i, g_col + h)),
            pl.BlockSpec(memory_space=pltpu.SMEM),
            pl.BlockSpec((1, LANES), lambda b, h, i: (0, 0)),
        ],
        out_specs=pl.BlockSpec((tq, LANES), lambda b, h, i: (b * nq + i, h)),
        out_shape=jax.ShapeDtypeStruct((bsz * seq, DIFF_HEADS * LANES), BF16),
        scratch_shapes=[pltpu.VMEM((v_rows, tq), F32), pltpu.VMEM((v_rows, tq), F32),
                        pltpu.VMEM((2, tq, tq), F32), pltpu.VMEM((2, tq, tq), F32)],
        compiler_params=_cparams("parallel", "parallel", "arbitrary"),
        name="diff_attn",
    )(q_rot, k_rot, vt_ext, proj, lam, sg_vec)


def _mem_attn_kernel(q_ref, g_ref, mk_ref, mv_ref, o_ref):
    scale = XATTN_HEAD_DIM ** -0.5
    for h in range(XATTN_HEADS):
        sl = slice(h * XATTN_HEAD_DIM, (h + 1) * XATTN_HEAD_DIM)
        s = lax.dot_general(q_ref[:, sl], mk_ref[:, sl], (((1,), (1,)), ((), ())),
                            preferred_element_type=F32) * scale
        p = jnp.exp(s - jnp.max(s, axis=-1, keepdims=True))
        p = p / jnp.sum(p, axis=-1, keepdims=True)
        o = jnp.dot(p.astype(BF16), mv_ref[:, sl], preferred_element_type=F32)
        o_ref[:, sl] = (o * jax.nn.silu(g_ref[:, sl].astype(F32))).astype(o_ref.dtype)


def _mem_attn(proj, mem_kv, *, bsz, seq, mem_tokens, q_col_block, g_col_block, tq):
    tq = min(tq, seq)
    nq = seq // tq
    w = XATTN_HEADS * XATTN_HEAD_DIM
    return pl.pallas_call(
        _mem_attn_kernel,
        grid=(bsz, nq),
        in_specs=[pl.BlockSpec((tq, w), lambda b, i: (b * nq + i, q_col_block)),
                  pl.BlockSpec((tq, w), lambda b, i: (b * nq + i, g_col_block)),
                  pl.BlockSpec((mem_tokens, w), lambda b, i: (b, 0)),
                  pl.BlockSpec((mem_tokens, w), lambda b, i: (b, 1))],
        out_specs=pl.BlockSpec((tq, w), lambda b, i: (b * nq + i, 0)),
        out_shape=jax.ShapeDtypeStruct((bsz * seq, w), BF16),
        compiler_params=_cparams("parallel", "parallel"),
        name="mem_attn",
    )(proj, proj, mem_kv, mem_kv)


def _s5_taps_kernel(lhs_ref, rhs_ref, o_ref):
    for s in range(lhs_ref.shape[0]):
        o_ref[s] = jnp.dot(lhs_ref[s], rhs_ref[s], preferred_element_type=F32,
                           precision=lax.Precision.HIGHEST)


def _s5_taps(lhs, rhs):
    g, r, k = lhs.shape
    n = rhs.shape[2]
    gb = GROUPS_PER_TILE
    return pl.pallas_call(
        _s5_taps_kernel,
        grid=(g // gb,),
        in_specs=[pl.BlockSpec((gb, r, k), lambda i: (i, 0, 0)),
                  pl.BlockSpec((gb, k, n), lambda i: (i, 0, 0))],
        out_specs=pl.BlockSpec((gb, r, n), lambda i: (i, 0, 0)),
        out_shape=jax.ShapeDtypeStruct((g, r, n), F32),
        compiler_params=_cparams("parallel"),
        name="s5_taps",
    )(lhs, rhs)


def _s5_weights(a_re, a_im, log_dt, b_re, b_im, c_re, c_im, d_skip):
    g = a_re.shape[0]
    p, h, t = SSM_STATE, SSM_GROUP, CHUNK
    dt = jnp.exp(log_dt.astype(F32))[:, None]
    lr, li = a_re.astype(F32), a_im.astype(F32)
    mag = jnp.exp(lr * dt)
    abar_re, abar_im = mag * jnp.cos(li * dt), mag * jnp.sin(li * dt)
    den = lr * lr + li * li
    nr, ni = abar_re - 1.0, abar_im
    z_re = (nr * lr + ni * li) / den
    z_im = (ni * lr - nr * li) / den
    br, bi = b_re.astype(F32), b_im.astype(F32)
    bbar_re = z_re[..., None] * br - z_im[..., None] * bi
    bbar_im = z_re[..., None] * bi + z_im[..., None] * br
    cr, ci = c_re.astype(F32), c_im.astype(F32)

    tau = jnp.arange(t + 1, dtype=F32)[None, :, None]
    pmag = jnp.exp(tau * (lr * dt)[:, None, :])
    pw_re = pmag * jnp.cos(tau * (li * dt)[:, None, :])
    pw_im = pmag * jnp.sin(tau * (li * dt)[:, None, :])

    ca_re = cr[:, None] * pw_re[:, :, None, :] - ci[:, None] * pw_im[:, :, None, :]
    ca_im = cr[:, None] * pw_im[:, :, None, :] + ci[:, None] * pw_re[:, :, None, :]

    lhs = jnp.concatenate([ca_re[:, :t], ca_im[:, :t]], axis=-1).reshape(g, t * h, 2 * p)
    rhs = jnp.concatenate([bbar_re, -bbar_im], axis=1)
    taps = _s5_taps(lhs, rhs).reshape(g, t, h, h)
    taps = taps.at[:, 0].add(jax.vmap(jnp.diag)(d_skip.astype(F32)))

    gt = GROUPS_PER_TILE
    nt = g // gt
    idx = jnp.arange(t)

    k_cat = taps.transpose(0, 3, 1, 2).reshape(nt, gt * h, t * h)

    pj_re, pj_im = pw_re[:, t - 1 - idx], pw_im[:, t - 1 - idx]
    bt_re, bt_im = bbar_re.transpose(0, 2, 1)[:, None], bbar_im.transpose(0, 2, 1)[:, None]
    win_re = pj_re[:, :, None, :] * bt_re - pj_im[:, :, None, :] * bt_im
    win_im = pj_re[:, :, None, :] * bt_im + pj_im[:, :, None, :] * bt_re
    win = jnp.stack([win_re, win_im], axis=3).reshape(nt, gt, t, h, 2, p)
    win_cat = win.transpose(0, 2, 1, 3, 4, 5).reshape(nt, t * gt * h, 2 * p)

    wo = jnp.stack([ca_re[:, 1:], -ca_im[:, 1:]], axis=1).reshape(nt, gt, 2, t, h, p)
    wo_cat = wo.transpose(0, 2, 1, 5, 3, 4).reshape(nt, 2 * gt * p, t * h)

    kk = (t * jnp.arange(2 * SUBLANES, dtype=F32))[:, None, None]
    cmag = jnp.exp(kk * (lr * dt)[None])
    a_chunk_re = (cmag * jnp.cos(kk * (li * dt)[None])).reshape(2 * SUBLANES, g * p)
    a_chunk_im = (cmag * jnp.sin(kk * (li * dt)[None])).reshape(2 * SUBLANES, g * p)
    return k_cat.astype(BF16), win_cat.astype(BF16), wo_cat.astype(BF16), a_chunk_re, a_chunk_im


def _lane_repeat(blocks, width, copies):
    r = np.kron(np.eye(blocks, dtype=np.float32),
                np.kron(np.ones((1, copies), np.float32), np.eye(width, dtype=np.float32)))
    return jnp.asarray(r, BF16)


def _group_of(index, period, size):
    assert period & (period - 1) == 0 and size & (size - 1) == 0
    return lax.shift_right_logical(index & (period - 1), size.bit_length() - 1)


def _expand_block_diag(compact, rep, row_group, col_group):
    e = jnp.dot(compact, rep, preferred_element_type=F32)
    r = lax.broadcasted_iota(jnp.int32, e.shape, 0)
    c = lax.broadcasted_iota(jnp.int32, e.shape, 1)
    return jnp.where(row_group(r) == col_group(c), e, 0.0).astype(BF16)


def _chunk_lhs(u_ref):
    return jnp.concatenate([u_ref[j] for j in range(CHUNK)], axis=1)


def _s5_state_in_kernel(u_ref, wc_ref, rep_ref, sre_ref, sim_ref, w_scr):
    @pl.when(pl.program_id(1) == 0)
    def _():
        w_scr[...] = _expand_block_diag(
            wc_ref[0], rep_ref[...],
            lambda r: _group_of(r, LANES, SSM_GROUP),
            lambda c: _group_of(c, GROUPS_PER_TILE * SSM_STATE, SSM_STATE))

    s = jnp.dot(_chunk_lhs(u_ref), w_scr[...], preferred_element_type=F32)
    half = s.shape[1] // 2
    sre_ref[...] = s[:, :half]
    sim_ref[...] = s[:, half:]


def _s5_state_in(us3, win_cat, *, rb):
    t, r, _ = us3.shape
    nt, k, nc = win_cat.shape
    n = nc * GROUPS_PER_TILE
    rb = min(rb, r)
    rep = _lane_repeat(2, SSM_STATE, GROUPS_PER_TILE)
    return pl.pallas_call(
        _s5_state_in_kernel,
        grid=(nt, r // rb),
        in_specs=[pl.BlockSpec((t, rb, LANES), lambda q, i: (0, i, q)),
                  pl.BlockSpec((1, k, nc), lambda q, i: (q, 0, 0)),
                  pl.BlockSpec(rep.shape, lambda q, i: (0, 0))],
        out_specs=[pl.BlockSpec((rb, n // 2), lambda q, i: (i, q)),
                   pl.BlockSpec((rb, n // 2), lambda q, i: (i, q))],
        out_shape=[jax.ShapeDtypeStruct((r, nt * n // 2), F32)] * 2,
        scratch_shapes=[pltpu.VMEM((k, n), BF16)],
        compiler_params=_cparams("parallel", "arbitrary"),
        name="s5_state_in",
    )(us3, win_cat, rep)


def _s5_recurrence_kernel(sre_ref, sim_ref, pre_ref, pim_ref, hre_ref, him_ref, *, bsz, n_chunks):
    tn = sre_ref.shape[1]
    shape = (SUBLANES, tn)
    row = lax.broadcasted_iota(jnp.int32, shape, 0)
    pw_re = pre_ref[0:SUBLANES, :]
    pw_im = pim_ref[0:SUBLANES, :]
    a_tile_re = jnp.broadcast_to(pre_ref[SUBLANES:SUBLANES + 1, :], shape)
    a_tile_im = jnp.broadcast_to(pim_ref[SUBLANES:SUBLANES + 1, :], shape)

    def tile_step(it, carry):
        out = []
        for b in range(bsz):
            h_re, h_im = carry[2 * b], carry[2 * b + 1]
            rows = pl.ds(pl.multiple_of(b * n_chunks + it * SUBLANES, SUBLANES), SUBLANES)
            y_re, y_im = sre_ref[rows, :], sim_ref[rows, :]
            for d in (1, 2, 4):
                m_re = jnp.broadcast_to(pre_ref[d:d + 1, :], shape)
                m_im = jnp.broadcast_to(pim_ref[d:d + 1, :], shape)
                s_re = jnp.where(row >= d, pltpu.roll(y_re, d, axis=0), 0.0)
                s_im = jnp.where(row >= d, pltpu.roll(y_im, d, axis=0), 0.0)
                y_re, y_im = (y_re + m_re * s_re - m_im * s_im,
                              y_im + m_re * s_im + m_im * s_re)
            e_re = jnp.where(row >= 1, pltpu.roll(y_re, 1, axis=0), 0.0)
            e_im = jnp.where(row >= 1, pltpu.roll(y_im, 1, axis=0), 0.0)
            hre_ref[rows, :] = pw_re * h_re - pw_im * h_im + e_re
            him_ref[rows, :] = pw_re * h_im + pw_im * h_re + e_im
            last_re = jnp.broadcast_to(y_re[SUBLANES - 1:SUBLANES, :], shape)
            last_im = jnp.broadcast_to(y_im[SUBLANES - 1:SUBLANES, :], shape)
            out.append(a_tile_re * h_re - a_tile_im * h_im + last_re)
            out.append(a_tile_re * h_im + a_tile_im * h_re + last_im)
        return tuple(out)

    zero = jnp.zeros(shape, F32)
    lax.fori_loop(0, n_chunks // SUBLANES, tile_step, (zero,) * (2 * bsz))


def _s5_recurrence(s_re, s_im, pw_re, pw_im, *, bsz, tn):
    r, n = s_re.shape
    tn = min(tn, n)
    spec = pl.BlockSpec((r, tn), lambda j: (0, j))
    pspec = pl.BlockSpec((pw_re.shape[0], tn), lambda j: (0, j))
    return pl.pallas_call(
        functools.partial(_s5_recurrence_kernel, bsz=bsz, n_chunks=r // bsz),
        grid=(n // tn,),
        in_specs=[spec, spec, pspec, pspec],
        out_specs=[spec, spec],
        out_shape=[jax.ShapeDtypeStruct((r, n), F32)] * 2,
        compiler_params=_cparams("parallel"),
        name="s5_recurrence",
    )(s_re, s_im, pw_re, pw_im)


def _s5_out_kernel(u_ref, kc_ref, hre_ref, him_ref, wc_ref, rep_ref, y_ref, t_scr, w_scr):
    @pl.when(pl.program_id(1) == 0)
    def _():
        rep = rep_ref[...]
        col_group = lambda c: _group_of(c, LANES, SSM_GROUP)
        taps = _expand_block_diag(kc_ref[0], rep, lambda r: _group_of(r, LANES, SSM_GROUP), col_group)
        t_scr[...] = jnp.zeros_like(t_scr)
        for j in range(CHUNK):
            t_scr[j * LANES:(j + 1) * LANES, j * LANES:] = taps[:, :(CHUNK - j) * LANES]
        w_scr[...] = _expand_block_diag(
            wc_ref[0], rep, lambda r: _group_of(r, GROUPS_PER_TILE * SSM_STATE, SSM_STATE), col_group)

    h = jnp.concatenate([hre_ref[...], him_ref[...]], axis=1).astype(BF16)
    y = jnp.dot(_chunk_lhs(u_ref), t_scr[...], preferred_element_type=F32)
    y += jnp.dot(h, w_scr[...], preferred_element_type=F32)
    for i in range(CHUNK):
        y_ref[i] = y[:, i * LANES:(i + 1) * LANES].astype(y_ref.dtype)


def _s5_out(us3, k_cat, h_re, h_im, wo_cat, *, rb):
    t, r, _ = us3.shape
    nt, ks, kc = wo_cat.shape
    k = t * LANES
    rb = min(rb, r)
    rep = _lane_repeat(t, SSM_GROUP, GROUPS_PER_TILE)
    return pl.pallas_call(
        _s5_out_kernel,
        grid=(nt, r // rb),
        in_specs=[pl.BlockSpec((t, rb, LANES), lambda q, i: (0, i, q)),
                  pl.BlockSpec((1, LANES, kc), lambda q, i: (q, 0, 0)),
                  pl.BlockSpec((rb, ks // 2), lambda q, i: (i, q)),
                  pl.BlockSpec((rb, ks // 2), lambda q, i: (i, q)),
                  pl.BlockSpec((1, ks, kc), lambda q, i: (q, 0, 0)),
                  pl.BlockSpec(rep.shape, lambda q, i: (0, 0))],
        out_specs=pl.BlockSpec((t, rb, LANES), lambda q, i: (0, i, q)),
        out_shape=jax.ShapeDtypeStruct((t, r, nt * LANES), BF16),
        scratch_shapes=[pltpu.VMEM((k, k), BF16), pltpu.VMEM((ks, k), BF16)],
        compiler_params=_cparams("parallel", "arbitrary"),
        name="s5_out",
    )(us3, k_cat, h_re, h_im, wo_cat, rep)


def _s5_glu_kernel(y_ref, gate_ref, w_ref, b_ref, pt_ref, o_ref):
    t, rb, w = y_ref.shape
    y = jax.nn.gelu(y_ref[...].reshape(t * rb, w).astype(F32))
    z = jnp.dot(y.astype(BF16), w_ref[...], preferred_element_type=F32) + b_ref[...]
    y = y * jax.nn.sigmoid(z)
    y = (y * jax.nn.silu(gate_ref[...].reshape(t * rb, w).astype(F32))).astype(BF16)
    o_ref[...] = jnp.dot(pt_ref[...], y, preferred_element_type=F32).astype(o_ref.dtype)


def _s5_glu(y3, us3, w_glu, b_glu, *, rb):
    t, r, w = y3.shape
    rb = min(rb, r)
    rows = t * rb
    return pl.pallas_call(
        _s5_glu_kernel,
        grid=(r // rb,),
        in_specs=[pl.BlockSpec((t, rb, w), lambda i: (0, i, 0)),
                  pl.BlockSpec((t, rb, w), lambda i: (0, i, 1)),
                  pl.BlockSpec((w, w), lambda i: (0, 0)),
                  pl.BlockSpec((1, w), lambda i: (0, 0)),
                  pl.BlockSpec((rows, rows), lambda i: (0, 0))],
        out_specs=pl.BlockSpec((rows, w), lambda i: (i, 0)),
        out_shape=jax.ShapeDtypeStruct((t * r, w), BF16),
        compiler_params=_cparams("parallel"),
        name="s5_glu",
    )(y3, us3, w_glu, b_glu.reshape(1, w), _chunk_major_perm(rows).T)


def _out_proj_kernel(ys_ref, yd_ref, yx_ref, w_ref, x_ref, g_ref, o_ref):
    ws, wd = ys_ref.shape[1], yd_ref.shape[1]
    mix = jnp.dot(ys_ref[...], w_ref[0:ws, :], preferred_element_type=F32)
    mix += jnp.dot(yd_ref[...], w_ref[ws:ws + wd, :], preferred_element_type=F32)
    mix += jnp.dot(yx_ref[...], w_ref[ws + wd:, :], preferred_element_type=F32)
    var = jnp.mean(mix * mix, axis=-1, keepdims=True)
    o_ref[...] = x_ref[...] + mix * lax.rsqrt(var + NORM_EPS) * g_ref[...]


def _out_proj(y_s, y_d, y_x, w_out, layer, x, g_post, *, tm):
    m, d = x.shape
    tm = min(tm, m)
    row = lambda i: (i, 0)
    const = lambda i: (0, 0)
    return pl.pallas_call(
        _out_proj_kernel,
        grid=(m // tm,),
        in_specs=[pl.BlockSpec((tm, y_s.shape[1]), row),
                  pl.BlockSpec((tm, y_d.shape[1]), row),
                  pl.BlockSpec((tm, y_x.shape[1]), row),
                  pl.BlockSpec((None,) + w_out.shape[1:], lambda i: (layer, 0, 0)),
                  pl.BlockSpec((tm, d), row),
                  pl.BlockSpec((1, d), const)],
        out_specs=pl.BlockSpec((tm, d), row),
        out_shape=jax.ShapeDtypeStruct((m, d), F32),
        compiler_params=_cparams("parallel"),
        name="out_proj",
    )(y_s, y_d, y_x, w_out, x, g_post.reshape(1, d))


def kernel(x, mem, positions, norm_pre, norm_post, norm_mem, w_in, w_out, w_mem_kv, ssm_a_re, ssm_a_im, ssm_log_dt, ssm_b_re, ssm_b_im, ssm_c_re, ssm_c_im, ssm_d, w_glu, b_glu, diff_lq1, diff_lk1, diff_lq2, diff_lk2, diff_subln):
    bsz, seq, d_model = x.shape
    mem_tokens = mem.shape[1]
    depth = w_in.shape[0]
    m = bsz * seq
    ssm_width = ssm_a_re.shape[1] * SSM_GROUP
    n_groups = ssm_width // SSM_GROUP
    diff_width = DIFF_HEADS * 2 * DIFF_HEAD_DIM
    xattn_width = XATTN_HEADS * XATTN_HEAD_DIM
    n_chunks = seq // CHUNK
    n_s5_cols = 2 * ssm_width
    col_qd = 0
    col_kd = col_qd + diff_width
    col_vd = col_kd + diff_width
    col_gd = col_vd + diff_width
    col_qx = col_gd + diff_width
    col_gx = col_qx + xattn_width
    assert col_kd == col_qd + diff_width and col_qd % diff_width == 0

    inv = ROPE_THETA ** (-jnp.arange(0, DIFF_HEAD_DIM, 2, dtype=F32) / DIFF_HEAD_DIM)
    ang = positions.astype(F32).reshape(m, 1) * inv
    cos, sin = jnp.cos(ang), jnp.sin(ang)
    cos_t = jnp.concatenate([cos, cos, cos, cos], axis=-1)
    sin_t = jnp.concatenate([-sin, sin, -sin, sin], axis=-1)

    xf = x.reshape(m, d_model)
    mem_f = mem.reshape(bsz * mem_tokens, d_model)
    w_in_bf, w_out_bf = w_in.astype(BF16), w_out.astype(BF16)
    for l in range(depth):
        lambda_init = 0.8 - 0.6 * math.exp(-0.3 * l)
        us3, proj = _in_proj(xf, norm_pre[l], w_in_bf, l, n_perm_cols=n_s5_cols, tm=1024, tn=512,
                             perm_rows=512)

        k_cat, win_cat, wo_cat, a_re, a_im = _s5_weights(
            ssm_a_re[l], ssm_a_im[l], ssm_log_dt[l], ssm_b_re[l], ssm_b_im[l],
            ssm_c_re[l], ssm_c_im[l], ssm_d[l])
        s_re, s_im = _s5_state_in(us3, win_cat, rb=512)
        h_re, h_im = _s5_recurrence(s_re, s_im, a_re, a_im, bsz=bsz, tn=1024)
        y3 = _s5_out(us3, k_cat, h_re, h_im, wo_cat, rb=512)
        y_s = _s5_glu(y3, us3, w_glu[l].astype(BF16), b_glu[l], rb=64)

        assert col_vd == col_kd + diff_width
        q_rot, k_rot, vt_ext = _rope(proj, cos_t, sin_t, bsz=bsz, seq=seq,
                                     q_col_block=col_qd // diff_width, tm=1024)
        lam = (jnp.exp(jnp.sum(diff_lq1[l].astype(F32) * diff_lk1[l].astype(F32)))
               - jnp.exp(jnp.sum(diff_lq2[l].astype(F32) * diff_lk2[l].astype(F32))) + lambda_init)
        sg_vec = (diff_subln[l].astype(F32) * (1.0 - lambda_init)).reshape(1, LANES)
        y_d = _diff_attn(q_rot, k_rot, vt_ext, proj, lam.reshape(1), sg_vec, bsz=bsz, seq=seq,
                         g_col=col_gd // LANES, tq=512)

        mem_kv = _norm_matmul(mem_f, norm_mem[l], w_mem_kv[l].astype(BF16), tm=512, tn=1024, name="mem_kv")
        y_x = _mem_attn(proj, mem_kv, bsz=bsz, seq=seq, mem_tokens=mem_tokens,
                        q_col_block=col_qx // xattn_width, g_col_block=col_gx // xattn_width, tq=1024)

        xf = _out_proj(y_s, y_d, y_x, w_out_bf, l, xf, norm_post[l], tm=512)
    return xf.reshape(bsz, seq, d_model)
```

```python
import functools
import math

import jax
import jax.numpy as jnp
import numpy as np
from jax import lax
from jax.experimental import pallas as pl
from jax.experimental.pallas import tpu as pltpu

F32 = jnp.float32
BF16 = jnp.bfloat16

SSM_GROUP = 16
SSM_STATE = 64
CHUNK = 16
DIFF_HEADS = 4
DIFF_HEAD_DIM = 64
XATTN_HEADS = 4
XATTN_HEAD_DIM = 128
ROPE_THETA = 10000.0
NORM_EPS = 1e-6
MASK_VALUE = -1e30
LANES = 128
SUBLANES = 8
BF16_SUBLANES = 16
GROUPS_PER_TILE = LANES // SSM_GROUP
VMEM_LIMIT = 56 * 1024 * 1024


def _cparams(*sem):
    return pltpu.CompilerParams(dimension_semantics=sem, vmem_limit_bytes=VMEM_LIMIT)


def _norm_matmul_kernel(x_ref, g_ref, w_ref, o_ref, h_ref):
    @pl.when(pl.program_id(1) == 0)
    def _():
        x = x_ref[...]
        var = jnp.mean(x * x, axis=-1, keepdims=True)
        h_ref[...] = (x * lax.rsqrt(var + NORM_EPS) * g_ref[...]).astype(BF16)

    o_ref[...] = jnp.dot(h_ref[...], w_ref[...], preferred_element_type=F32).astype(o_ref.dtype)


def _norm_matmul(x, g, w, *, tm, tn, name):
    m, d = x.shape
    n = w.shape[1]
    tm, tn = min(tm, m), min(tn, n)
    return pl.pallas_call(
        _norm_matmul_kernel,
        grid=(m // tm, n // tn),
        in_specs=[pl.BlockSpec((tm, d), lambda i, j: (i, 0)),
                  pl.BlockSpec((1, d), lambda i, j: (0, 0)),
                  pl.BlockSpec((d, tn), lambda i, j: (0, j))],
        out_specs=pl.BlockSpec((tm, tn), lambda i, j: (i, j)),
        out_shape=jax.ShapeDtypeStruct((m, n), BF16),
        scratch_shapes=[pltpu.VMEM((tm, d), BF16)],
        compiler_params=_cparams("parallel", "arbitrary"),
        name=name,
    )(x, g.reshape(1, d), w)


def _chunk_major_perm(rows):
    rb = rows // CHUNK
    src = (np.arange(rows) % rb) * CHUNK + np.arange(rows) // rb
    return jnp.asarray(np.eye(rows, dtype=np.float32)[src], BF16)


def _in_proj_kernel(x_ref, g_ref, w_ref, p_ref, o3_ref, o_ref, hp_ref, hn_ref, *, n_perm):
    j = pl.program_id(1)
    tm = x_ref.shape[0]
    pr = p_ref.shape[0]
    rb = pr // CHUNK

    @pl.when(j == 0)
    def _():
        for s in range(tm // pr):
            x = x_ref[s * pr:(s + 1) * pr, :]
            var = jnp.mean(x * x, axis=-1, keepdims=True)
            hn = (x * lax.rsqrt(var + NORM_EPS) * g_ref[...]).astype(BF16)
            hn_ref[s * pr:(s + 1) * pr, :] = hn
            hp_ref[s * pr:(s + 1) * pr, :] = jnp.dot(p_ref[...], hn, preferred_element_type=F32).astype(BF16)

    @pl.when(j < n_perm)
    def _():
        res = jnp.dot(hp_ref[...], w_ref[...], preferred_element_type=F32)
        for s in range(tm // pr):
            for i in range(CHUNK):
                rows = slice(s * pr + i * rb, s * pr + (i + 1) * rb)
                o3_ref[i, s * rb:(s + 1) * rb, :] = res[rows].astype(o3_ref.dtype)

    @pl.when(j >= n_perm)
    def _():
        o_ref[...] = jnp.dot(hn_ref[...], w_ref[...], preferred_element_type=F32).astype(o_ref.dtype)


def _in_proj(x, g, w, layer, *, n_perm_cols, tm, tn, perm_rows):
    m, d = x.shape
    n = w.shape[2]
    tm = min(tm, m)
    perm_rows = min(perm_rows, tm)
    n_perm = n_perm_cols // tn
    return pl.pallas_call(
        functools.partial(_in_proj_kernel, n_perm=n_perm),
        grid=(m // tm, n // tn),
        in_specs=[pl.BlockSpec((tm, d), lambda i, j: (i, 0)),
                  pl.BlockSpec((1, d), lambda i, j: (0, 0)),
                  pl.BlockSpec((None, d, tn), lambda i, j: (layer, 0, j)),
                  pl.BlockSpec((perm_rows, perm_rows), lambda i, j: (0, 0))],
        out_specs=[pl.BlockSpec((CHUNK, tm // CHUNK, tn), lambda i, j: (0, i, jnp.minimum(j, n_perm - 1))),
                   pl.BlockSpec((tm, tn), lambda i, j: (i, jnp.maximum(j - n_perm, 0)))],
        out_shape=[jax.ShapeDtypeStruct((CHUNK, m // CHUNK, n_perm_cols), BF16),
                   jax.ShapeDtypeStruct((m, n - n_perm_cols), BF16)],
        scratch_shapes=[pltpu.VMEM((tm, d), BF16), pltpu.VMEM((tm, d), BF16)],
        compiler_params=_cparams("parallel", "arbitrary"),
        name="in_proj",
    )(x, g.reshape(1, d), w, _chunk_major_perm(perm_rows))


def _rope_kernel(q_ref, k_ref, v_ref, cos_ref, sin_ref, qo_ref, ko_ref, vt_ref, *, q_scale):
    cos = cos_ref[...]
    sin = sin_ref[...]
    lane = lax.broadcasted_iota(jnp.int32, cos.shape, 1)
    first_half = (lane % DIFF_HEAD_DIM) < (DIFF_HEAD_DIM // 2)
    for x_ref, o_ref, mult in ((q_ref, qo_ref, q_scale), (k_ref, ko_ref, 1.0)):
        for h in range(DIFF_HEADS):
            x = x_ref[:, h * LANES:(h + 1) * LANES].astype(F32)
            partner = jnp.where(first_half,
                                pltpu.roll(x, LANES - DIFF_HEAD_DIM // 2, axis=1),
                                pltpu.roll(x, DIFF_HEAD_DIM // 2, axis=1))
            o_ref[:, h * LANES:(h + 1) * LANES] = ((x * cos + partner * sin) * mult).astype(o_ref.dtype)
    d = 2 * DIFF_HEAD_DIM
    for h in range(DIFF_HEADS):
        vt_ref[h, 0:d, :] = v_ref[:, h * d:(h + 1) * d].astype(F32).T.astype(vt_ref.dtype)
        vt_ref[h, d:, :] = jnp.ones((vt_ref.shape[1] - d, vt_ref.shape[2]), vt_ref.dtype)


def _rope(proj, cos_t, sin_t, *, bsz, seq, q_col_block, tm):
    m = proj.shape[0]
    tm = min(tm, seq)
    nb = seq // tm
    w = DIFF_HEADS * LANES
    v_rows = 2 * DIFF_HEAD_DIM + BF16_SUBLANES
    q_scale = DIFF_HEAD_DIM ** -0.5 * math.log2(math.e)
    return pl.pallas_call(
        functools.partial(_rope_kernel, q_scale=q_scale),
        grid=(m // tm,),
        in_specs=[pl.BlockSpec((tm, w), lambda i: (i, q_col_block)),
                  pl.BlockSpec((tm, w), lambda i: (i, q_col_block + 1)),
                  pl.BlockSpec((tm, w), lambda i: (i, q_col_block + 2)),
                  pl.BlockSpec((tm, LANES), lambda i: (i, 0)),
                  pl.BlockSpec((tm, LANES), lambda i: (i, 0))],
        out_specs=[pl.BlockSpec((tm, w), lambda i: (i, 0)),
                   pl.BlockSpec((tm, w), lambda i: (i, 0)),
                   pl.BlockSpec((DIFF_HEADS, v_rows, tm), lambda i: (i // nb, 0, i % nb))],
        out_shape=[jax.ShapeDtypeStruct((m, w), BF16),
                   jax.ShapeDtypeStruct((m, w), BF16),
                   jax.ShapeDtypeStruct((bsz * DIFF_HEADS, v_rows, seq), BF16)],
        compiler_params=_cparams("parallel"),
        name="rope",
    )(proj, proj, proj, cos_t, sin_t)


def _diff_attn_kernel(q_ref, k_ref, vt_ref, g_ref, lam_ref, sg_ref, o_ref, acc1, acc2, s_a, s_b, *, tq):
    qi = pl.program_id(2)
    q = q_ref[...]
    lane = lax.broadcasted_iota(jnp.int32, q.shape, 1)
    q_maps = (jnp.where(lane < DIFF_HEAD_DIM, q, jnp.zeros_like(q)),
              jnp.where(lane >= DIFF_HEAD_DIM, q, jnp.zeros_like(q)))
    accs = (acc1, acc2)
    acc1[...] = jnp.zeros_like(acc1)
    acc2[...] = jnp.zeros_like(acc2)

    def scores_into(s_ref, j):
        k = k_ref[pl.ds(pl.multiple_of(j * tq, tq), tq), :]
        for mi, qm in enumerate(q_maps):
            s_ref[mi] = lax.dot_general(k, qm, (((1,), (1,)), ((), ())),
                                        preferred_element_type=F32)

    def consume(s_ref, j, ms, masked):
        vt = vt_ref[0, :, pl.ds(pl.multiple_of(j * tq, tq), tq)]
        out = []
        for mi, (m_old, acc) in enumerate(zip(ms, accs)):
            s = s_ref[mi]
            if masked:
                key = lax.broadcasted_iota(jnp.int32, s.shape, 0)
                qry = lax.broadcasted_iota(jnp.int32, s.shape, 1)
                s = jnp.where(key <= qry, s, MASK_VALUE)
            m_new = jnp.maximum(m_old, jnp.max(s, axis=0, keepdims=True))
            p = jnp.exp2(s - m_new)
            acc[...] = acc[...] * jnp.exp2(m_old - m_new) + jnp.dot(vt, p.astype(BF16),
                                                                    preferred_element_type=F32)
            out.append(m_new)
        return tuple(out)

    def finish():
        d = 2 * DIFF_HEAD_DIM
        a1, a2 = acc1[...], acc2[...]
        o_t = a1[:d] / a1[d:d + 1] - lam_ref[0] * (a2[:d] / a2[d:d + 1])
        o = o_t.T
        var = jnp.mean(o * o, axis=-1, keepdims=True)
        o = o * lax.rsqrt(var + NORM_EPS) * sg_ref[...]
        o_ref[...] = (o * jax.nn.silu(g_ref[...].astype(F32))).astype(o_ref.dtype)

    scores_into(s_a, 0)

    def two_blocks(i, ms):
        scores_into(s_b, 2 * i + 1)
        ms = consume(s_a, 2 * i, ms, False)
        scores_into(s_a, 2 * i + 2)
        return consume(s_b, 2 * i + 1, ms, False)

    m0 = jnp.full((1, tq), MASK_VALUE, F32)
    ms = lax.fori_loop(0, qi // 2, two_blocks, (m0, m0))

    @pl.when(qi % 2 == 0)
    def _():
        consume(s_a, qi, ms, True)
        finish()

    @pl.when(qi % 2 == 1)
    def _():
        scores_into(s_b, qi)
        consume(s_b, qi, consume(s_a, qi - 1, ms, False), True)
        finish()


def _diff_attn(q_rot, k_rot, vt_ext, proj, lam, sg_vec, *, bsz, seq, g_col, tq):
    tq = min(tq, seq)
    nq = seq // tq
    v_rows = vt_ext.shape[1]
    return pl.pallas_call(
        functools.partial(_diff_attn_kernel, tq=tq),
        grid=(bsz, DIFF_HEADS, nq),
        in_specs=[
            pl.BlockSpec((tq, LANES), lambda b, h, i: (b * nq + i, h)),
            pl.BlockSpec((seq, LANES), lambda b, h, i: (b, h)),
            pl.BlockSpec((1, v_rows, seq), lambda b, h, i: (b * DIFF_HEADS + h, 0, 0)),
            pl.BlockSpec((tq, LANES), lambda b, h, i: (b * nq + i, g_col + h)),
            pl.BlockSpec(memory_space=pltpu.SMEM),
            pl.BlockSpec((1, LANES), lambda b, h, i: (0, 0)),
        ],
        out_specs=pl.BlockSpec((tq, LANES), lambda b, h, i: (b * nq + i, h)),
        out_shape=jax.ShapeDtypeStruct((bsz * seq, DIFF_HEADS * LANES), BF16),
        scratch_shapes=[pltpu.VMEM((v_rows, tq), F32), pltpu.VMEM((v_rows, tq), F32),
                        pltpu.VMEM((2, tq, tq), F32), pltpu.VMEM((2, tq, tq), F32)],
        compiler_params=_cparams("parallel", "parallel", "arbitrary"),
        name="diff_attn",
    )(q_rot, k_rot, vt_ext, proj, lam, sg_vec)


def _mem_attn_kernel(q_ref, g_ref, mk_ref, mv_ref, o_ref):
    scale = XATTN_HEAD_DIM ** -0.5
    for h in range(XATTN_HEADS):
        sl = slice(h * XATTN_HEAD_DIM, (h + 1) * XATTN_HEAD_DIM)
        s = lax.dot_general(q_ref[:, sl], mk_ref[:, sl], (((1,), (1,)), ((), ())),
                            preferred_element_type=F32) * scale
        p = jnp.exp(s - jnp.max(s, axis=-1, keepdims=True))
        p = p / jnp.sum(p, axis=-1, keepdims=True)
        o = jnp.dot(p.astype(BF16), mv_ref[:, sl], preferred_element_type=F32)
        o_ref[:, sl] = (o * jax.nn.silu(g_ref[:, sl].astype(F32))).astype(o_ref.dtype)


def _mem_attn(proj, mem_kv, *, bsz, seq, mem_tokens, q_col_block, g_col_block, tq):
    tq = min(tq, seq)
    nq = seq // tq
    w = XATTN_HEADS * XATTN_HEAD_DIM
    return pl.pallas_call(
        _mem_attn_kernel,
        grid=(bsz, nq),
        in_specs=[pl.BlockSpec((tq, w), lambda b, i: (b * nq + i, q_col_block)),
                  pl.BlockSpec((tq, w), lambda b, i: (b * nq + i, g_col_block)),
                  pl.BlockSpec((mem_tokens, w), lambda b, i: (b, 0)),
                  pl.BlockSpec((mem_tokens, w), lambda b, i: (b, 1))],
        out_specs=pl.BlockSpec((tq, w), lambda b, i: (b * nq + i, 0)),
        out_shape=jax.ShapeDtypeStruct((bsz * seq, w), BF16),
        compiler_params=_cparams("parallel", "parallel"),
        name="mem_attn",
    )(proj, proj, mem_kv, mem_kv)


def _s5_taps_kernel(bt_ref, ca_ref, d_ref, o_ref):
    for s in range(bt_ref.shape[0]):
        k = lax.dot_general(bt_ref[s], ca_ref[s], (((1,), (1,)), ((), ())),
                            preferred_element_type=F32, precision=lax.Precision.HIGHEST)
        row = lax.broadcasted_iota(jnp.int32, k.shape, 0)
        col = lax.broadcasted_iota(jnp.int32, k.shape, 1)
        o_ref[s] = k + jnp.where(row == col, d_ref[s], 0.0)


def _s5_taps(bt, ca, d_pad):
    g, h, k = bt.shape
    n = ca.shape[1]
    gb = GROUPS_PER_TILE
    return pl.pallas_call(
        _s5_taps_kernel,
        grid=(g // gb,),
        in_specs=[pl.BlockSpec((gb, h, k), lambda i: (i, 0, 0)),
                  pl.BlockSpec((gb, n, k), lambda i: (i, 0, 0)),
                  pl.BlockSpec((gb, 1, n), lambda i: (i, 0, 0))],
        out_specs=pl.BlockSpec((gb, h, n), lambda i: (i, 0, 0)),
        out_shape=jax.ShapeDtypeStruct((g, h, n), F32),
        compiler_params=_cparams("parallel"),
        name="s5_taps",
    )(bt, ca, d_pad)


def _s5_weights(a_re, a_im, log_dt, b_re, b_im, c_re, c_im, d_skip):
    g = a_re.shape[0]
    p, h, t = SSM_STATE, SSM_GROUP, CHUNK
    dt = jnp.exp(log_dt.astype(F32))[:, None]
    lr, li = a_re.astype(F32), a_im.astype(F32)
    mag = jnp.exp(lr * dt)
    abar_re, abar_im = mag * jnp.cos(li * dt), mag * jnp.sin(li * dt)
    den = lr * lr + li * li
    nr, ni = abar_re - 1.0, abar_im
    z_re = ((nr * lr + ni * li) / den)[:, None, :]
    z_im = ((ni * lr - nr * li) / den)[:, None, :]
    br, bi = b_re.astype(F32).transpose(0, 2, 1), b_im.astype(F32).transpose(0, 2, 1)
    bt_re = z_re * br - z_im * bi
    bt_im = z_re * bi + z_im * br
    cr, ci = c_re.astype(F32), c_im.astype(F32)

    tau = jnp.arange(t + 1, dtype=F32)[None, :, None]
    pmag = jnp.exp(tau * (lr * dt)[:, None, :])
    pw_re = pmag * jnp.cos(tau * (li * dt)[:, None, :])
    pw_im = pmag * jnp.sin(tau * (li * dt)[:, None, :])

    ca_re = cr[:, None] * pw_re[:, :, None, :] - ci[:, None] * pw_im[:, :, None, :]
    ca_im = cr[:, None] * pw_im[:, :, None, :] + ci[:, None] * pw_re[:, :, None, :]

    ca = jnp.concatenate([ca_re[:, :t], ca_im[:, :t]], axis=-1).reshape(g, t * h, 2 * p)
    bt = jnp.concatenate([bt_re, -bt_im], axis=-1)
    d_pad = jnp.pad(d_skip.astype(F32), ((0, 0), (0, (t - 1) * h)))[:, None, :]
    taps_t = _s5_taps(bt, ca, d_pad)

    gt = GROUPS_PER_TILE
    nt = g // gt
    idx = jnp.arange(t)

    k_cat = taps_t.reshape(nt, gt * h, t * h)

    pj_re, pj_im = pw_re[:, t - 1 - idx][:, :, None, :], pw_im[:, t - 1 - idx][:, :, None, :]
    win = jnp.concatenate([pj_re * bt_re[:, None] - pj_im * bt_im[:, None],
                           pj_re * bt_im[:, None] + pj_im * bt_re[:, None]], axis=-1)
    win_cat = win.astype(BF16).reshape(nt, gt, t, h, 2 * p).transpose(0, 2, 1, 3, 4)
    win_cat = win_cat.reshape(nt, t * gt * h, 2 * p)

    wo = jnp.stack([ca_re[:, 1:].reshape(g, t * h, p), -ca_im[:, 1:].reshape(g, t * h, p)], axis=1)
    wo_cat = wo.astype(BF16).transpose(0, 1, 3, 2).reshape(nt, gt, 2, p, t * h).transpose(0, 2, 1, 3, 4)
    wo_cat = wo_cat.reshape(nt, 2 * gt * p, t * h)

    kk = (t * jnp.arange(2 * SUBLANES, dtype=F32))[:, None, None]
    cmag = jnp.exp(kk * (lr * dt)[None])
    a_chunk_re = (cmag * jnp.cos(kk * (li * dt)[None])).reshape(2 * SUBLANES, g * p)
    a_chunk_im = (cmag * jnp.sin(kk * (li * dt)[None])).reshape(2 * SUBLANES, g * p)
    return k_cat.astype(BF16), win_cat, wo_cat, a_chunk_re, a_chunk_im


def _lane_repeat(blocks, width, copies):
    r = np.kron(np.eye(blocks, dtype=np.float32),
                np.kron(np.ones((1, copies), np.float32), np.eye(width, dtype=np.float32)))
    return jnp.asarray(r, BF16)


def _group_of(index, period, size):
    assert period & (period - 1) == 0 and size & (size - 1) == 0
    return lax.shift_right_logical(index & (period - 1), size.bit_length() - 1)


def _expand_block_diag(compact, rep, row_group, col_group):
    e = jnp.dot(compact, rep, preferred_element_type=F32)
    r = lax.broadcasted_iota(jnp.int32, e.shape, 0)
    c = lax.broadcasted_iota(jnp.int32, e.shape, 1)
    return jnp.where(row_group(r) == col_group(c), e, 0.0).astype(BF16)


def _chunk_lhs(u_ref):
    return jnp.concatenate([u_ref[j] for j in range(CHUNK)], axis=1)


def _s5_state_in_kernel(u_ref, wc_ref, rep_ref, sre_ref, sim_ref, w_scr):
    @pl.when(pl.program_id(1) == 0)
    def _():
        w_scr[...] = _expand_block_diag(
            wc_ref[0], rep_ref[...],
            lambda r: _group_of(r, LANES, SSM_GROUP),
            lambda c: _group_of(c, GROUPS_PER_TILE * SSM_STATE, SSM_STATE))

    s = jnp.dot(_chunk_lhs(u_ref), w_scr[...], preferred_element_type=F32)
    half = s.shape[1] // 2
    sre_ref[...] = s[:, :half]
    sim_ref[...] = s[:, half:]


def _s5_state_in(us3, win_cat, layer, *, rb):
    t, r, width = us3.shape
    nt = width // (2 * LANES)
    _, k, nc = win_cat.shape
    n = nc * GROUPS_PER_TILE
    rb = min(rb, r)
    rep = _lane_repeat(2, SSM_STATE, GROUPS_PER_TILE)
    return pl.pallas_call(
        _s5_state_in_kernel,
        grid=(nt, r // rb),
        in_specs=[pl.BlockSpec((t, rb, LANES), lambda q, i: (0, i, q)),
                  pl.BlockSpec((1, k, nc), lambda q, i: (layer * nt + q, 0, 0)),
                  pl.BlockSpec(rep.shape, lambda q, i: (0, 0))],
        out_specs=[pl.BlockSpec((rb, n // 2), lambda q, i: (i, q)),
                   pl.BlockSpec((rb, n // 2), lambda q, i: (i, q))],
        out_shape=[jax.ShapeDtypeStruct((r, nt * n // 2), F32)] * 2,
        scratch_shapes=[pltpu.VMEM((k, n), BF16)],
        compiler_params=_cparams("parallel", "arbitrary"),
        name="s5_state_in",
    )(us3, win_cat, rep)


def _s5_recurrence_kernel(sre_ref, sim_ref, pre_ref, pim_ref, hre_ref, him_ref, *, bsz, n_chunks):
    tn = sre_ref.shape[1]
    shape = (SUBLANES, tn)
    row = lax.broadcasted_iota(jnp.int32, shape, 0)
    pw_re = pre_ref[0:SUBLANES, :]
    pw_im = pim_ref[0:SUBLANES, :]
    a_tile_re = jnp.broadcast_to(pre_ref[SUBLANES:SUBLANES + 1, :], shape)
    a_tile_im = jnp.broadcast_to(pim_ref[SUBLANES:SUBLANES + 1, :], shape)

    def tile_step(it, carry):
        out = []
        for b in range(bsz):
            h_re, h_im = carry[2 * b], carry[2 * b + 1]
            rows = pl.ds(pl.multiple_of(b * n_chunks + it * SUBLANES, SUBLANES), SUBLANES)
            y_re, y_im = sre_ref[rows, :], sim_ref[rows, :]
            for d in (1, 2, 4):
                m_re = jnp.broadcast_to(pre_ref[d:d + 1, :], shape)
                m_im = jnp.broadcast_to(pim_ref[d:d + 1, :], shape)
                s_re = jnp.where(row >= d, pltpu.roll(y_re, d, axis=0), 0.0)
                s_im = jnp.where(row >= d, pltpu.roll(y_im, d, axis=0), 0.0)
                y_re, y_im = (y_re + m_re * s_re - m_im * s_im,
                              y_im + m_re * s_im + m_im * s_re)
            e_re = jnp.where(row >= 1, pltpu.roll(y_re, 1, axis=0), 0.0)
            e_im = jnp.where(row >= 1, pltpu.roll(y_im, 1, axis=0), 0.0)
            hre_ref[rows, :] = pw_re * h_re - pw_im * h_im + e_re
            him_ref[rows, :] = pw_re * h_im + pw_im * h_re + e_im
            last_re = jnp.broadcast_to(y_re[SUBLANES - 1:SUBLANES, :], shape)
            last_im = jnp.broadcast_to(y_im[SUBLANES - 1:SUBLANES, :], shape)
            out.append(a_tile_re * h_re - a_tile_im * h_im + last_re)
            out.append(a_tile_re * h_im + a_tile_im * h_re + last_im)
        return tuple(out)

    zero = jnp.zeros(shape, F32)
    lax.fori_loop(0, n_chunks // SUBLANES, tile_step, (zero,) * (2 * bsz))


def _s5_recurrence(s_re, s_im, pw_re, pw_im, layer, *, bsz, tn):
    r, n = s_re.shape
    tn = min(tn, n)
    spec = pl.BlockSpec((r, tn), lambda j: (0, j))
    pspec = pl.BlockSpec((pw_re.shape[0], tn), lambda j: (0, layer * (n // tn) + j))
    return pl.pallas_call(
        functools.partial(_s5_recurrence_kernel, bsz=bsz, n_chunks=r // bsz),
        grid=(n // tn,),
        in_specs=[spec, spec, pspec, pspec],
        out_specs=[spec, spec],
        out_shape=[jax.ShapeDtypeStruct((r, n), F32)] * 2,
        compiler_params=_cparams("parallel"),
        name="s5_recurrence",
    )(s_re, s_im, pw_re, pw_im)


def _s5_out_kernel(u_ref, kc_ref, hre_ref, him_ref, wc_ref, rep_ref, y_ref, t_scr, w_scr):
    @pl.when(pl.program_id(1) == 0)
    def _():
        rep = rep_ref[...]
        col_group = lambda c: _group_of(c, LANES, SSM_GROUP)
        taps = _expand_block_diag(kc_ref[0], rep, lambda r: _group_of(r, LANES, SSM_GROUP), col_group)
        t_scr[...] = jnp.zeros_like(t_scr)
        for j in range(CHUNK):
            t_scr[j * LANES:(j + 1) * LANES, j * LANES:] = taps[:, :(CHUNK - j) * LANES]
        w_scr[...] = _expand_block_diag(
            wc_ref[0], rep, lambda r: _group_of(r, GROUPS_PER_TILE * SSM_STATE, SSM_STATE), col_group)

    h = jnp.concatenate([hre_ref[...], him_ref[...]], axis=1).astype(BF16)
    y = jnp.dot(_chunk_lhs(u_ref), t_scr[...], preferred_element_type=F32)
    y += jnp.dot(h, w_scr[...], preferred_element_type=F32)
    for i in range(CHUNK):
        y_ref[i] = y[:, i * LANES:(i + 1) * LANES].astype(y_ref.dtype)


def _s5_out(us3, k_cat, h_re, h_im, wo_cat, layer, *, rb):
    t, r, width = us3.shape
    nt = width // (2 * LANES)
    _, ks, kc = wo_cat.shape
    k = t * LANES
    rb = min(rb, r)
    rep = _lane_repeat(t, SSM_GROUP, GROUPS_PER_TILE)
    return pl.pallas_call(
        _s5_out_kernel,
        grid=(nt, r // rb),
        in_specs=[pl.BlockSpec((t, rb, LANES), lambda q, i: (0, i, q)),
                  pl.BlockSpec((1, LANES, kc), lambda q, i: (layer * nt + q, 0, 0)),
                  pl.BlockSpec((rb, ks // 2), lambda q, i: (i, q)),
                  pl.BlockSpec((rb, ks // 2), lambda q, i: (i, q)),
                  pl.BlockSpec((1, ks, kc), lambda q, i: (layer * nt + q, 0, 0)),
                  pl.BlockSpec(rep.shape, lambda q, i: (0, 0))],
        out_specs=pl.BlockSpec((t, rb, LANES), lambda q, i: (0, i, q)),
        out_shape=jax.ShapeDtypeStruct((t, r, nt * LANES), BF16),
        scratch_shapes=[pltpu.VMEM((k, k), BF16), pltpu.VMEM((ks, k), BF16)],
        compiler_params=_cparams("parallel", "arbitrary"),
        name="s5_out",
    )(us3, k_cat, h_re, h_im, wo_cat, rep)


def _s5_glu_kernel(y_ref, gate_ref, w_ref, b_ref, pt_ref, o_ref):
    t, rb, w = y_ref.shape
    y = jax.nn.gelu(y_ref[...].reshape(t * rb, w).astype(F32))
    z = jnp.dot(y.astype(BF16), w_ref[...], preferred_element_type=F32) + b_ref[...]
    y = y * jax.nn.sigmoid(z)
    y = (y * jax.nn.silu(gate_ref[...].reshape(t * rb, w).astype(F32))).astype(BF16)
    o_ref[...] = jnp.dot(pt_ref[...], y, preferred_element_type=F32).astype(o_ref.dtype)


def _s5_glu(y3, us3, w_glu, b_glu, *, rb):
    t, r, w = y3.shape
    rb = min(rb, r)
    rows = t * rb
    return pl.pallas_call(
        _s5_glu_kernel,
        grid=(r // rb,),
        in_specs=[pl.BlockSpec((t, rb, w), lambda i: (0, i, 0)),
                  pl.BlockSpec((t, rb, w), lambda i: (0, i, 1)),
                  pl.BlockSpec((w, w), lambda i: (0, 0)),
                  pl.BlockSpec((1, w), lambda i: (0, 0)),
                  pl.BlockSpec((rows, rows), lambda i: (0, 0))],
        out_specs=pl.BlockSpec((rows, w), lambda i: (i, 0)),
        out_shape=jax.ShapeDtypeStruct((t * r, w), BF16),
        compiler_params=_cparams("parallel"),
        name="s5_glu",
    )(y3, us3, w_glu, b_glu.reshape(1, w), _chunk_major_perm(rows).T)


def _out_proj_kernel(ys_ref, yd_ref, yx_ref, w_ref, x_ref, g_ref, o_ref):
    ws, wd = ys_ref.shape[1], yd_ref.shape[1]
    mix = jnp.dot(ys_ref[...], w_ref[0:ws, :], preferred_element_type=F32)
    mix += jnp.dot(yd_ref[...], w_ref[ws:ws + wd, :], preferred_element_type=F32)
    mix += jnp.dot(yx_ref[...], w_ref[ws + wd:, :], preferred_element_type=F32)
    var = jnp.mean(mix * mix, axis=-1, keepdims=True)
    o_ref[...] = x_ref[...] + mix * lax.rsqrt(var + NORM_EPS) * g_ref[...]


def _out_proj(y_s, y_d, y_x, w_out, layer, x, g_post, *, tm):
    m, d = x.shape
    tm = min(tm, m)
    row = lambda i: (i, 0)
    const = lambda i: (0, 0)
    return pl.pallas_call(
        _out_proj_kernel,
        grid=(m // tm,),
        in_specs=[pl.BlockSpec((tm, y_s.shape[1]), row),
                  pl.BlockSpec((tm, y_d.shape[1]), row),
                  pl.BlockSpec((tm, y_x.shape[1]), row),
                  pl.BlockSpec((None,) + w_out.shape[1:], lambda i: (layer, 0, 0)),
                  pl.BlockSpec((tm, d), row),
                  pl.BlockSpec((1, d), const)],
        out_specs=pl.BlockSpec((tm, d), row),
        out_shape=jax.ShapeDtypeStruct((m, d), F32),
        compiler_params=_cparams("parallel"),
        name="out_proj",
    )(y_s, y_d, y_x, w_out, x, g_post.reshape(1, d))


def kernel(x, mem, positions, norm_pre, norm_post, norm_mem, w_in, w_out, w_mem_kv, ssm_a_re, ssm_a_im, ssm_log_dt, ssm_b_re, ssm_b_im, ssm_c_re, ssm_c_im, ssm_d, w_glu, b_glu, diff_lq1, diff_lk1, diff_lq2, diff_lk2, diff_subln):
    bsz, seq, d_model = x.shape
    mem_tokens = mem.shape[1]
    depth = w_in.shape[0]
    m = bsz * seq
    ssm_width = ssm_a_re.shape[1] * SSM_GROUP
    n_groups = ssm_width // SSM_GROUP
    diff_width = DIFF_HEADS * 2 * DIFF_HEAD_DIM
    xattn_width = XATTN_HEADS * XATTN_HEAD_DIM
    n_chunks = seq // CHUNK
    n_s5_cols = 2 * ssm_width
    col_qd = 0
    col_kd = col_qd + diff_width
    col_vd = col_kd + diff_width
    col_gd = col_vd + diff_width
    col_qx = col_gd + diff_width
    col_gx = col_qx + xattn_width
    assert col_kd == col_qd + diff_width and col_qd % diff_width == 0

    inv = ROPE_THETA ** (-jnp.arange(0, DIFF_HEAD_DIM, 2, dtype=F32) / DIFF_HEAD_DIM)
    ang = positions.astype(F32).reshape(m, 1) * inv
    cos, sin = jnp.cos(ang), jnp.sin(ang)
    cos_t = jnp.concatenate([cos, cos, cos, cos], axis=-1)
    sin_t = jnp.concatenate([-sin, sin, -sin, sin], axis=-1)

    xf = x.reshape(m, d_model)
    mem_f = mem.reshape(bsz * mem_tokens, d_model)
    w_in_bf, w_out_bf = w_in.astype(BF16), w_out.astype(BF16)
    all_groups = lambda a: a.reshape((depth * n_groups,) + a.shape[2:])
    k_cat, win_cat, wo_cat, a_re, a_im = _s5_weights(
        *(all_groups(a) for a in (ssm_a_re, ssm_a_im, ssm_log_dt, ssm_b_re, ssm_b_im,
                                  ssm_c_re, ssm_c_im, ssm_d)))
    for l in range(depth):
        lambda_init = 0.8 - 0.6 * math.exp(-0.3 * l)
        us3, proj = _in_proj(xf, norm_pre[l], w_in_bf, l, n_perm_cols=n_s5_cols, tm=1024, tn=512,
                             perm_rows=512)

        s_re, s_im = _s5_state_in(us3, win_cat, l, rb=512)
        h_re, h_im = _s5_recurrence(s_re, s_im, a_re, a_im, l, bsz=bsz, tn=1024)
        y3 = _s5_out(us3, k_cat, h_re, h_im, wo_cat, l, rb=512)
        y_s = _s5_glu(y3, us3, w_glu[l].astype(BF16), b_glu[l], rb=64)

        assert col_vd == col_kd + diff_width
        q_rot, k_rot, vt_ext = _rope(proj, cos_t, sin_t, bsz=bsz, seq=seq,
                                     q_col_block=col_qd // diff_width, tm=1024)
        lam = (jnp.exp(jnp.sum(diff_lq1[l].astype(F32) * diff_lk1[l].astype(F32)))
               - jnp.exp(jnp.sum(diff_lq2[l].astype(F32) * diff_lk2[l].astype(F32))) + lambda_init)
        sg_vec = (diff_subln[l].astype(F32) * (1.0 - lambda_init)).reshape(1, LANES)
        y_d = _diff_attn(q_rot, k_rot, vt_ext, proj, lam.reshape(1), sg_vec, bsz=bsz, seq=seq,
                         g_col=col_gd // LANES, tq=512)

        mem_kv = _norm_matmul(mem_f, norm_mem[l], w_mem_kv[l].astype(BF16), tm=512, tn=1024, name="mem_kv")
        y_x = _mem_attn(proj, mem_kv, bsz=bsz, seq=seq, mem_tokens=mem_tokens,
                        q_col_block=col_qx // xattn_width, g_col_block=col_gx // xattn_width, tq=1024)

        xf = _out_proj(y_s, y_d, y_x, w_out_bf, l, xf, norm_post[l], tm=512)
    return xf.reshape(bsz, seq, d_model)
```

```python
import functools
import math

import jax
import jax.numpy as jnp
import numpy as np
from jax import lax
from jax.experimental import pallas as pl
from jax.experimental.pallas import tpu as pltpu

F32 = jnp.float32
BF16 = jnp.bfloat16

SSM_GROUP = 16
SSM_STATE = 64
CHUNK = 16
DIFF_HEADS = 4
DIFF_HEAD_DIM = 64
XATTN_HEADS = 4
XATTN_HEAD_DIM = 128
ROPE_THETA = 10000.0
NORM_EPS = 1e-6
MASK_VALUE = -1e30
LANES = 128
SUBLANES = 8
BF16_SUBLANES = 16
GROUPS_PER_TILE = LANES // SSM_GROUP
S5_OUT_BANDS = 4
VMEM_LIMIT = 56 * 1024 * 1024


def _cparams(*sem):
    return pltpu.CompilerParams(dimension_semantics=sem, vmem_limit_bytes=VMEM_LIMIT)


def _norm_matmul_kernel(x_ref, g_ref, w_ref, o_ref, h_ref):
    @pl.when(pl.program_id(1) == 0)
    def _():
        x = x_ref[...]
        var = jnp.mean(x * x, axis=-1, keepdims=True)
        h_ref[...] = (x * lax.rsqrt(var + NORM_EPS) * g_ref[...]).astype(BF16)

    o_ref[...] = jnp.dot(h_ref[...], w_ref[...], preferred_element_type=F32).astype(o_ref.dtype)


def _norm_matmul(x, g, w, *, tm, tn, name):
    m, d = x.shape
    n = w.shape[1]
    tm, tn = min(tm, m), min(tn, n)
    return pl.pallas_call(
        _norm_matmul_kernel,
        grid=(m // tm, n // tn),
        in_specs=[pl.BlockSpec((tm, d), lambda i, j: (i, 0)),
                  pl.BlockSpec((1, d), lambda i, j: (0, 0)),
                  pl.BlockSpec((d, tn), lambda i, j: (0, j))],
        out_specs=pl.BlockSpec((tm, tn), lambda i, j: (i, j)),
        out_shape=jax.ShapeDtypeStruct((m, n), BF16),
        scratch_shapes=[pltpu.VMEM((tm, d), BF16)],
        compiler_params=_cparams("parallel", "arbitrary"),
        name=name,
    )(x, g.reshape(1, d), w)


def _chunk_major_perm(rows):
    rb = rows // CHUNK
    src = (np.arange(rows) % rb) * CHUNK + np.arange(rows) // rb
    return jnp.asarray(np.eye(rows, dtype=np.float32)[src], BF16)


def _in_proj_kernel(x_ref, g_ref, w_ref, p_ref, o3_ref, o_ref, hp_ref, hn_ref, *, n_perm):
    j = pl.program_id(1)
    tm = x_ref.shape[0]
    pr = p_ref.shape[0]
    rb = pr // CHUNK

    @pl.when(j == 0)
    def _():
        for s in range(tm // pr):
            x = x_ref[s * pr:(s + 1) * pr, :]
            var = jnp.mean(x * x, axis=-1, keepdims=True)
            hn = (x * lax.rsqrt(var + NORM_EPS) * g_ref[...]).astype(BF16)
            hn_ref[s * pr:(s + 1) * pr, :] = hn
            hp_ref[s * pr:(s + 1) * pr, :] = jnp.dot(p_ref[...], hn, preferred_element_type=F32).astype(BF16)

    @pl.when(j < n_perm)
    def _():
        res = jnp.dot(hp_ref[...], w_ref[...], preferred_element_type=F32)
        for s in range(tm // pr):
            for i in range(CHUNK):
                rows = slice(s * pr + i * rb, s * pr + (i + 1) * rb)
                o3_ref[i, s * rb:(s + 1) * rb, :] = res[rows].astype(o3_ref.dtype)

    @pl.when(j >= n_perm)
    def _():
        o_ref[...] = jnp.dot(hn_ref[...], w_ref[...], preferred_element_type=F32).astype(o_ref.dtype)


def _in_proj(x, g, w, layer, *, n_perm_cols, tm, tn, perm_rows):
    m, d = x.shape
    n = w.shape[2]
    tm = min(tm, m)
    perm_rows = min(perm_rows, tm)
    n_perm = n_perm_cols // tn
    return pl.pallas_call(
        functools.partial(_in_proj_kernel, n_perm=n_perm),
        grid=(m // tm, n // tn),
        in_specs=[pl.BlockSpec((tm, d), lambda i, j: (i, 0)),
                  pl.BlockSpec((1, d), lambda i, j: (0, 0)),
                  pl.BlockSpec((None, d, tn), lambda i, j: (layer, 0, j)),
                  pl.BlockSpec((perm_rows, perm_rows), lambda i, j: (0, 0))],
        out_specs=[pl.BlockSpec((CHUNK, tm // CHUNK, tn), lambda i, j: (0, i, jnp.minimum(j, n_perm - 1))),
                   pl.BlockSpec((tm, tn), lambda i, j: (i, jnp.maximum(j - n_perm, 0)))],
        out_shape=[jax.ShapeDtypeStruct((CHUNK, m // CHUNK, n_perm_cols), BF16),
                   jax.ShapeDtypeStruct((m, n - n_perm_cols), BF16)],
        scratch_shapes=[pltpu.VMEM((tm, d), BF16), pltpu.VMEM((tm, d), BF16)],
        compiler_params=_cparams("parallel", "arbitrary"),
        name="in_proj",
    )(x, g.reshape(1, d), w, _chunk_major_perm(perm_rows))


def _rope_kernel(q_ref, k_ref, v_ref, cos_ref, sin_ref, qo_ref, ko_ref, vt_ref, *, q_scale):
    cos = cos_ref[...]
    sin = sin_ref[...]
    lane = lax.broadcasted_iota(jnp.int32, cos.shape, 1)
    first_half = (lane % DIFF_HEAD_DIM) < (DIFF_HEAD_DIM // 2)
    for x_ref, o_ref, mult in ((q_ref, qo_ref, q_scale), (k_ref, ko_ref, 1.0)):
        for h in range(DIFF_HEADS):
            x = x_ref[:, h * LANES:(h + 1) * LANES].astype(F32)
            partner = jnp.where(first_half,
                                pltpu.roll(x, LANES - DIFF_HEAD_DIM // 2, axis=1),
                                pltpu.roll(x, DIFF_HEAD_DIM // 2, axis=1))
            o_ref[:, h * LANES:(h + 1) * LANES] = ((x * cos + partner * sin) * mult).astype(o_ref.dtype)
    d = 2 * DIFF_HEAD_DIM
    for h in range(DIFF_HEADS):
        vt_ref[h, 0:d, :] = v_ref[:, h * d:(h + 1) * d].astype(F32).T.astype(vt_ref.dtype)
        vt_ref[h, d:, :] = jnp.ones((vt_ref.shape[1] - d, vt_ref.shape[2]), vt_ref.dtype)


def _rope(proj, cos_t, sin_t, *, bsz, seq, q_col_block, tm):
    m = proj.shape[0]
    tm = min(tm, seq)
    nb = seq // tm
    w = DIFF_HEADS * LANES
    v_rows = 2 * DIFF_HEAD_DIM + BF16_SUBLANES
    q_scale = DIFF_HEAD_DIM ** -0.5 * math.log2(math.e)
    return pl.pallas_call(
        functools.partial(_rope_kernel, q_scale=q_scale),
        grid=(m // tm,),
        in_specs=[pl.BlockSpec((tm, w), lambda i: (i, q_col_block)),
                  pl.BlockSpec((tm, w), lambda i: (i, q_col_block + 1)),
                  pl.BlockSpec((tm, w), lambda i: (i, q_col_block + 2)),
                  pl.BlockSpec((tm, LANES), lambda i: (i, 0)),
                  pl.BlockSpec((tm, LANES), lambda i: (i, 0))],
        out_specs=[pl.BlockSpec((tm, w), lambda i: (i, 0)),
                   pl.BlockSpec((tm, w), lambda i: (i, 0)),
                   pl.BlockSpec((DIFF_HEADS, v_rows, tm), lambda i: (i // nb, 0, i % nb))],
        out_shape=[jax.ShapeDtypeStruct((m, w), BF16),
                   jax.ShapeDtypeStruct((m, w), BF16),
                   jax.ShapeDtypeStruct((bsz * DIFF_HEADS, v_rows, seq), BF16)],
        compiler_params=_cparams("parallel"),
        name="rope",
    )(proj, proj, proj, cos_t, sin_t)


def _diff_attn_kernel(q_ref, k_ref, vt_ref, g_ref, lam_ref, sg_ref, o_ref, acc1, acc2, s_a, s_b, *, tk):
    qi = pl.program_id(2)
    tq = q_ref.shape[0]
    diag_blocks = tq // tk
    assert diag_blocks % 2 == 0
    q = q_ref[...]
    lane = lax.broadcasted_iota(jnp.int32, q.shape, 1)
    q_maps = (jnp.where(lane < DIFF_HEAD_DIM, q, jnp.zeros_like(q)),
              jnp.where(lane >= DIFF_HEAD_DIM, q, jnp.zeros_like(q)))
    accs = (acc1, acc2)
    acc1[...] = jnp.zeros_like(acc1)
    acc2[...] = jnp.zeros_like(acc2)

    def scores_into(s_ref, j):
        k = k_ref[pl.ds(pl.multiple_of(j * tk, tk), tk), :]
        for mi, qm in enumerate(q_maps):
            s_ref[mi] = lax.dot_general(k, qm, (((1,), (1,)), ((), ())),
                                        preferred_element_type=F32)

    def consume(s_ref, j, ms, key_offset=None):
        vt = vt_ref[0, :, pl.ds(pl.multiple_of(j * tk, tk), tk)]
        out = []
        for mi, (m_old, acc) in enumerate(zip(ms, accs)):
            s = s_ref[mi]
            if key_offset is not None:
                key = lax.broadcasted_iota(jnp.int32, s.shape, 0) + key_offset
                qry = lax.broadcasted_iota(jnp.int32, s.shape, 1)
                s = jnp.where(key <= qry, s, MASK_VALUE)
            m_new = jnp.maximum(m_old, jnp.max(s, axis=0, keepdims=True))
            p = jnp.exp2(s - m_new)
            acc[...] = acc[...] * jnp.exp2(m_old - m_new) + jnp.dot(vt, p.astype(BF16),
                                                                    preferred_element_type=F32)
            out.append(m_new)
        return tuple(out)

    def finish():
        d = 2 * DIFF_HEAD_DIM
        a1, a2 = acc1[...], acc2[...]
        o_t = a1[:d] / a1[d:d + 1] - lam_ref[0] * (a2[:d] / a2[d:d + 1])
        o = o_t.T
        var = jnp.mean(o * o, axis=-1, keepdims=True)
        o = o * lax.rsqrt(var + NORM_EPS) * sg_ref[...]
        o_ref[...] = (o * jax.nn.silu(g_ref[...].astype(F32))).astype(o_ref.dtype)

    scores_into(s_a, 0)

    def two_blocks(i, ms):
        scores_into(s_b, 2 * i + 1)
        ms = consume(s_a, 2 * i, ms)
        scores_into(s_a, 2 * i + 2)
        return consume(s_b, 2 * i + 1, ms)

    m0 = jnp.full((1, tq), MASK_VALUE, F32)
    first_diag = qi * diag_blocks
    ms = lax.fori_loop(0, first_diag // 2, two_blocks, (m0, m0))
    for d in range(diag_blocks):
        cur, nxt = (s_a, s_b) if d % 2 == 0 else (s_b, s_a)
        if d + 1 < diag_blocks:
            scores_into(nxt, first_diag + d + 1)
        ms = consume(cur, first_diag + d, ms, key_offset=d * tk)
    finish()


def _diff_attn(q_rot, k_rot, vt_ext, proj, lam, sg_vec, *, bsz, seq, g_col, tq):
    tq = min(tq, seq)
    tk = tq // 2
    nq = seq // tq
    v_rows = vt_ext.shape[1]
    return pl.pallas_call(
        functools.partial(_diff_attn_kernel, tk=tk),
        grid=(bsz, DIFF_HEADS, nq),
        in_specs=[
            pl.BlockSpec((tq, LANES), lambda b, h, i: (b * nq + i, h)),
            pl.BlockSpec((seq, LANES), lambda b, h, i: (b, h)),
            pl.BlockSpec((1, v_rows, seq), lambda b, h, i: (b * DIFF_HEADS + h, 0, 0)),
            pl.BlockSpec((tq, LANES), lambda b, h, i: (b * nq + i, g_col + h)),
            pl.BlockSpec(memory_space=pltpu.SMEM),
            pl.BlockSpec((1, LANES), lambda b, h, i: (0, 0)),
        ],
        out_specs=pl.BlockSpec((tq, LANES), lambda b, h, i: (b * nq + i, h)),
        out_shape=jax.ShapeDtypeStruct((bsz * seq, DIFF_HEADS * LANES), BF16),
        scratch_shapes=[pltpu.VMEM((v_rows, tq), F32), pltpu.VMEM((v_rows, tq), F32),
                        pltpu.VMEM((2, tk, tq), F32), pltpu.VMEM((2, tk, tq), F32)],
        compiler_params=_cparams("parallel", "parallel", "arbitrary"),
        name="diff_attn",
    )(q_rot, k_rot, vt_ext, proj, lam, sg_vec)


def _mem_attn_kernel(q_ref, g_ref, mk_ref, mv_ref, o_ref):
    scale = XATTN_HEAD_DIM ** -0.5
    for h in range(XATTN_HEADS):
        sl = slice(h * XATTN_HEAD_DIM, (h + 1) * XATTN_HEAD_DIM)
        s = lax.dot_general(q_ref[:, sl], mk_ref[:, sl], (((1,), (1,)), ((), ())),
                            preferred_element_type=F32) * scale
        p = jnp.exp(s - jnp.max(s, axis=-1, keepdims=True))
        p = p / jnp.sum(p, axis=-1, keepdims=True)
        o = jnp.dot(p.astype(BF16), mv_ref[:, sl], preferred_element_type=F32)
        o_ref[:, sl] = (o * jax.nn.silu(g_ref[:, sl].astype(F32))).astype(o_ref.dtype)


def _mem_attn(proj, mem_kv, *, bsz, seq, mem_tokens, q_col_block, g_col_block, tq):
    tq = min(tq, seq)
    nq = seq // tq
    w = XATTN_HEADS * XATTN_HEAD_DIM
    return pl.pallas_call(
        _mem_attn_kernel,
        grid=(bsz, nq),
        in_specs=[pl.BlockSpec((tq, w), lambda b, i: (b * nq + i, q_col_block)),
                  pl.BlockSpec((tq, w), lambda b, i: (b * nq + i, g_col_block)),
                  pl.BlockSpec((mem_tokens, w), lambda b, i: (b, 0)),
                  pl.BlockSpec((mem_tokens, w), lambda b, i: (b, 1))],
        out_specs=pl.BlockSpec((tq, w), lambda b, i: (b * nq + i, 0)),
        out_shape=jax.ShapeDtypeStruct((bsz * seq, w), BF16),
        compiler_params=_cparams("parallel", "parallel"),
        name="mem_attn",
    )(proj, proj, mem_kv, mem_kv)


def _s5_taps_kernel(bt_ref, ca_ref, d_ref, o_ref):
    for s in range(bt_ref.shape[0]):
        k = lax.dot_general(bt_ref[s], ca_ref[s], (((1,), (1,)), ((), ())),
                            preferred_element_type=F32, precision=lax.Precision.HIGHEST)
        row = lax.broadcasted_iota(jnp.int32, k.shape, 0)
        col = lax.broadcasted_iota(jnp.int32, k.shape, 1)
        o_ref[s] = k + jnp.where(row == col, d_ref[s], 0.0)


def _s5_taps(bt, ca, d_pad):
    g, h, k = bt.shape
    n = ca.shape[1]
    gb = GROUPS_PER_TILE
    return pl.pallas_call(
        _s5_taps_kernel,
        grid=(g // gb,),
        in_specs=[pl.BlockSpec((gb, h, k), lambda i: (i, 0, 0)),
                  pl.BlockSpec((gb, n, k), lambda i: (i, 0, 0)),
                  pl.BlockSpec((gb, 1, n), lambda i: (i, 0, 0))],
        out_specs=pl.BlockSpec((gb, h, n), lambda i: (i, 0, 0)),
        out_shape=jax.ShapeDtypeStruct((g, h, n), F32),
        compiler_params=_cparams("parallel"),
        name="s5_taps",
    )(bt, ca, d_pad)


def _s5_weights(a_re, a_im, log_dt, b_re, b_im, c_re, c_im, d_skip):
    g = a_re.shape[0]
    p, h, t = SSM_STATE, SSM_GROUP, CHUNK
    dt = jnp.exp(log_dt.astype(F32))[:, None]
    lr, li = a_re.astype(F32), a_im.astype(F32)
    mag = jnp.exp(lr * dt)
    abar_re, abar_im = mag * jnp.cos(li * dt), mag * jnp.sin(li * dt)
    den = lr * lr + li * li
    nr, ni = abar_re - 1.0, abar_im
    z_re = ((nr * lr + ni * li) / den)[:, None, :]
    z_im = ((ni * lr - nr * li) / den)[:, None, :]
    br, bi = b_re.astype(F32).transpose(0, 2, 1), b_im.astype(F32).transpose(0, 2, 1)
    bt_re = z_re * br - z_im * bi
    bt_im = z_re * bi + z_im * br
    cr, ci = c_re.astype(F32), c_im.astype(F32)

    tau = jnp.arange(t + 1, dtype=F32)[None, :, None]
    pmag = jnp.exp(tau * (lr * dt)[:, None, :])
    pw_re = pmag * jnp.cos(tau * (li * dt)[:, None, :])
    pw_im = pmag * jnp.sin(tau * (li * dt)[:, None, :])

    ca_re = cr[:, None] * pw_re[:, :, None, :] - ci[:, None] * pw_im[:, :, None, :]
    ca_im = cr[:, None] * pw_im[:, :, None, :] + ci[:, None] * pw_re[:, :, None, :]

    ca = jnp.concatenate([ca_re[:, :t], ca_im[:, :t]], axis=-1).reshape(g, t * h, 2 * p)
    bt = jnp.concatenate([bt_re, -bt_im], axis=-1)
    d_pad = jnp.pad(d_skip.astype(F32), ((0, 0), (0, (t - 1) * h)))[:, None, :]
    taps_t = _s5_taps(bt, ca, d_pad)

    gt = GROUPS_PER_TILE
    nt = g // gt
    idx = jnp.arange(t)

    k_cat = taps_t.reshape(nt, gt * h, t * h)

    pj_re, pj_im = pw_re[:, t - 1 - idx][:, :, None, :], pw_im[:, t - 1 - idx][:, :, None, :]
    win = jnp.concatenate([pj_re * bt_re[:, None] - pj_im * bt_im[:, None],
                           pj_re * bt_im[:, None] + pj_im * bt_re[:, None]], axis=-1)
    win_cat = win.astype(BF16).reshape(nt, gt, t, h, 2 * p).transpose(0, 2, 1, 3, 4)
    win_cat = win_cat.reshape(nt, t * gt * h, 2 * p)

    wo = jnp.stack([ca_re[:, 1:].reshape(g, t * h, p), -ca_im[:, 1:].reshape(g, t * h, p)], axis=1)
    wo_cat = wo.astype(BF16).transpose(0, 1, 3, 2).reshape(nt, gt, 2, p, t * h).transpose(0, 2, 1, 3, 4)
    wo_cat = wo_cat.reshape(nt, 2 * gt * p, t * h)

    kk = (t * jnp.arange(2 * SUBLANES, dtype=F32))[:, None, None]
    cmag = jnp.exp(kk * (lr * dt)[None])
    a_chunk_re = (cmag * jnp.cos(kk * (li * dt)[None])).reshape(2 * SUBLANES, g * p)
    a_chunk_im = (cmag * jnp.sin(kk * (li * dt)[None])).reshape(2 * SUBLANES, g * p)
    return k_cat.astype(BF16), win_cat, wo_cat, a_chunk_re, a_chunk_im


def _lane_repeat(blocks, width, copies):
    r = np.kron(np.eye(blocks, dtype=np.float32),
                np.kron(np.ones((1, copies), np.float32), np.eye(width, dtype=np.float32)))
    return jnp.asarray(r, BF16)


def _group_of(index, period, size):
    assert period & (period - 1) == 0 and size & (size - 1) == 0
    return lax.shift_right_logical(index & (period - 1), size.bit_length() - 1)


def _expand_block_diag(compact, rep, row_group, col_group):
    e = jnp.dot(compact, rep, preferred_element_type=F32)
    r = lax.broadcasted_iota(jnp.int32, e.shape, 0)
    c = lax.broadcasted_iota(jnp.int32, e.shape, 1)
    return jnp.where(row_group(r) == col_group(c), e, 0.0).astype(BF16)


def _chunk_lhs(u_ref):
    return jnp.concatenate([u_ref[j] for j in range(CHUNK)], axis=1)


def _s5_state_in_kernel(u_ref, wc_ref, rep_ref, sre_ref, sim_ref, w_scr):
    @pl.when(pl.program_id(1) == 0)
    def _():
        w_scr[...] = _expand_block_diag(
            wc_ref[0], rep_ref[...],
            lambda r: _group_of(r, LANES, SSM_GROUP),
            lambda c: _group_of(c, GROUPS_PER_TILE * SSM_STATE, SSM_STATE))

    s = jnp.dot(_chunk_lhs(u_ref), w_scr[...], preferred_element_type=F32)
    half = s.shape[1] // 2
    sre_ref[...] = s[:, :half]
    sim_ref[...] = s[:, half:]


def _s5_state_in(us3, win_cat, layer, *, rb):
    t, r, width = us3.shape
    nt = width // (2 * LANES)
    _, k, nc = win_cat.shape
    n = nc * GROUPS_PER_TILE
    rb = min(rb, r)
    rep = _lane_repeat(2, SSM_STATE, GROUPS_PER_TILE)
    return pl.pallas_call(
        _s5_state_in_kernel,
        grid=(nt, r // rb),
        in_specs=[pl.BlockSpec((t, rb, LANES), lambda q, i: (0, i, q)),
                  pl.BlockSpec((1, k, nc), lambda q, i: (layer * nt + q, 0, 0)),
                  pl.BlockSpec(rep.shape, lambda q, i: (0, 0))],
        out_specs=[pl.BlockSpec((rb, n // 2), lambda q, i: (i, q)),
                   pl.BlockSpec((rb, n // 2), lambda q, i: (i, q))],
        out_shape=[jax.ShapeDtypeStruct((r, nt * n // 2), F32)] * 2,
        scratch_shapes=[pltpu.VMEM((k, n), BF16)],
        compiler_params=_cparams("parallel", "arbitrary"),
        name="s5_state_in",
    )(us3, win_cat, rep)


def _s5_recurrence_kernel(sre_ref, sim_ref, pre_ref, pim_ref, hre_ref, him_ref, *, bsz, n_chunks):
    tn = sre_ref.shape[1]
    shape = (SUBLANES, tn)
    row = lax.broadcasted_iota(jnp.int32, shape, 0)
    pw_re = pre_ref[0:SUBLANES, :]
    pw_im = pim_ref[0:SUBLANES, :]
    a_tile_re = jnp.broadcast_to(pre_ref[SUBLANES:SUBLANES + 1, :], shape)
    a_tile_im = jnp.broadcast_to(pim_ref[SUBLANES:SUBLANES + 1, :], shape)

    def tile_step(it, carry):
        out = []
        for b in range(bsz):
            h_re, h_im = carry[2 * b], carry[2 * b + 1]
            rows = pl.ds(pl.multiple_of(b * n_chunks + it * SUBLANES, SUBLANES), SUBLANES)
            y_re, y_im = sre_ref[rows, :], sim_ref[rows, :]
            for d in (1, 2, 4):
                m_re = jnp.broadcast_to(pre_ref[d:d + 1, :], shape)
                m_im = jnp.broadcast_to(pim_ref[d:d + 1, :], shape)
                s_re = jnp.where(row >= d, pltpu.roll(y_re, d, axis=0), 0.0)
                s_im = jnp.where(row >= d, pltpu.roll(y_im, d, axis=0), 0.0)
                y_re, y_im = (y_re + m_re * s_re - m_im * s_im,
                              y_im + m_re * s_im + m_im * s_re)
            e_re = jnp.where(row >= 1, pltpu.roll(y_re, 1, axis=0), 0.0)
            e_im = jnp.where(row >= 1, pltpu.roll(y_im, 1, axis=0), 0.0)
            hre_ref[rows, :] = pw_re * h_re - pw_im * h_im + e_re
            him_ref[rows, :] = pw_re * h_im + pw_im * h_re + e_im
            last_re = jnp.broadcast_to(y_re[SUBLANES - 1:SUBLANES, :], shape)
            last_im = jnp.broadcast_to(y_im[SUBLANES - 1:SUBLANES, :], shape)
            out.append(a_tile_re * h_re - a_tile_im * h_im + last_re)
            out.append(a_tile_re * h_im + a_tile_im * h_re + last_im)
        return tuple(out)

    zero = jnp.zeros(shape, F32)
    lax.fori_loop(0, n_chunks // SUBLANES, tile_step, (zero,) * (2 * bsz))


def _s5_recurrence(s_re, s_im, pw_re, pw_im, layer, *, bsz, tn):
    r, n = s_re.shape
    tn = min(tn, n)
    spec = pl.BlockSpec((r, tn), lambda j: (0, j))
    pspec = pl.BlockSpec((pw_re.shape[0], tn), lambda j: (0, layer * (n // tn) + j))
    return pl.pallas_call(
        functools.partial(_s5_recurrence_kernel, bsz=bsz, n_chunks=r // bsz),
        grid=(n // tn,),
        in_specs=[spec, spec, pspec, pspec],
        out_specs=[spec, spec],
        out_shape=[jax.ShapeDtypeStruct((r, n), F32)] * 2,
        compiler_params=_cparams("parallel"),
        name="s5_recurrence",
    )(s_re, s_im, pw_re, pw_im)


def _s5_out_kernel(u_ref, kc_ref, hre_ref, him_ref, wc_ref, rep_ref, y_ref, t_scr, w_scr):
    @pl.when(pl.program_id(1) == 0)
    def _():
        rep = rep_ref[...]
        col_group = lambda c: _group_of(c, LANES, SSM_GROUP)
        taps = _expand_block_diag(kc_ref[0], rep, lambda r: _group_of(r, LANES, SSM_GROUP), col_group)
        t_scr[...] = jnp.zeros_like(t_scr)
        for j in range(CHUNK):
            t_scr[j * LANES:(j + 1) * LANES, j * LANES:] = taps[:, :(CHUNK - j) * LANES]
        w_scr[...] = _expand_block_diag(
            wc_ref[0], rep, lambda r: _group_of(r, GROUPS_PER_TILE * SSM_STATE, SSM_STATE), col_group)

    h = jnp.concatenate([hre_ref[...], him_ref[...]], axis=1).astype(BF16)
    lhs = _chunk_lhs(u_ref)
    band = CHUNK // S5_OUT_BANDS
    for c in range(S5_OUT_BANDS):
        cols = slice(c * band * LANES, (c + 1) * band * LANES)
        y = jnp.dot(lhs[:, :(c + 1) * band * LANES], t_scr[:(c + 1) * band * LANES, cols],
                    preferred_element_type=F32)
        y += jnp.dot(h, w_scr[:, cols], preferred_element_type=F32)
        for i in range(band):
            y_ref[c * band + i] = y[:, i * LANES:(i + 1) * LANES].astype(y_ref.dtype)


def _s5_out(us3, k_cat, h_re, h_im, wo_cat, layer, *, rb):
    t, r, width = us3.shape
    nt = width // (2 * LANES)
    _, ks, kc = wo_cat.shape
    k = t * LANES
    rb = min(rb, r)
    rep = _lane_repeat(t, SSM_GROUP, GROUPS_PER_TILE)
    return pl.pallas_call(
        _s5_out_kernel,
        grid=(nt, r // rb),
        in_specs=[pl.BlockSpec((t, rb, LANES), lambda q, i: (0, i, q)),
                  pl.BlockSpec((1, LANES, kc), lambda q, i: (layer * nt + q, 0, 0)),
                  pl.BlockSpec((rb, ks // 2), lambda q, i: (i, q)),
                  pl.BlockSpec((rb, ks // 2), lambda q, i: (i, q)),
                  pl.BlockSpec((1, ks, kc), lambda q, i: (layer * nt + q, 0, 0)),
                  pl.BlockSpec(rep.shape, lambda q, i: (0, 0))],
        out_specs=pl.BlockSpec((t, rb, LANES), lambda q, i: (0, i, q)),
        out_shape=jax.ShapeDtypeStruct((t, r, nt * LANES), BF16),
        scratch_shapes=[pltpu.VMEM((k, k), BF16), pltpu.VMEM((ks, k), BF16)],
        compiler_params=_cparams("parallel", "arbitrary"),
        name="s5_out",
    )(us3, k_cat, h_re, h_im, wo_cat, rep)


def _s5_glu_kernel(y_ref, gate_ref, w_ref, b_ref, pt_ref, o_ref):
    t, rb, w = y_ref.shape
    y = jax.nn.gelu(y_ref[...].reshape(t * rb, w).astype(F32))
    z = jnp.dot(y.astype(BF16), w_ref[...], preferred_element_type=F32) + b_ref[...]
    y = y * jax.nn.sigmoid(z)
    y = (y * jax.nn.silu(gate_ref[...].reshape(t * rb, w).astype(F32))).astype(BF16)
    o_ref[...] = jnp.dot(pt_ref[...], y, preferred_element_type=F32).astype(o_ref.dtype)


def _s5_glu(y3, us3, w_glu, b_glu, *, rb):
    t, r, w = y3.shape
    rb = min(rb, r)
    rows = t * rb
    return pl.pallas_call(
        _s5_glu_kernel,
        grid=(r // rb,),
        in_specs=[pl.BlockSpec((t, rb, w), lambda i: (0, i, 0)),
                  pl.BlockSpec((t, rb, w), lambda i: (0, i, 1)),
                  pl.BlockSpec((w, w), lambda i: (0, 0)),
                  pl.BlockSpec((1, w), lambda i: (0, 0)),
                  pl.BlockSpec((rows, rows), lambda i: (0, 0))],
        out_specs=pl.BlockSpec((rows, w), lambda i: (i, 0)),
        out_shape=jax.ShapeDtypeStruct((t * r, w), BF16),
        compiler_params=_cparams("parallel"),
        name="s5_glu",
    )(y3, us3, w_glu, b_glu.reshape(1, w), _chunk_major_perm(rows).T)


def _out_proj_kernel(ys_ref, yd_ref, yx_ref, w_ref, x_ref, g_ref, o_ref):
    ws, wd = ys_ref.shape[1], yd_ref.shape[1]
    mix = jnp.dot(ys_ref[...], w_ref[0:ws, :], preferred_element_type=F32)
    mix += jnp.dot(yd_ref[...], w_ref[ws:ws + wd, :], preferred_element_type=F32)
    mix += jnp.dot(yx_ref[...], w_ref[ws + wd:, :], preferred_element_type=F32)
    var = jnp.mean(mix * mix, axis=-1, keepdims=True)
    o_ref[...] = x_ref[...] + mix * lax.rsqrt(var + NORM_EPS) * g_ref[...]


def _out_proj(y_s, y_d, y_x, w_out, layer, x, g_post, *, tm):
    m, d = x.shape
    tm = min(tm, m)
    row = lambda i: (i, 0)
    const = lambda i: (0, 0)
    return pl.pallas_call(
        _out_proj_kernel,
        grid=(m // tm,),
        in_specs=[pl.BlockSpec((tm, y_s.shape[1]), row),
                  pl.BlockSpec((tm, y_d.shape[1]), row),
                  pl.BlockSpec((tm, y_x.shape[1]), row),
                  pl.BlockSpec((None,) + w_out.shape[1:], lambda i: (layer, 0, 0)),
                  pl.BlockSpec((tm, d), row),
                  pl.BlockSpec((1, d), const)],
        out_specs=pl.BlockSpec((tm, d), row),
        out_shape=jax.ShapeDtypeStruct((m, d), F32),
        compiler_params=_cparams("parallel"),
        name="out_proj",
    )(y_s, y_d, y_x, w_out, x, g_post.reshape(1, d))


def kernel(x, mem, positions, norm_pre, norm_post, norm_mem, w_in, w_out, w_mem_kv, ssm_a_re, ssm_a_im, ssm_log_dt, ssm_b_re, ssm_b_im, ssm_c_re, ssm_c_im, ssm_d, w_glu, b_glu, diff_lq1, diff_lk1, diff_lq2, diff_lk2, diff_subln):
    bsz, seq, d_model = x.shape
    mem_tokens = mem.shape[1]
    depth = w_in.shape[0]
    m = bsz * seq
    ssm_width = ssm_a_re.shape[1] * SSM_GROUP
    n_groups = ssm_width // SSM_GROUP
    diff_width = DIFF_HEADS * 2 * DIFF_HEAD_DIM
    xattn_width = XATTN_HEADS * XATTN_HEAD_DIM
    n_chunks = seq // CHUNK
    n_s5_cols = 2 * ssm_width
    col_qd = 0
    col_kd = col_qd + diff_width
    col_vd = col_kd + diff_width
    col_gd = col_vd + diff_width
    col_qx = col_gd + diff_width
    col_gx = col_qx + xattn_width
    assert col_kd == col_qd + diff_width and col_qd % diff_width == 0

    inv = ROPE_THETA ** (-jnp.arange(0, DIFF_HEAD_DIM, 2, dtype=F32) / DIFF_HEAD_DIM)
    ang = positions.astype(F32).reshape(m, 1) * inv
    cos, sin = jnp.cos(ang), jnp.sin(ang)
    cos_t = jnp.concatenate([cos, cos, cos, cos], axis=-1)
    sin_t = jnp.concatenate([-sin, sin, -sin, sin], axis=-1)

    xf = x.reshape(m, d_model)
    mem_f = mem.reshape(bsz * mem_tokens, d_model)
    w_in_bf, w_out_bf = w_in.astype(BF16), w_out.astype(BF16)
    all_groups = lambda a: a.reshape((depth * n_groups,) + a.shape[2:])
    k_cat, win_cat, wo_cat, a_re, a_im = _s5_weights(
        *(all_groups(a) for a in (ssm_a_re, ssm_a_im, ssm_log_dt, ssm_b_re, ssm_b_im,
                                  ssm_c_re, ssm_c_im, ssm_d)))
    for l in range(depth):
        lambda_init = 0.8 - 0.6 * math.exp(-0.3 * l)
        us3, proj = _in_proj(xf, norm_pre[l], w_in_bf, l, n_perm_cols=n_s5_cols, tm=1024, tn=512,
                             perm_rows=512)

        s_re, s_im = _s5_state_in(us3, win_cat, l, rb=512)
        h_re, h_im = _s5_recurrence(s_re, s_im, a_re, a_im, l, bsz=bsz, tn=1024)
        y3 = _s5_out(us3, k_cat, h_re, h_im, wo_cat, l, rb=512)
        y_s = _s5_glu(y3, us3, w_glu[l].astype(BF16), b_glu[l], rb=64)

        assert col_vd == col_kd + diff_width
        q_rot, k_rot, vt_ext = _rope(proj, cos_t, sin_t, bsz=bsz, seq=seq,
                                     q_col_block=col_qd // diff_width, tm=1024)
        lam = (jnp.exp(jnp.sum(diff_lq1[l].astype(F32) * diff_lk1[l].astype(F32)))
               - jnp.exp(jnp.sum(diff_lq2[l].astype(F32) * diff_lk2[l].astype(F32))) + lambda_init)
        sg_vec = (diff_subln[l].astype(F32) * (1.0 - lambda_init)).reshape(1, LANES)
        y_d = _diff_attn(q_rot, k_rot, vt_ext, proj, lam.reshape(1), sg_vec, bsz=bsz, seq=seq,
                         g_col=col_gd // LANES, tq=1024)

        mem_kv = _norm_matmul(mem_f, norm_mem[l], w_mem_kv[l].astype(BF16), tm=512, tn=1024, name="mem_kv")
        y_x = _mem_attn(proj, mem_kv, bsz=bsz, seq=seq, mem_tokens=mem_tokens,
                        q_col_block=col_qx // xattn_width, g_col_block=col_gx // xattn_width, tq=1024)

        xf = _out_proj(y_s, y_d, y_x, w_out_bf, l, xf, norm_post[l], tm=512)
    return xf.reshape(bsz, seq, d_model)
```

```python
import functools
import math

import jax
import jax.numpy as jnp
import numpy as np
from jax import lax
from jax.experimental import pallas as pl
from jax.experimental.pallas import tpu as pltpu

F32 = jnp.float32
BF16 = jnp.bfloat16

SSM_GROUP = 16
SSM_STATE = 64
CHUNK = 16
DIFF_HEADS = 4
DIFF_HEAD_DIM = 64
XATTN_HEADS = 4
XATTN_HEAD_DIM = 128
ROPE_THETA = 10000.0
NORM_EPS = 1e-6
MASK_VALUE = -1e30
LANES = 128
SUBLANES = 8
BF16_SUBLANES = 16
GROUPS_PER_TILE = LANES // SSM_GROUP
S5_OUT_BANDS = 8
VMEM_LIMIT = 56 * 1024 * 1024


def _cparams(*sem):
    return pltpu.CompilerParams(dimension_semantics=sem, vmem_limit_bytes=VMEM_LIMIT)


def _norm_matmul_kernel(x_ref, g_ref, w_ref, o_ref, h_ref):
    @pl.when(pl.program_id(1) == 0)
    def _():
        x = x_ref[...]
        var = jnp.mean(x * x, axis=-1, keepdims=True)
        h_ref[...] = (x * lax.rsqrt(var + NORM_EPS) * g_ref[...]).astype(BF16)

    o_ref[...] = jnp.dot(h_ref[...], w_ref[...], preferred_element_type=F32).astype(o_ref.dtype)


def _norm_matmul(x, g, w, *, tm, tn, name):
    m, d = x.shape
    n = w.shape[1]
    tm, tn = min(tm, m), min(tn, n)
    return pl.pallas_call(
        _norm_matmul_kernel,
        grid=(m // tm, n // tn),
        in_specs=[pl.BlockSpec((tm, d), lambda i, j: (i, 0)),
                  pl.BlockSpec((1, d), lambda i, j: (0, 0)),
                  pl.BlockSpec((d, tn), lambda i, j: (0, j))],
        out_specs=pl.BlockSpec((tm, tn), lambda i, j: (i, j)),
        out_shape=jax.ShapeDtypeStruct((m, n), BF16),
        scratch_shapes=[pltpu.VMEM((tm, d), BF16)],
        compiler_params=_cparams("parallel", "arbitrary"),
        name=name,
    )(x, g.reshape(1, d), w)


def _chunk_major_perm(rows):
    rb = rows // CHUNK
    src = (np.arange(rows) % rb) * CHUNK + np.arange(rows) // rb
    return jnp.asarray(np.eye(rows, dtype=np.float32)[src], BF16)


def _in_proj_kernel(x_ref, g_ref, w_ref, p_ref, o3_ref, o_ref, hp_ref, hn_ref, *, n_perm):
    j = pl.program_id(1)
    tm = x_ref.shape[0]
    pr = p_ref.shape[0]
    rb = pr // CHUNK

    @pl.when(j == 0)
    def _():
        for s in range(tm // pr):
            x = x_ref[s * pr:(s + 1) * pr, :]
            var = jnp.mean(x * x, axis=-1, keepdims=True)
            hn = (x * lax.rsqrt(var + NORM_EPS) * g_ref[...]).astype(BF16)
            hn_ref[s * pr:(s + 1) * pr, :] = hn
            hp_ref[s * pr:(s + 1) * pr, :] = jnp.dot(p_ref[...], hn, preferred_element_type=F32).astype(BF16)

    @pl.when(j < n_perm)
    def _():
        res = jnp.dot(hp_ref[...], w_ref[...], preferred_element_type=F32)
        for s in range(tm // pr):
            for i in range(CHUNK):
                rows = slice(s * pr + i * rb, s * pr + (i + 1) * rb)
                o3_ref[i, s * rb:(s + 1) * rb, :] = res[rows].astype(o3_ref.dtype)

    @pl.when(j >= n_perm)
    def _():
        o_ref[...] = jnp.dot(hn_ref[...], w_ref[...], preferred_element_type=F32).astype(o_ref.dtype)


def _in_proj(x, g, w, layer, *, n_perm_cols, tm, tn, perm_rows):
    m, d = x.shape
    n = w.shape[2]
    tm = min(tm, m)
    perm_rows = min(perm_rows, tm)
    n_perm = n_perm_cols // tn
    return pl.pallas_call(
        functools.partial(_in_proj_kernel, n_perm=n_perm),
        grid=(m // tm, n // tn),
        in_specs=[pl.BlockSpec((tm, d), lambda i, j: (i, 0)),
                  pl.BlockSpec((1, d), lambda i, j: (0, 0)),
                  pl.BlockSpec((None, d, tn), lambda i, j: (layer, 0, j)),
                  pl.BlockSpec((perm_rows, perm_rows), lambda i, j: (0, 0))],
        out_specs=[pl.BlockSpec((CHUNK, tm // CHUNK, tn), lambda i, j: (0, i, jnp.minimum(j, n_perm - 1))),
                   pl.BlockSpec((tm, tn), lambda i, j: (i, jnp.maximum(j - n_perm, 0)))],
        out_shape=[jax.ShapeDtypeStruct((CHUNK, m // CHUNK, n_perm_cols), BF16),
                   jax.ShapeDtypeStruct((m, n - n_perm_cols), BF16)],
        scratch_shapes=[pltpu.VMEM((tm, d), BF16), pltpu.VMEM((tm, d), BF16)],
        compiler_params=_cparams("parallel", "arbitrary"),
        name="in_proj",
    )(x, g.reshape(1, d), w, _chunk_major_perm(perm_rows))


def _rope_kernel(q_ref, k_ref, v_ref, cos_ref, sin_ref, qo_ref, ko_ref, vt_ref, *, q_scale):
    cos = cos_ref[...]
    sin = sin_ref[...]
    lane = lax.broadcasted_iota(jnp.int32, cos.shape, 1)
    first_half = (lane % DIFF_HEAD_DIM) < (DIFF_HEAD_DIM // 2)
    for x_ref, o_ref, mult in ((q_ref, qo_ref, q_scale), (k_ref, ko_ref, 1.0)):
        for h in range(DIFF_HEADS):
            x = x_ref[:, h * LANES:(h + 1) * LANES].astype(F32)
            partner = jnp.where(first_half,
                                pltpu.roll(x, LANES - DIFF_HEAD_DIM // 2, axis=1),
                                pltpu.roll(x, DIFF_HEAD_DIM // 2, axis=1))
            o_ref[:, h * LANES:(h + 1) * LANES] = ((x * cos + partner * sin) * mult).astype(o_ref.dtype)
    d = 2 * DIFF_HEAD_DIM
    for h in range(DIFF_HEADS):
        vt_ref[h, 0:d, :] = v_ref[:, h * d:(h + 1) * d].astype(F32).T.astype(vt_ref.dtype)
        vt_ref[h, d:, :] = jnp.ones((vt_ref.shape[1] - d, vt_ref.shape[2]), vt_ref.dtype)


def _rope(proj, cos_t, sin_t, *, bsz, seq, q_col_block, tm):
    m = proj.shape[0]
    tm = min(tm, seq)
    nb = seq // tm
    w = DIFF_HEADS * LANES
    v_rows = 2 * DIFF_HEAD_DIM + BF16_SUBLANES
    q_scale = DIFF_HEAD_DIM ** -0.5 * math.log2(math.e)
    return pl.pallas_call(
        functools.partial(_rope_kernel, q_scale=q_scale),
        grid=(m // tm,),
        in_specs=[pl.BlockSpec((tm, w), lambda i: (i, q_col_block)),
                  pl.BlockSpec((tm, w), lambda i: (i, q_col_block + 1)),
                  pl.BlockSpec((tm, w), lambda i: (i, q_col_block + 2)),
                  pl.BlockSpec((tm, LANES), lambda i: (i, 0)),
                  pl.BlockSpec((tm, LANES), lambda i: (i, 0))],
        out_specs=[pl.BlockSpec((tm, w), lambda i: (i, 0)),
                   pl.BlockSpec((tm, w), lambda i: (i, 0)),
                   pl.BlockSpec((DIFF_HEADS, v_rows, tm), lambda i: (i // nb, 0, i % nb))],
        out_shape=[jax.ShapeDtypeStruct((m, w), BF16),
                   jax.ShapeDtypeStruct((m, w), BF16),
                   jax.ShapeDtypeStruct((bsz * DIFF_HEADS, v_rows, seq), BF16)],
        compiler_params=_cparams("parallel"),
        name="rope",
    )(proj, proj, proj, cos_t, sin_t)


def _diff_attn_kernel(q_ref, k_ref, vt_ref, g_ref, lam_ref, sg_ref, o_ref, acc1, acc2, s_a, s_b, *, tk):
    qi = pl.program_id(2)
    tq = q_ref.shape[0]
    diag_blocks = tq // tk
    assert diag_blocks % 2 == 0
    q = q_ref[...]
    lane = lax.broadcasted_iota(jnp.int32, q.shape, 1)
    q_maps = (jnp.where(lane < DIFF_HEAD_DIM, q, jnp.zeros_like(q)),
              jnp.where(lane >= DIFF_HEAD_DIM, q, jnp.zeros_like(q)))
    accs = (acc1, acc2)
    acc1[...] = jnp.zeros_like(acc1)
    acc2[...] = jnp.zeros_like(acc2)

    def scores_into(s_ref, j):
        k = k_ref[pl.ds(pl.multiple_of(j * tk, tk), tk), :]
        for mi, qm in enumerate(q_maps):
            s_ref[mi] = lax.dot_general(k, qm, (((1,), (1,)), ((), ())),
                                        preferred_element_type=F32)

    def consume(s_ref, j, ms, key_offset=None):
        vt = vt_ref[0, :, pl.ds(pl.multiple_of(j * tk, tk), tk)]
        out = []
        for mi, (m_old, acc) in enumerate(zip(ms, accs)):
            s = s_ref[mi]
            if key_offset is not None:
                key = lax.broadcasted_iota(jnp.int32, s.shape, 0) + key_offset
                qry = lax.broadcasted_iota(jnp.int32, s.shape, 1)
                s = jnp.where(key <= qry, s, MASK_VALUE)
            m_new = jnp.maximum(m_old, jnp.max(s, axis=0, keepdims=True))
            p = jnp.exp2(s - m_new)
            acc[...] = acc[...] * jnp.exp2(m_old - m_new) + jnp.dot(vt, p.astype(BF16),
                                                                    preferred_element_type=F32)
            out.append(m_new)
        return tuple(out)

    def finish():
        d = 2 * DIFF_HEAD_DIM
        a1, a2 = acc1[...], acc2[...]
        o_t = a1[:d] / a1[d:d + 1] - lam_ref[0] * (a2[:d] / a2[d:d + 1])
        o = o_t.T
        var = jnp.mean(o * o, axis=-1, keepdims=True)
        o = o * lax.rsqrt(var + NORM_EPS) * sg_ref[...]
        o_ref[...] = (o * jax.nn.silu(g_ref[...].astype(F32))).astype(o_ref.dtype)

    scores_into(s_a, 0)

    def two_blocks(i, ms):
        scores_into(s_b, 2 * i + 1)
        ms = consume(s_a, 2 * i, ms)
        scores_into(s_a, 2 * i + 2)
        return consume(s_b, 2 * i + 1, ms)

    m0 = jnp.full((1, tq), MASK_VALUE, F32)
    first_diag = qi * diag_blocks
    ms = lax.fori_loop(0, first_diag // 2, two_blocks, (m0, m0))
    for d in range(diag_blocks):
        cur, nxt = (s_a, s_b) if d % 2 == 0 else (s_b, s_a)
        if d + 1 < diag_blocks:
            scores_into(nxt, first_diag + d + 1)
        ms = consume(cur, first_diag + d, ms, key_offset=d * tk)
    finish()


def _diff_attn(q_rot, k_rot, vt_ext, proj, lam, sg_vec, *, bsz, seq, g_col, tq):
    tq = min(tq, seq)
    tk = tq // 2
    nq = seq // tq
    v_rows = vt_ext.shape[1]
    return pl.pallas_call(
        functools.partial(_diff_attn_kernel, tk=tk),
        grid=(bsz, DIFF_HEADS, nq),
        in_specs=[
            pl.BlockSpec((tq, LANES), lambda b, h, i: (b * nq + i, h)),
            pl.BlockSpec((seq, LANES), lambda b, h, i: (b, h)),
            pl.BlockSpec((1, v_rows, seq), lambda b, h, i: (b * DIFF_HEADS + h, 0, 0)),
            pl.BlockSpec((tq, LANES), lambda b, h, i: (b * nq + i, g_col + h)),
            pl.BlockSpec(memory_space=pltpu.SMEM),
            pl.BlockSpec((1, LANES), lambda b, h, i: (0, 0)),
        ],
        out_specs=pl.BlockSpec((tq, LANES), lambda b, h, i: (b * nq + i, h)),
        out_shape=jax.ShapeDtypeStruct((bsz * seq, DIFF_HEADS * LANES), BF16),
        scratch_shapes=[pltpu.VMEM((v_rows, tq), F32), pltpu.VMEM((v_rows, tq), F32),
                        pltpu.VMEM((2, tk, tq), F32), pltpu.VMEM((2, tk, tq), F32)],
        compiler_params=_cparams("parallel", "parallel", "arbitrary"),
        name="diff_attn",
    )(q_rot, k_rot, vt_ext, proj, lam, sg_vec)


def _mem_attn_kernel(q_ref, g_ref, mk_ref, mv_ref, o_ref):
    scale = XATTN_HEAD_DIM ** -0.5
    for h in range(XATTN_HEADS):
        sl = slice(h * XATTN_HEAD_DIM, (h + 1) * XATTN_HEAD_DIM)
        s = lax.dot_general(q_ref[:, sl], mk_ref[:, sl], (((1,), (1,)), ((), ())),
                            preferred_element_type=F32) * scale
        p = jnp.exp(s - jnp.max(s, axis=-1, keepdims=True))
        p = p / jnp.sum(p, axis=-1, keepdims=True)
        o = jnp.dot(p.astype(BF16), mv_ref[:, sl], preferred_element_type=F32)
        o_ref[:, sl] = (o * jax.nn.silu(g_ref[:, sl].astype(F32))).astype(o_ref.dtype)


def _mem_attn(proj, mem_kv, *, bsz, seq, mem_tokens, q_col_block, g_col_block, tq):
    tq = min(tq, seq)
    nq = seq // tq
    w = XATTN_HEADS * XATTN_HEAD_DIM
    return pl.pallas_call(
        _mem_attn_kernel,
        grid=(bsz, nq),
        in_specs=[pl.BlockSpec((tq, w), lambda b, i: (b * nq + i, q_col_block)),
                  pl.BlockSpec((tq, w), lambda b, i: (b * nq + i, g_col_block)),
                  pl.BlockSpec((mem_tokens, w), lambda b, i: (b, 0)),
                  pl.BlockSpec((mem_tokens, w), lambda b, i: (b, 1))],
        out_specs=pl.BlockSpec((tq, w), lambda b, i: (b * nq + i, 0)),
        out_shape=jax.ShapeDtypeStruct((bsz * seq, w), BF16),
        compiler_params=_cparams("parallel", "parallel"),
        name="mem_attn",
    )(proj, proj, mem_kv, mem_kv)


def _s5_taps_kernel(bt_ref, ca_ref, d_ref, o_ref):
    for s in range(bt_ref.shape[0]):
        k = lax.dot_general(bt_ref[s], ca_ref[s], (((1,), (1,)), ((), ())),
                            preferred_element_type=F32, precision=lax.Precision.HIGHEST)
        row = lax.broadcasted_iota(jnp.int32, k.shape, 0)
        col = lax.broadcasted_iota(jnp.int32, k.shape, 1)
        o_ref[s] = k + jnp.where(row == col, d_ref[s], 0.0)


def _s5_taps(bt, ca, d_pad):
    g, h, k = bt.shape
    n = ca.shape[1]
    gb = GROUPS_PER_TILE
    return pl.pallas_call(
        _s5_taps_kernel,
        grid=(g // gb,),
        in_specs=[pl.BlockSpec((gb, h, k), lambda i: (i, 0, 0)),
                  pl.BlockSpec((gb, n, k), lambda i: (i, 0, 0)),
                  pl.BlockSpec((gb, 1, n), lambda i: (i, 0, 0))],
        out_specs=pl.BlockSpec((gb, h, n), lambda i: (i, 0, 0)),
        out_shape=jax.ShapeDtypeStruct((g, h, n), F32),
        compiler_params=_cparams("parallel"),
        name="s5_taps",
    )(bt, ca, d_pad)


def _s5_weights(a_re, a_im, log_dt, b_re, b_im, c_re, c_im, d_skip):
    g = a_re.shape[0]
    p, h, t = SSM_STATE, SSM_GROUP, CHUNK
    dt = jnp.exp(log_dt.astype(F32))[:, None]
    lr, li = a_re.astype(F32), a_im.astype(F32)
    mag = jnp.exp(lr * dt)
    abar_re, abar_im = mag * jnp.cos(li * dt), mag * jnp.sin(li * dt)
    den = lr * lr + li * li
    nr, ni = abar_re - 1.0, abar_im
    z_re = ((nr * lr + ni * li) / den)[:, None, :]
    z_im = ((ni * lr - nr * li) / den)[:, None, :]
    br, bi = b_re.astype(F32).transpose(0, 2, 1), b_im.astype(F32).transpose(0, 2, 1)
    bt_re = z_re * br - z_im * bi
    bt_im = z_re * bi + z_im * br
    cr, ci = c_re.astype(F32), c_im.astype(F32)

    tau = jnp.arange(t + 1, dtype=F32)[None, :, None]
    pmag = jnp.exp(tau * (lr * dt)[:, None, :])
    pw_re = pmag * jnp.cos(tau * (li * dt)[:, None, :])
    pw_im = pmag * jnp.sin(tau * (li * dt)[:, None, :])

    ca_re = cr[:, None] * pw_re[:, :, None, :] - ci[:, None] * pw_im[:, :, None, :]
    ca_im = cr[:, None] * pw_im[:, :, None, :] + ci[:, None] * pw_re[:, :, None, :]

    ca = jnp.concatenate([ca_re[:, :t], ca_im[:, :t]], axis=-1).reshape(g, t * h, 2 * p)
    bt = jnp.concatenate([bt_re, -bt_im], axis=-1)
    d_pad = jnp.pad(d_skip.astype(F32), ((0, 0), (0, (t - 1) * h)))[:, None, :]
    taps_t = _s5_taps(bt, ca, d_pad)

    gt = GROUPS_PER_TILE
    nt = g // gt
    idx = jnp.arange(t)

    k_cat = taps_t.reshape(nt, gt * h, t * h)

    pj_re, pj_im = pw_re[:, t - 1 - idx][:, :, None, :], pw_im[:, t - 1 - idx][:, :, None, :]
    win = jnp.concatenate([pj_re * bt_re[:, None] - pj_im * bt_im[:, None],
                           pj_re * bt_im[:, None] + pj_im * bt_re[:, None]], axis=-1)
    win_cat = win.astype(BF16).reshape(nt, gt, t, h, 2 * p).transpose(0, 2, 1, 3, 4)
    win_cat = win_cat.reshape(nt, t * gt * h, 2 * p)

    wo = jnp.stack([ca_re[:, 1:].reshape(g, t * h, p), -ca_im[:, 1:].reshape(g, t * h, p)], axis=1)
    wo_cat = wo.astype(BF16).transpose(0, 1, 3, 2).reshape(nt, gt, 2, p, t * h).transpose(0, 2, 1, 3, 4)
    wo_cat = wo_cat.reshape(nt, 2 * gt * p, t * h)

    kk = (t * jnp.arange(2 * SUBLANES, dtype=F32))[:, None, None]
    cmag = jnp.exp(kk * (lr * dt)[None])
    a_chunk_re = (cmag * jnp.cos(kk * (li * dt)[None])).reshape(2 * SUBLANES, g * p)
    a_chunk_im = (cmag * jnp.sin(kk * (li * dt)[None])).reshape(2 * SUBLANES, g * p)
    return k_cat.astype(BF16), win_cat, wo_cat, a_chunk_re, a_chunk_im


def _lane_repeat(blocks, width, copies):
    r = np.kron(np.eye(blocks, dtype=np.float32),
                np.kron(np.ones((1, copies), np.float32), np.eye(width, dtype=np.float32)))
    return jnp.asarray(r, BF16)


def _group_of(index, period, size):
    assert period & (period - 1) == 0 and size & (size - 1) == 0
    return lax.shift_right_logical(index & (period - 1), size.bit_length() - 1)


def _expand_block_diag(compact, rep, row_group, col_group):
    e = jnp.dot(compact, rep, preferred_element_type=F32)
    r = lax.broadcasted_iota(jnp.int32, e.shape, 0)
    c = lax.broadcasted_iota(jnp.int32, e.shape, 1)
    return jnp.where(row_group(r) == col_group(c), e, 0.0).astype(BF16)


def _chunk_lhs(u_ref):
    return jnp.concatenate([u_ref[j] for j in range(CHUNK)], axis=1)


def _s5_state_in_kernel(u_ref, wc_ref, rep_ref, sre_ref, sim_ref, w_scr):
    @pl.when(pl.program_id(1) == 0)
    def _():
        w_scr[...] = _expand_block_diag(
            wc_ref[0], rep_ref[...],
            lambda r: _group_of(r, LANES, SSM_GROUP),
            lambda c: _group_of(c, GROUPS_PER_TILE * SSM_STATE, SSM_STATE))

    s = jnp.dot(_chunk_lhs(u_ref), w_scr[...], preferred_element_type=F32)
    half = s.shape[1] // 2
    sre_ref[...] = s[:, :half]
    sim_ref[...] = s[:, half:]


def _s5_state_in(us3, win_cat, layer, *, rb):
    t, r, width = us3.shape
    nt = width // (2 * LANES)
    _, k, nc = win_cat.shape
    n = nc * GROUPS_PER_TILE
    rb = min(rb, r)
    rep = _lane_repeat(2, SSM_STATE, GROUPS_PER_TILE)
    return pl.pallas_call(
        _s5_state_in_kernel,
        grid=(nt, r // rb),
        in_specs=[pl.BlockSpec((t, rb, LANES), lambda q, i: (0, i, q)),
                  pl.BlockSpec((1, k, nc), lambda q, i: (layer * nt + q, 0, 0)),
                  pl.BlockSpec(rep.shape, lambda q, i: (0, 0))],
        out_specs=[pl.BlockSpec((rb, n // 2), lambda q, i: (i, q)),
                   pl.BlockSpec((rb, n // 2), lambda q, i: (i, q))],
        out_shape=[jax.ShapeDtypeStruct((r, nt * n // 2), F32)] * 2,
        scratch_shapes=[pltpu.VMEM((k, n), BF16)],
        compiler_params=_cparams("parallel", "arbitrary"),
        name="s5_state_in",
    )(us3, win_cat, rep)


def _s5_recurrence_kernel(sre_ref, sim_ref, pre_ref, pim_ref, hre_ref, him_ref, *, bsz, n_chunks):
    tn = sre_ref.shape[1]
    shape = (SUBLANES, tn)
    row = lax.broadcasted_iota(jnp.int32, shape, 0)
    pw_re = pre_ref[0:SUBLANES, :]
    pw_im = pim_ref[0:SUBLANES, :]
    a_tile_re = jnp.broadcast_to(pre_ref[SUBLANES:SUBLANES + 1, :], shape)
    a_tile_im = jnp.broadcast_to(pim_ref[SUBLANES:SUBLANES + 1, :], shape)

    def tile_step(it, carry):
        out = []
        for b in range(bsz):
            h_re, h_im = carry[2 * b], carry[2 * b + 1]
            rows = pl.ds(pl.multiple_of(b * n_chunks + it * SUBLANES, SUBLANES), SUBLANES)
            y_re, y_im = sre_ref[rows, :], sim_ref[rows, :]
            for d in (1, 2, 4):
                m_re = jnp.broadcast_to(pre_ref[d:d + 1, :], shape)
                m_im = jnp.broadcast_to(pim_ref[d:d + 1, :], shape)
                s_re = jnp.where(row >= d, pltpu.roll(y_re, d, axis=0), 0.0)
                s_im = jnp.where(row >= d, pltpu.roll(y_im, d, axis=0), 0.0)
                y_re, y_im = (y_re + m_re * s_re - m_im * s_im,
                              y_im + m_re * s_im + m_im * s_re)
            e_re = jnp.where(row >= 1, pltpu.roll(y_re, 1, axis=0), 0.0)
            e_im = jnp.where(row >= 1, pltpu.roll(y_im, 1, axis=0), 0.0)
            hre_ref[rows, :] = pw_re * h_re - pw_im * h_im + e_re
            him_ref[rows, :] = pw_re * h_im + pw_im * h_re + e_im
            last_re = jnp.broadcast_to(y_re[SUBLANES - 1:SUBLANES, :], shape)
            last_im = jnp.broadcast_to(y_im[SUBLANES - 1:SUBLANES, :], shape)
            out.append(a_tile_re * h_re - a_tile_im * h_im + last_re)
            out.append(a_tile_re * h_im + a_tile_im * h_re + last_im)
        return tuple(out)

    zero = jnp.zeros(shape, F32)
    lax.fori_loop(0, n_chunks // SUBLANES, tile_step, (zero,) * (2 * bsz))


def _s5_recurrence(s_re, s_im, pw_re, pw_im, layer, *, bsz, tn):
    r, n = s_re.shape
    tn = min(tn, n)
    spec = pl.BlockSpec((r, tn), lambda j: (0, j))
    pspec = pl.BlockSpec((pw_re.shape[0], tn), lambda j: (0, layer * (n // tn) + j))
    return pl.pallas_call(
        functools.partial(_s5_recurrence_kernel, bsz=bsz, n_chunks=r // bsz),
        grid=(n // tn,),
        in_specs=[spec, spec, pspec, pspec],
        out_specs=[spec, spec],
        out_shape=[jax.ShapeDtypeStruct((r, n), F32)] * 2,
        compiler_params=_cparams("parallel"),
        name="s5_recurrence",
    )(s_re, s_im, pw_re, pw_im)


def _s5_out_kernel(u_ref, kc_ref, hre_ref, him_ref, wc_ref, rep_ref, y_ref, t_scr, w_scr):
    @pl.when(pl.program_id(1) == 0)
    def _():
        rep = rep_ref[...]
        col_group = lambda c: _group_of(c, LANES, SSM_GROUP)
        taps = _expand_block_diag(kc_ref[0], rep, lambda r: _group_of(r, LANES, SSM_GROUP), col_group)
        t_scr[...] = jnp.zeros_like(t_scr)
        for j in range(CHUNK):
            t_scr[j * LANES:(j + 1) * LANES, j * LANES:] = taps[:, :(CHUNK - j) * LANES]
        w_scr[...] = _expand_block_diag(
            wc_ref[0], rep, lambda r: _group_of(r, GROUPS_PER_TILE * SSM_STATE, SSM_STATE), col_group)

    h = jnp.concatenate([hre_ref[...], him_ref[...]], axis=1).astype(BF16)
    lhs = _chunk_lhs(u_ref)
    band = CHUNK // S5_OUT_BANDS
    for c in range(S5_OUT_BANDS):
        cols = slice(c * band * LANES, (c + 1) * band * LANES)
        y = jnp.dot(lhs[:, :(c + 1) * band * LANES], t_scr[:(c + 1) * band * LANES, cols],
                    preferred_element_type=F32)
        y += jnp.dot(h, w_scr[:, cols], preferred_element_type=F32)
        for i in range(band):
            y_ref[c * band + i] = y[:, i * LANES:(i + 1) * LANES].astype(y_ref.dtype)


def _s5_out(us3, k_cat, h_re, h_im, wo_cat, layer, *, rb):
    t, r, width = us3.shape
    nt = width // (2 * LANES)
    _, ks, kc = wo_cat.shape
    k = t * LANES
    rb = min(rb, r)
    rep = _lane_repeat(t, SSM_GROUP, GROUPS_PER_TILE)
    return pl.pallas_call(
        _s5_out_kernel,
        grid=(nt, r // rb),
        in_specs=[pl.BlockSpec((t, rb, LANES), lambda q, i: (0, i, q)),
                  pl.BlockSpec((1, LANES, kc), lambda q, i: (layer * nt + q, 0, 0)),
                  pl.BlockSpec((rb, ks // 2), lambda q, i: (i, q)),
                  pl.BlockSpec((rb, ks // 2), lambda q, i: (i, q)),
                  pl.BlockSpec((1, ks, kc), lambda q, i: (layer * nt + q, 0, 0)),
                  pl.BlockSpec(rep.shape, lambda q, i: (0, 0))],
        out_specs=pl.BlockSpec((t, rb, LANES), lambda q, i: (0, i, q)),
        out_shape=jax.ShapeDtypeStruct((t, r, nt * LANES), BF16),
        scratch_shapes=[pltpu.VMEM((k, k), BF16), pltpu.VMEM((ks, k), BF16)],
        compiler_params=_cparams("parallel", "arbitrary"),
        name="s5_out",
    )(us3, k_cat, h_re, h_im, wo_cat, rep)


def _s5_glu_kernel(y_ref, gate_ref, w_ref, b_ref, pt_ref, o_ref):
    t, rb, w = y_ref.shape
    y = jax.nn.gelu(y_ref[...].reshape(t * rb, w).astype(F32))
    z = jnp.dot(y.astype(BF16), w_ref[...], preferred_element_type=F32) + b_ref[...]
    y = y * jax.nn.sigmoid(z)
    y = (y * jax.nn.silu(gate_ref[...].reshape(t * rb, w).astype(F32))).astype(BF16)
    o_ref[...] = jnp.dot(pt_ref[...], y, preferred_element_type=F32).astype(o_ref.dtype)


def _s5_glu(y3, us3, w_glu, b_glu, *, rb):
    t, r, w = y3.shape
    rb = min(rb, r)
    rows = t * rb
    return pl.pallas_call(
        _s5_glu_kernel,
        grid=(r // rb,),
        in_specs=[pl.BlockSpec((t, rb, w), lambda i: (0, i, 0)),
                  pl.BlockSpec((t, rb, w), lambda i: (0, i, 1)),
                  pl.BlockSpec((w, w), lambda i: (0, 0)),
                  pl.BlockSpec((1, w), lambda i: (0, 0)),
                  pl.BlockSpec((rows, rows), lambda i: (0, 0))],
        out_specs=pl.BlockSpec((rows, w), lambda i: (i, 0)),
        out_shape=jax.ShapeDtypeStruct((t * r, w), BF16),
        compiler_params=_cparams("parallel"),
        name="s5_glu",
    )(y3, us3, w_glu, b_glu.reshape(1, w), _chunk_major_perm(rows).T)


def _out_proj_kernel(ys_ref, yd_ref, yx_ref, w_ref, x_ref, g_ref, o_ref):
    ws, wd = ys_ref.shape[1], yd_ref.shape[1]
    mix = jnp.dot(ys_ref[...], w_ref[0:ws, :], preferred_element_type=F32)
    mix += jnp.dot(yd_ref[...], w_ref[ws:ws + wd, :], preferred_element_type=F32)
    mix += jnp.dot(yx_ref[...], w_ref[ws + wd:, :], preferred_element_type=F32)
    var = jnp.mean(mix * mix, axis=-1, keepdims=True)
    o_ref[...] = x_ref[...] + mix * lax.rsqrt(var + NORM_EPS) * g_ref[...]


def _out_proj(y_s, y_d, y_x, w_out, layer, x, g_post, *, tm):
    m, d = x.shape
    tm = min(tm, m)
    row = lambda i: (i, 0)
    const = lambda i: (0, 0)
    return pl.pallas_call(
        _out_proj_kernel,
        grid=(m // tm,),
        in_specs=[pl.BlockSpec((tm, y_s.shape[1]), row),
                  pl.BlockSpec((tm, y_d.shape[1]), row),
                  pl.BlockSpec((tm, y_x.shape[1]), row),
                  pl.BlockSpec((None,) + w_out.shape[1:], lambda i: (layer, 0, 0)),
                  pl.BlockSpec((tm, d), row),
                  pl.BlockSpec((1, d), const)],
        out_specs=pl.BlockSpec((tm, d), row),
        out_shape=jax.ShapeDtypeStruct((m, d), F32),
        compiler_params=_cparams("parallel"),
        name="out_proj",
    )(y_s, y_d, y_x, w_out, x, g_post.reshape(1, d))


def kernel(x, mem, positions, norm_pre, norm_post, norm_mem, w_in, w_out, w_mem_kv, ssm_a_re, ssm_a_im, ssm_log_dt, ssm_b_re, ssm_b_im, ssm_c_re, ssm_c_im, ssm_d, w_glu, b_glu, diff_lq1, diff_lk1, diff_lq2, diff_lk2, diff_subln):
    bsz, seq, d_model = x.shape
    mem_tokens = mem.shape[1]
    depth = w_in.shape[0]
    m = bsz * seq
    ssm_width = ssm_a_re.shape[1] * SSM_GROUP
    n_groups = ssm_width // SSM_GROUP
    diff_width = DIFF_HEADS * 2 * DIFF_HEAD_DIM
    xattn_width = XATTN_HEADS * XATTN_HEAD_DIM
    n_chunks = seq // CHUNK
    n_s5_cols = 2 * ssm_width
    col_qd = 0
    col_kd = col_qd + diff_width
    col_vd = col_kd + diff_width
    col_gd = col_vd + diff_width
    col_qx = col_gd + diff_width
    col_gx = col_qx + xattn_width
    assert col_kd == col_qd + diff_width and col_qd % diff_width == 0

    inv = ROPE_THETA ** (-jnp.arange(0, DIFF_HEAD_DIM, 2, dtype=F32) / DIFF_HEAD_DIM)
    half = DIFF_HEAD_DIM // 2
    ang = positions.astype(F32).reshape(m, 1) * jnp.tile(inv, LANES // half)
    cos_t = jnp.cos(ang)
    sin_t = jnp.sin(ang) * jnp.tile(jnp.concatenate([-jnp.ones(half, F32), jnp.ones(half, F32)]),
                                    LANES // DIFF_HEAD_DIM)

    xf = x.reshape(m, d_model)
    mem_f = mem.reshape(bsz * mem_tokens, d_model)
    w_in_bf, w_out_bf = w_in.astype(BF16), w_out.astype(BF16)
    all_groups = lambda a: a.reshape((depth * n_groups,) + a.shape[2:])
    k_cat, win_cat, wo_cat, a_re, a_im = _s5_weights(
        *(all_groups(a) for a in (ssm_a_re, ssm_a_im, ssm_log_dt, ssm_b_re, ssm_b_im,
                                  ssm_c_re, ssm_c_im, ssm_d)))
    for l in range(depth):
        lambda_init = 0.8 - 0.6 * math.exp(-0.3 * l)
        us3, proj = _in_proj(xf, norm_pre[l], w_in_bf, l, n_perm_cols=n_s5_cols, tm=1024, tn=512,
                             perm_rows=512)

        s_re, s_im = _s5_state_in(us3, win_cat, l, rb=512)
        h_re, h_im = _s5_recurrence(s_re, s_im, a_re, a_im, l, bsz=bsz, tn=1024)
        y3 = _s5_out(us3, k_cat, h_re, h_im, wo_cat, l, rb=512)
        y_s = _s5_glu(y3, us3, w_glu[l].astype(BF16), b_glu[l], rb=64)

        assert col_vd == col_kd + diff_width
        q_rot, k_rot, vt_ext = _rope(proj, cos_t, sin_t, bsz=bsz, seq=seq,
                                     q_col_block=col_qd // diff_width, tm=1024)
        lam = (jnp.exp(jnp.sum(diff_lq1[l].astype(F32) * diff_lk1[l].astype(F32)))
               - jnp.exp(jnp.sum(diff_lq2[l].astype(F32) * diff_lk2[l].astype(F32))) + lambda_init)
        sg_vec = (diff_subln[l].astype(F32) * (1.0 - lambda_init)).reshape(1, LANES)
        y_d = _diff_attn(q_rot, k_rot, vt_ext, proj, lam.reshape(1), sg_vec, bsz=bsz, seq=seq,
                         g_col=col_gd // LANES, tq=1024)

        mem_kv = _norm_matmul(mem_f, norm_mem[l], w_mem_kv[l].astype(BF16), tm=512, tn=1024, name="mem_kv")
        y_x = _mem_attn(proj, mem_kv, bsz=bsz, seq=seq, mem_tokens=mem_tokens,
                        q_col_block=col_qx // xattn_width, g_col_block=col_gx // xattn_width, tq=1024)

        xf = _out_proj(y_s, y_d, y_x, w_out_bf, l, xf, norm_post[l], tm=512)
    return xf.reshape(bsz, seq, d_model)
```

```python
import functools
import math

import jax
import jax.numpy as jnp
import numpy as np
from jax import lax
from jax.experimental import pallas as pl
from jax.experimental.pallas import tpu as pltpu

F32 = jnp.float32
BF16 = jnp.bfloat16

SSM_GROUP = 16
SSM_STATE = 64
CHUNK = 16
DIFF_HEADS = 4
DIFF_HEAD_DIM = 64
XATTN_HEADS = 4
XATTN_HEAD_DIM = 128
ROPE_THETA = 10000.0
NORM_EPS = 1e-6
MASK_VALUE = -1e30
LANES = 128
SUBLANES = 8
BF16_SUBLANES = 16
GROUPS_PER_TILE = LANES // SSM_GROUP
S5_OUT_BANDS = 8
VMEM_LIMIT = 56 * 1024 * 1024


def _cparams(*sem):
    return pltpu.CompilerParams(dimension_semantics=sem, vmem_limit_bytes=VMEM_LIMIT)


def _norm_matmul_kernel(x_ref, g_ref, w_ref, o_ref, h_ref):
    @pl.when(pl.program_id(1) == 0)
    def _():
        x = x_ref[...]
        var = jnp.mean(x * x, axis=-1, keepdims=True)
        h_ref[...] = (x * lax.rsqrt(var + NORM_EPS) * g_ref[...]).astype(BF16)

    o_ref[...] = jnp.dot(h_ref[...], w_ref[...], preferred_element_type=F32).astype(o_ref.dtype)


def _norm_matmul(x, g, w, *, tm, tn, name):
    m, d = x.shape
    n = w.shape[1]
    tm, tn = min(tm, m), min(tn, n)
    return pl.pallas_call(
        _norm_matmul_kernel,
        grid=(m // tm, n // tn),
        in_specs=[pl.BlockSpec((tm, d), lambda i, j: (i, 0)),
                  pl.BlockSpec((1, d), lambda i, j: (0, 0)),
                  pl.BlockSpec((d, tn), lambda i, j: (0, j))],
        out_specs=pl.BlockSpec((tm, tn), lambda i, j: (i, j)),
        out_shape=jax.ShapeDtypeStruct((m, n), BF16),
        scratch_shapes=[pltpu.VMEM((tm, d), BF16)],
        compiler_params=_cparams("parallel", "arbitrary"),
        name=name,
    )(x, g.reshape(1, d), w)


def _chunk_major_perm(rows):
    rb = rows // CHUNK
    src = (np.arange(rows) % rb) * CHUNK + np.arange(rows) // rb
    return jnp.asarray(np.eye(rows, dtype=np.float32)[src], BF16)


def _in_proj_kernel(x_ref, g_ref, w_ref, p_ref, o3_ref, o_ref, hp_ref, hn_ref, *, n_perm):
    j = pl.program_id(1)
    tm = x_ref.shape[0]
    pr = p_ref.shape[0]
    rb = pr // CHUNK

    @pl.when(j == 0)
    def _():
        for s in range(tm // pr):
            x = x_ref[s * pr:(s + 1) * pr, :]
            var = jnp.mean(x * x, axis=-1, keepdims=True)
            hn = (x * lax.rsqrt(var + NORM_EPS) * g_ref[...]).astype(BF16)
            hn_ref[s * pr:(s + 1) * pr, :] = hn
            hp_ref[s * pr:(s + 1) * pr, :] = jnp.dot(p_ref[...], hn, preferred_element_type=F32).astype(BF16)

    @pl.when(j < n_perm)
    def _():
        res = jnp.dot(hp_ref[...], w_ref[...], preferred_element_type=F32)
        for s in range(tm // pr):
            for i in range(CHUNK):
                rows = slice(s * pr + i * rb, s * pr + (i + 1) * rb)
                o3_ref[i, s * rb:(s + 1) * rb, :] = res[rows].astype(o3_ref.dtype)

    @pl.when(j >= n_perm)
    def _():
        o_ref[...] = jnp.dot(hn_ref[...], w_ref[...], preferred_element_type=F32).astype(o_ref.dtype)


def _in_proj(x, g, w, layer, *, n_perm_cols, tm, tn, perm_rows):
    m, d = x.shape
    n = w.shape[2]
    tm = min(tm, m)
    perm_rows = min(perm_rows, tm)
    n_perm = n_perm_cols // tn
    return pl.pallas_call(
        functools.partial(_in_proj_kernel, n_perm=n_perm),
        grid=(m // tm, n // tn),
        in_specs=[pl.BlockSpec((tm, d), lambda i, j: (i, 0)),
                  pl.BlockSpec((1, d), lambda i, j: (0, 0)),
                  pl.BlockSpec((None, d, tn), lambda i, j: (layer, 0, j)),
                  pl.BlockSpec((perm_rows, perm_rows), lambda i, j: (0, 0))],
        out_specs=[pl.BlockSpec((CHUNK, tm // CHUNK, tn), lambda i, j: (0, i, jnp.minimum(j, n_perm - 1))),
                   pl.BlockSpec((tm, tn), lambda i, j: (i, jnp.maximum(j - n_perm, 0)))],
        out_shape=[jax.ShapeDtypeStruct((CHUNK, m // CHUNK, n_perm_cols), BF16),
                   jax.ShapeDtypeStruct((m, n - n_perm_cols), BF16)],
        scratch_shapes=[pltpu.VMEM((tm, d), BF16), pltpu.VMEM((tm, d), BF16)],
        compiler_params=_cparams("parallel", "arbitrary"),
        name="in_proj",
    )(x, g.reshape(1, d), w, _chunk_major_perm(perm_rows))


def _rope_kernel(q_ref, k_ref, v_ref, cos_ref, sin_ref, qo_ref, ko_ref, vt_ref, *, q_scale):
    cos = cos_ref[...]
    sin = sin_ref[...]
    lane = lax.broadcasted_iota(jnp.int32, cos.shape, 1)
    first_half = (lane % DIFF_HEAD_DIM) < (DIFF_HEAD_DIM // 2)
    for x_ref, o_ref, mult in ((q_ref, qo_ref, q_scale), (k_ref, ko_ref, 1.0)):
        for h in range(DIFF_HEADS):
            x = x_ref[:, h * LANES:(h + 1) * LANES].astype(F32)
            partner = jnp.where(first_half,
                                pltpu.roll(x, LANES - DIFF_HEAD_DIM // 2, axis=1),
                                pltpu.roll(x, DIFF_HEAD_DIM // 2, axis=1))
            o_ref[:, h * LANES:(h + 1) * LANES] = ((x * cos + partner * sin) * mult).astype(o_ref.dtype)
    d = 2 * DIFF_HEAD_DIM
    for h in range(DIFF_HEADS):
        vt_ref[h, 0:d, :] = v_ref[:, h * d:(h + 1) * d].astype(F32).T.astype(vt_ref.dtype)
        vt_ref[h, d:, :] = jnp.ones((vt_ref.shape[1] - d, vt_ref.shape[2]), vt_ref.dtype)


def _rope(proj, cos_t, sin_t, *, bsz, seq, q_col_block, tm):
    m = proj.shape[0]
    tm = min(tm, seq)
    nb = seq // tm
    w = DIFF_HEADS * LANES
    v_rows = 2 * DIFF_HEAD_DIM + BF16_SUBLANES
    q_scale = DIFF_HEAD_DIM ** -0.5 * math.log2(math.e)
    return pl.pallas_call(
        functools.partial(_rope_kernel, q_scale=q_scale),
        grid=(m // tm,),
        in_specs=[pl.BlockSpec((tm, w), lambda i: (i, q_col_block)),
                  pl.BlockSpec((tm, w), lambda i: (i, q_col_block + 1)),
                  pl.BlockSpec((tm, w), lambda i: (i, q_col_block + 2)),
                  pl.BlockSpec((tm, LANES), lambda i: (i, 0)),
                  pl.BlockSpec((tm, LANES), lambda i: (i, 0))],
        out_specs=[pl.BlockSpec((tm, w), lambda i: (i, 0)),
                   pl.BlockSpec((tm, w), lambda i: (i, 0)),
                   pl.BlockSpec((DIFF_HEADS, v_rows, tm), lambda i: (i // nb, 0, i % nb))],
        out_shape=[jax.ShapeDtypeStruct((m, w), BF16),
                   jax.ShapeDtypeStruct((m, w), BF16),
                   jax.ShapeDtypeStruct((bsz * DIFF_HEADS, v_rows, seq), BF16)],
        compiler_params=_cparams("parallel"),
        name="rope",
    )(proj, proj, proj, cos_t, sin_t)


def _diff_attn_kernel(q_ref, k_ref, vt_ref, g_ref, lam_ref, sg_ref, o_ref, acc1, acc2, s_a, s_b, *, tk):
    qi = pl.program_id(2)
    tq = q_ref.shape[0]
    diag_blocks = tq // tk
    assert diag_blocks % 2 == 0
    q = q_ref[...]
    lane = lax.broadcasted_iota(jnp.int32, q.shape, 1)
    q_maps = (jnp.where(lane < DIFF_HEAD_DIM, q, jnp.zeros_like(q)),
              jnp.where(lane >= DIFF_HEAD_DIM, q, jnp.zeros_like(q)))
    accs = (acc1, acc2)
    acc1[...] = jnp.zeros_like(acc1)
    acc2[...] = jnp.zeros_like(acc2)

    def scores_into(s_ref, j):
        k = k_ref[pl.ds(pl.multiple_of(j * tk, tk), tk), :]
        for mi, qm in enumerate(q_maps):
            s_ref[mi] = lax.dot_general(k, qm, (((1,), (1,)), ((), ())),
                                        preferred_element_type=F32)

    def consume(s_ref, j, ms, key_offset=None):
        vt = vt_ref[0, :, pl.ds(pl.multiple_of(j * tk, tk), tk)]
        out = []
        for mi, (m_old, acc) in enumerate(zip(ms, accs)):
            s = s_ref[mi]
            if key_offset is not None:
                key = lax.broadcasted_iota(jnp.int32, s.shape, 0) + key_offset
                qry = lax.broadcasted_iota(jnp.int32, s.shape, 1)
                s = jnp.where(key <= qry, s, MASK_VALUE)
            m_new = jnp.maximum(m_old, jnp.max(s, axis=0, keepdims=True))
            p = jnp.exp2(s - m_new)
            acc[...] = acc[...] * jnp.exp2(m_old - m_new) + jnp.dot(vt, p.astype(BF16),
                                                                    preferred_element_type=F32)
            out.append(m_new)
        return tuple(out)

    def finish():
        d = 2 * DIFF_HEAD_DIM
        a1, a2 = acc1[...], acc2[...]
        o_t = a1[:d] / a1[d:d + 1] - lam_ref[0] * (a2[:d] / a2[d:d + 1])
        o = o_t.T
        var = jnp.mean(o * o, axis=-1, keepdims=True)
        o = o * lax.rsqrt(var + NORM_EPS) * sg_ref[...]
        o_ref[...] = (o * jax.nn.silu(g_ref[...].astype(F32))).astype(o_ref.dtype)

    scores_into(s_a, 0)

    def two_blocks(i, ms):
        scores_into(s_b, 2 * i + 1)
        ms = consume(s_a, 2 * i, ms)
        scores_into(s_a, 2 * i + 2)
        return consume(s_b, 2 * i + 1, ms)

    m0 = jnp.full((1, tq), MASK_VALUE, F32)
    first_diag = qi * diag_blocks
    ms = lax.fori_loop(0, first_diag // 2, two_blocks, (m0, m0))
    for d in range(diag_blocks):
        cur, nxt = (s_a, s_b) if d % 2 == 0 else (s_b, s_a)
        if d + 1 < diag_blocks:
            scores_into(nxt, first_diag + d + 1)
        ms = consume(cur, first_diag + d, ms, key_offset=d * tk)
    finish()


def _diff_attn(q_rot, k_rot, vt_ext, proj, lam, sg_vec, *, bsz, seq, g_col, tq):
    tq = min(tq, seq)
    tk = tq // 2
    nq = seq // tq
    v_rows = vt_ext.shape[1]
    return pl.pallas_call(
        functools.partial(_diff_attn_kernel, tk=tk),
        grid=(bsz, DIFF_HEADS, nq),
        in_specs=[
            pl.BlockSpec((tq, LANES), lambda b, h, i: (b * nq + i, h)),
            pl.BlockSpec((seq, LANES), lambda b, h, i: (b, h)),
            pl.BlockSpec((1, v_rows, seq), lambda b, h, i: (b * DIFF_HEADS + h, 0, 0)),
            pl.BlockSpec((tq, LANES), lambda b, h, i: (b * nq + i, g_col + h)),
            pl.BlockSpec(memory_space=pltpu.SMEM),
            pl.BlockSpec((1, LANES), lambda b, h, i: (0, 0)),
        ],
        out_specs=pl.BlockSpec((tq, LANES), lambda b, h, i: (b * nq + i, h)),
        out_shape=jax.ShapeDtypeStruct((bsz * seq, DIFF_HEADS * LANES), BF16),
        scratch_shapes=[pltpu.VMEM((v_rows, tq), F32), pltpu.VMEM((v_rows, tq), F32),
                        pltpu.VMEM((2, tk, tq), F32), pltpu.VMEM((2, tk, tq), F32)],
        compiler_params=_cparams("parallel", "parallel", "arbitrary"),
        name="diff_attn",
    )(q_rot, k_rot, vt_ext, proj, lam, sg_vec)


def _mem_attn_kernel(q_ref, g_ref, mk_ref, mvt_ref, o_ref):
    scale = XATTN_HEAD_DIM ** -0.5
    for h in range(XATTN_HEADS):
        sl = slice(h * XATTN_HEAD_DIM, (h + 1) * XATTN_HEAD_DIM)
        s = lax.dot_general(mk_ref[:, sl], q_ref[:, sl], (((1,), (1,)), ((), ())),
                            preferred_element_type=F32) * scale
        p = jnp.exp(s - jnp.max(s, axis=0, keepdims=True))
        l = jnp.sum(p, axis=0, keepdims=True)
        o_t = jnp.dot(mvt_ref[sl, :], p.astype(BF16), preferred_element_type=F32) / l
        o_ref[:, sl] = (o_t.T * jax.nn.silu(g_ref[:, sl].astype(F32))).astype(o_ref.dtype)


def _mem_attn(proj, mem_kv, *, bsz, seq, mem_tokens, q_col_block, g_col_block, tq):
    tq = min(tq, seq)
    nq = seq // tq
    w = XATTN_HEADS * XATTN_HEAD_DIM
    mv_t = mem_kv[:, w:].reshape(bsz, mem_tokens, w).transpose(0, 2, 1)
    return pl.pallas_call(
        _mem_attn_kernel,
        grid=(bsz, nq),
        in_specs=[pl.BlockSpec((tq, w), lambda b, i: (b * nq + i, q_col_block)),
                  pl.BlockSpec((tq, w), lambda b, i: (b * nq + i, g_col_block)),
                  pl.BlockSpec((mem_tokens, w), lambda b, i: (b, 0)),
                  pl.BlockSpec((None, w, mem_tokens), lambda b, i: (b, 0, 0))],
        out_specs=pl.BlockSpec((tq, w), lambda b, i: (b * nq + i, 0)),
        out_shape=jax.ShapeDtypeStruct((bsz * seq, w), BF16),
        compiler_params=_cparams("parallel", "parallel"),
        name="mem_attn",
    )(proj, proj, mem_kv, mv_t)


def _s5_taps_kernel(bt_ref, ca_ref, d_ref, o_ref):
    for s in range(bt_ref.shape[0]):
        k = lax.dot_general(bt_ref[s], ca_ref[s], (((1,), (1,)), ((), ())),
                            preferred_element_type=F32, precision=lax.Precision.HIGHEST)
        row = lax.broadcasted_iota(jnp.int32, k.shape, 0)
        col = lax.broadcasted_iota(jnp.int32, k.shape, 1)
        o_ref[s] = k + jnp.where(row == col, d_ref[s], 0.0)


def _s5_taps(bt, ca, d_pad):
    g, h, k = bt.shape
    n = ca.shape[1]
    gb = GROUPS_PER_TILE
    return pl.pallas_call(
        _s5_taps_kernel,
        grid=(g // gb,),
        in_specs=[pl.BlockSpec((gb, h, k), lambda i: (i, 0, 0)),
                  pl.BlockSpec((gb, n, k), lambda i: (i, 0, 0)),
                  pl.BlockSpec((gb, 1, n), lambda i: (i, 0, 0))],
        out_specs=pl.BlockSpec((gb, h, n), lambda i: (i, 0, 0)),
        out_shape=jax.ShapeDtypeStruct((g, h, n), F32),
        compiler_params=_cparams("parallel"),
        name="s5_taps",
    )(bt, ca, d_pad)


def _s5_weights(a_re, a_im, log_dt, b_re, b_im, c_re, c_im, d_skip):
    g = a_re.shape[0]
    p, h, t = SSM_STATE, SSM_GROUP, CHUNK
    dt = jnp.exp(log_dt.astype(F32))[:, None]
    lr, li = a_re.astype(F32), a_im.astype(F32)
    mag = jnp.exp(lr * dt)
    abar_re, abar_im = mag * jnp.cos(li * dt), mag * jnp.sin(li * dt)
    den = lr * lr + li * li
    nr, ni = abar_re - 1.0, abar_im
    z_re = ((nr * lr + ni * li) / den)[:, None, :]
    z_im = ((ni * lr - nr * li) / den)[:, None, :]
    br, bi = b_re.astype(F32).transpose(0, 2, 1), b_im.astype(F32).transpose(0, 2, 1)
    bt_re = z_re * br - z_im * bi
    bt_im = z_re * bi + z_im * br
    cr, ci = c_re.astype(F32), c_im.astype(F32)

    tau = jnp.arange(t + 1, dtype=F32)[None, :, None]
    pmag = jnp.exp(tau * (lr * dt)[:, None, :])
    pw_re = pmag * jnp.cos(tau * (li * dt)[:, None, :])
    pw_im = pmag * jnp.sin(tau * (li * dt)[:, None, :])

    ca_re = cr[:, None] * pw_re[:, :, None, :] - ci[:, None] * pw_im[:, :, None, :]
    ca_im = cr[:, None] * pw_im[:, :, None, :] + ci[:, None] * pw_re[:, :, None, :]

    ca = jnp.concatenate([ca_re[:, :t], ca_im[:, :t]], axis=-1).reshape(g, t * h, 2 * p)
    bt = jnp.concatenate([bt_re, -bt_im], axis=-1)
    d_pad = jnp.pad(d_skip.astype(F32), ((0, 0), (0, (t - 1) * h)))[:, None, :]
    taps_t = _s5_taps(bt, ca, d_pad)

    gt = GROUPS_PER_TILE
    nt = g // gt
    idx = jnp.arange(t)

    k_cat = taps_t.reshape(nt, gt * h, t * h)

    pj_re, pj_im = pw_re[:, t - 1 - idx][:, :, None, :], pw_im[:, t - 1 - idx][:, :, None, :]
    win = jnp.concatenate([pj_re * bt_re[:, None] - pj_im * bt_im[:, None],
                           pj_re * bt_im[:, None] + pj_im * bt_re[:, None]], axis=-1)
    win_cat = win.astype(BF16).reshape(nt, gt, t, h, 2 * p).transpose(0, 2, 1, 3, 4)
    win_cat = win_cat.reshape(nt, t * gt * h, 2 * p)

    wo = jnp.stack([ca_re[:, 1:].reshape(g, t * h, p), -ca_im[:, 1:].reshape(g, t * h, p)], axis=1)
    wo_cat = wo.astype(BF16).transpose(0, 1, 3, 2).reshape(nt, gt, 2, p, t * h).transpose(0, 2, 1, 3, 4)
    wo_cat = wo_cat.reshape(nt, 2 * gt * p, t * h)

    kk = (t * jnp.arange(2 * SUBLANES, dtype=F32))[:, None, None]
    cmag = jnp.exp(kk * (lr * dt)[None])
    a_chunk_re = (cmag * jnp.cos(kk * (li * dt)[None])).reshape(2 * SUBLANES, g * p)
    a_chunk_im = (cmag * jnp.sin(kk * (li * dt)[None])).reshape(2 * SUBLANES, g * p)
    return k_cat.astype(BF16), win_cat, wo_cat, a_chunk_re, a_chunk_im


def _lane_repeat(blocks, width, copies):
    r = np.kron(np.eye(blocks, dtype=np.float32),
                np.kron(np.ones((1, copies), np.float32), np.eye(width, dtype=np.float32)))
    return jnp.asarray(r, BF16)


def _group_of(index, period, size):
    assert period & (period - 1) == 0 and size & (size - 1) == 0
    return lax.shift_right_logical(index & (period - 1), size.bit_length() - 1)


def _expand_block_diag(compact, rep, row_group, col_group):
    e = jnp.dot(compact, rep, preferred_element_type=F32)
    r = lax.broadcasted_iota(jnp.int32, e.shape, 0)
    c = lax.broadcasted_iota(jnp.int32, e.shape, 1)
    return jnp.where(row_group(r) == col_group(c), e, 0.0).astype(BF16)


def _chunk_lhs(u_ref):
    return jnp.concatenate([u_ref[j] for j in range(CHUNK)], axis=1)


def _s5_state_in_kernel(u_ref, wc_ref, rep_ref, sre_ref, sim_ref, w_scr):
    @pl.when(pl.program_id(1) == 0)
    def _():
        w_scr[...] = _expand_block_diag(
            wc_ref[0], rep_ref[...],
            lambda r: _group_of(r, LANES, SSM_GROUP),
            lambda c: _group_of(c, GROUPS_PER_TILE * SSM_STATE, SSM_STATE))

    s = jnp.dot(_chunk_lhs(u_ref), w_scr[...], preferred_element_type=F32)
    half = s.shape[1] // 2
    sre_ref[...] = s[:, :half]
    sim_ref[...] = s[:, half:]


def _s5_state_in(us3, win_cat, layer, *, rb):
    t, r, width = us3.shape
    nt = width // (2 * LANES)
    _, k, nc = win_cat.shape
    n = nc * GROUPS_PER_TILE
    rb = min(rb, r)
    rep = _lane_repeat(2, SSM_STATE, GROUPS_PER_TILE)
    return pl.pallas_call(
        _s5_state_in_kernel,
        grid=(nt, r // rb),
        in_specs=[pl.BlockSpec((t, rb, LANES), lambda q, i: (0, i, q)),
                  pl.BlockSpec((1, k, nc), lambda q, i: (layer * nt + q, 0, 0)),
                  pl.BlockSpec(rep.shape, lambda q, i: (0, 0))],
        out_specs=[pl.BlockSpec((rb, n // 2), lambda q, i: (i, q)),
                   pl.BlockSpec((rb, n // 2), lambda q, i: (i, q))],
        out_shape=[jax.ShapeDtypeStruct((r, nt * n // 2), F32)] * 2,
        scratch_shapes=[pltpu.VMEM((k, n), BF16)],
        compiler_params=_cparams("parallel", "arbitrary"),
        name="s5_state_in",
    )(us3, win_cat, rep)


def _s5_recurrence_kernel(sre_ref, sim_ref, pre_ref, pim_ref, hre_ref, him_ref, *, bsz, n_chunks):
    tn = sre_ref.shape[1]
    shape = (SUBLANES, tn)
    row = lax.broadcasted_iota(jnp.int32, shape, 0)
    pw_re = pre_ref[0:SUBLANES, :]
    pw_im = pim_ref[0:SUBLANES, :]
    a_tile_re = jnp.broadcast_to(pre_ref[SUBLANES:SUBLANES + 1, :], shape)
    a_tile_im = jnp.broadcast_to(pim_ref[SUBLANES:SUBLANES + 1, :], shape)

    def tile_step(it, carry):
        out = []
        for b in range(bsz):
            h_re, h_im = carry[2 * b], carry[2 * b + 1]
            rows = pl.ds(pl.multiple_of(b * n_chunks + it * SUBLANES, SUBLANES), SUBLANES)
            y_re, y_im = sre_ref[rows, :], sim_ref[rows, :]
            for d in (1, 2, 4):
                m_re = jnp.broadcast_to(pre_ref[d:d + 1, :], shape)
                m_im = jnp.broadcast_to(pim_ref[d:d + 1, :], shape)
                s_re = jnp.where(row >= d, pltpu.roll(y_re, d, axis=0), 0.0)
                s_im = jnp.where(row >= d, pltpu.roll(y_im, d, axis=0), 0.0)
                y_re, y_im = (y_re + m_re * s_re - m_im * s_im,
                              y_im + m_re * s_im + m_im * s_re)
            e_re = jnp.where(row >= 1, pltpu.roll(y_re, 1, axis=0), 0.0)
            e_im = jnp.where(row >= 1, pltpu.roll(y_im, 1, axis=0), 0.0)
            hre_ref[rows, :] = pw_re * h_re - pw_im * h_im + e_re
            him_ref[rows, :] = pw_re * h_im + pw_im * h_re + e_im
            last_re = jnp.broadcast_to(y_re[SUBLANES - 1:SUBLANES, :], shape)
            last_im = jnp.broadcast_to(y_im[SUBLANES - 1:SUBLANES, :], shape)
            out.append(a_tile_re * h_re - a_tile_im * h_im + last_re)
            out.append(a_tile_re * h_im + a_tile_im * h_re + last_im)
        return tuple(out)

    zero = jnp.zeros(shape, F32)
    lax.fori_loop(0, n_chunks // SUBLANES, tile_step, (zero,) * (2 * bsz))


def _s5_recurrence(s_re, s_im, pw_re, pw_im, layer, *, bsz, tn):
    r, n = s_re.shape
    tn = min(tn, n)
    spec = pl.BlockSpec((r, tn), lambda j: (0, j))
    pspec = pl.BlockSpec((pw_re.shape[0], tn), lambda j: (0, layer * (n // tn) + j))
    return pl.pallas_call(
        functools.partial(_s5_recurrence_kernel, bsz=bsz, n_chunks=r // bsz),
        grid=(n // tn,),
        in_specs=[spec, spec, pspec, pspec],
        out_specs=[spec, spec],
        out_shape=[jax.ShapeDtypeStruct((r, n), F32)] * 2,
        compiler_params=_cparams("parallel"),
        name="s5_recurrence",
    )(s_re, s_im, pw_re, pw_im)


def _s5_out_kernel(u_ref, kc_ref, hre_ref, him_ref, wc_ref, rep_ref, y_ref, t_scr, w_scr):
    @pl.when(pl.program_id(1) == 0)
    def _():
        rep = rep_ref[...]
        col_group = lambda c: _group_of(c, LANES, SSM_GROUP)
        taps = _expand_block_diag(kc_ref[0], rep, lambda r: _group_of(r, LANES, SSM_GROUP), col_group)
        t_scr[...] = jnp.zeros_like(t_scr)
        for j in range(CHUNK):
            t_scr[j * LANES:(j + 1) * LANES, j * LANES:] = taps[:, :(CHUNK - j) * LANES]
        w_scr[...] = _expand_block_diag(
            wc_ref[0], rep, lambda r: _group_of(r, GROUPS_PER_TILE * SSM_STATE, SSM_STATE), col_group)

    h = jnp.concatenate([hre_ref[...], him_ref[...]], axis=1).astype(BF16)
    lhs = _chunk_lhs(u_ref)
    band = CHUNK // S5_OUT_BANDS
    for c in range(S5_OUT_BANDS):
        cols = slice(c * band * LANES, (c + 1) * band * LANES)
        y = jnp.dot(lhs[:, :(c + 1) * band * LANES], t_scr[:(c + 1) * band * LANES, cols],
                    preferred_element_type=F32)
        y += jnp.dot(h, w_scr[:, cols], preferred_element_type=F32)
        for i in range(band):
            y_ref[c * band + i] = y[:, i * LANES:(i + 1) * LANES].astype(y_ref.dtype)


def _s5_out(us3, k_cat, h_re, h_im, wo_cat, layer, *, rb):
    t, r, width = us3.shape
    nt = width // (2 * LANES)
    _, ks, kc = wo_cat.shape
    k = t * LANES
    rb = min(rb, r)
    rep = _lane_repeat(t, SSM_GROUP, GROUPS_PER_TILE)
    return pl.pallas_call(
        _s5_out_kernel,
        grid=(nt, r // rb),
        in_specs=[pl.BlockSpec((t, rb, LANES), lambda q, i: (0, i, q)),
                  pl.BlockSpec((1, LANES, kc), lambda q, i: (layer * nt + q, 0, 0)),
                  pl.BlockSpec((rb, ks // 2), lambda q, i: (i, q)),
                  pl.BlockSpec((rb, ks // 2), lambda q, i: (i, q)),
                  pl.BlockSpec((1, ks, kc), lambda q, i: (layer * nt + q, 0, 0)),
                  pl.BlockSpec(rep.shape, lambda q, i: (0, 0))],
        out_specs=pl.BlockSpec((t, rb, LANES), lambda q, i: (0, i, q)),
        out_shape=jax.ShapeDtypeStruct((t, r, nt * LANES), BF16),
        scratch_shapes=[pltpu.VMEM((k, k), BF16), pltpu.VMEM((ks, k), BF16)],
        compiler_params=_cparams("parallel", "arbitrary"),
        name="s5_out",
    )(us3, k_cat, h_re, h_im, wo_cat, rep)


def _s5_glu_kernel(y_ref, gate_ref, w_ref, b_ref, pt_ref, o_ref):
    t, rb, w = y_ref.shape
    y = jax.nn.gelu(y_ref[...].reshape(t * rb, w).astype(F32))
    z = jnp.dot(y.astype(BF16), w_ref[...], preferred_element_type=F32) + b_ref[...]
    y = y * jax.nn.sigmoid(z)
    y = (y * jax.nn.silu(gate_ref[...].reshape(t * rb, w).astype(F32))).astype(BF16)
    o_ref[...] = jnp.dot(pt_ref[...], y, preferred_element_type=F32).astype(o_ref.dtype)


def _s5_glu(y3, us3, w_glu, b_glu, *, rb):
    t, r, w = y3.shape
    rb = min(rb, r)
    rows = t * rb
    return pl.pallas_call(
        _s5_glu_kernel,
        grid=(r // rb,),
        in_specs=[pl.BlockSpec((t, rb, w), lambda i: (0, i, 0)),
                  pl.BlockSpec((t, rb, w), lambda i: (0, i, 1)),
                  pl.BlockSpec((w, w), lambda i: (0, 0)),
                  pl.BlockSpec((1, w), lambda i: (0, 0)),
                  pl.BlockSpec((rows, rows), lambda i: (0, 0))],
        out_specs=pl.BlockSpec((rows, w), lambda i: (i, 0)),
        out_shape=jax.ShapeDtypeStruct((t * r, w), BF16),
        compiler_params=_cparams("parallel"),
        name="s5_glu",
    )(y3, us3, w_glu, b_glu.reshape(1, w), _chunk_major_perm(rows).T)


def _out_proj_kernel(ys_ref, yd_ref, yx_ref, w_ref, x_ref, g_ref, o_ref):
    ws, wd = ys_ref.shape[1], yd_ref.shape[1]
    mix = jnp.dot(ys_ref[...], w_ref[0:ws, :], preferred_element_type=F32)
    mix += jnp.dot(yd_ref[...], w_ref[ws:ws + wd, :], preferred_element_type=F32)
    mix += jnp.dot(yx_ref[...], w_ref[ws + wd:, :], preferred_element_type=F32)
    var = jnp.mean(mix * mix, axis=-1, keepdims=True)
    o_ref[...] = x_ref[...] + mix * lax.rsqrt(var + NORM_EPS) * g_ref[...]


def _out_proj(y_s, y_d, y_x, w_out, layer, x, g_post, *, tm):
    m, d = x.shape
    tm = min(tm, m)
    row = lambda i: (i, 0)
    const = lambda i: (0, 0)
    return pl.pallas_call(
        _out_proj_kernel,
        grid=(m // tm,),
        in_specs=[pl.BlockSpec((tm, y_s.shape[1]), row),
                  pl.BlockSpec((tm, y_d.shape[1]), row),
                  pl.BlockSpec((tm, y_x.shape[1]), row),
                  pl.BlockSpec((None,) + w_out.shape[1:], lambda i: (layer, 0, 0)),
                  pl.BlockSpec((tm, d), row),
                  pl.BlockSpec((1, d), const)],
        out_specs=pl.BlockSpec((tm, d), row),
        out_shape=jax.ShapeDtypeStruct((m, d), F32),
        compiler_params=_cparams("parallel"),
        name="out_proj",
    )(y_s, y_d, y_x, w_out, x, g_post.reshape(1, d))


def kernel(x, mem, positions, norm_pre, norm_post, norm_mem, w_in, w_out, w_mem_kv, ssm_a_re, ssm_a_im, ssm_log_dt, ssm_b_re, ssm_b_im, ssm_c_re, ssm_c_im, ssm_d, w_glu, b_glu, diff_lq1, diff_lk1, diff_lq2, diff_lk2, diff_subln):
    bsz, seq, d_model = x.shape
    mem_tokens = mem.shape[1]
    depth = w_in.shape[0]
    m = bsz * seq
    ssm_width = ssm_a_re.shape[1] * SSM_GROUP
    n_groups = ssm_width // SSM_GROUP
    diff_width = DIFF_HEADS * 2 * DIFF_HEAD_DIM
    xattn_width = XATTN_HEADS * XATTN_HEAD_DIM
    n_chunks = seq // CHUNK
    n_s5_cols = 2 * ssm_width
    col_qd = 0
    col_kd = col_qd + diff_width
    col_vd = col_kd + diff_width
    col_gd = col_vd + diff_width
    col_qx = col_gd + diff_width
    col_gx = col_qx + xattn_width
    assert col_kd == col_qd + diff_width and col_qd % diff_width == 0

    inv = ROPE_THETA ** (-jnp.arange(0, DIFF_HEAD_DIM, 2, dtype=F32) / DIFF_HEAD_DIM)
    half = DIFF_HEAD_DIM // 2
    ang = positions.astype(F32).reshape(m, 1) * jnp.tile(inv, LANES // half)
    cos_t = jnp.cos(ang)
    sin_t = jnp.sin(ang) * jnp.tile(jnp.concatenate([-jnp.ones(half, F32), jnp.ones(half, F32)]),
                                    LANES // DIFF_HEAD_DIM)

    xf = x.reshape(m, d_model)
    mem_f = mem.reshape(bsz * mem_tokens, d_model)
    w_in_bf, w_out_bf = w_in.astype(BF16), w_out.astype(BF16)
    all_groups = lambda a: a.reshape((depth * n_groups,) + a.shape[2:])
    k_cat, win_cat, wo_cat, a_re, a_im = _s5_weights(
        *(all_groups(a) for a in (ssm_a_re, ssm_a_im, ssm_log_dt, ssm_b_re, ssm_b_im,
                                  ssm_c_re, ssm_c_im, ssm_d)))
    for l in range(depth):
        lambda_init = 0.8 - 0.6 * math.exp(-0.3 * l)
        us3, proj = _in_proj(xf, norm_pre[l], w_in_bf, l, n_perm_cols=n_s5_cols, tm=1024, tn=512,
                             perm_rows=512)

        s_re, s_im = _s5_state_in(us3, win_cat, l, rb=512)
        h_re, h_im = _s5_recurrence(s_re, s_im, a_re, a_im, l, bsz=bsz, tn=1024)
        y3 = _s5_out(us3, k_cat, h_re, h_im, wo_cat, l, rb=512)
        y_s = _s5_glu(y3, us3, w_glu[l].astype(BF16), b_glu[l], rb=64)

        assert col_vd == col_kd + diff_width
        q_rot, k_rot, vt_ext = _rope(proj, cos_t, sin_t, bsz=bsz, seq=seq,
                                     q_col_block=col_qd // diff_width, tm=1024)
        lam = (jnp.exp(jnp.sum(diff_lq1[l].astype(F32) * diff_lk1[l].astype(F32)))
               - jnp.exp(jnp.sum(diff_lq2[l].astype(F32) * diff_lk2[l].astype(F32))) + lambda_init)
        sg_vec = (diff_subln[l].astype(F32) * (1.0 - lambda_init)).reshape(1, LANES)
        y_d = _diff_attn(q_rot, k_rot, vt_ext, proj, lam.reshape(1), sg_vec, bsz=bsz, seq=seq,
                         g_col=col_gd // LANES, tq=1024)

        mem_kv = _norm_matmul(mem_f, norm_mem[l], w_mem_kv[l].astype(BF16), tm=512, tn=1024, name="mem_kv")
        y_x = _mem_attn(proj, mem_kv, bsz=bsz, seq=seq, mem_tokens=mem_tokens,
                        q_col_block=col_qx // xattn_width, g_col_block=col_gx // xattn_width, tq=1024)

        xf = _out_proj(y_s, y_d, y_x, w_out_bf, l, xf, norm_post[l], tm=512)
    return xf.reshape(bsz, seq, d_model)
```

```python
import functools
import math

import jax
import jax.numpy as jnp
import numpy as np
from jax import lax
from jax.experimental import pallas as pl
from jax.experimental.pallas import tpu as pltpu

F32 = jnp.float32
BF16 = jnp.bfloat16

SSM_GROUP = 16
SSM_STATE = 64
CHUNK = 16
DIFF_HEADS = 4
DIFF_HEAD_DIM = 64
XATTN_HEADS = 4
XATTN_HEAD_DIM = 128
ROPE_THETA = 10000.0
NORM_EPS = 1e-6
MASK_VALUE = -1e30
LANES = 128
SUBLANES = 8
BF16_SUBLANES = 16
GROUPS_PER_TILE = LANES // SSM_GROUP
S5_OUT_BANDS = 8
VMEM_LIMIT = 56 * 1024 * 1024


def _cparams(*sem):
    return pltpu.CompilerParams(dimension_semantics=sem, vmem_limit_bytes=VMEM_LIMIT)


def _norm_matmul_kernel(x_ref, g_ref, w_ref, o_ref, h_ref):
    @pl.when(pl.program_id(1) == 0)
    def _():
        x = x_ref[...]
        var = jnp.mean(x * x, axis=-1, keepdims=True)
        h_ref[...] = (x * lax.rsqrt(var + NORM_EPS) * g_ref[...]).astype(BF16)

    o_ref[...] = jnp.dot(h_ref[...], w_ref[...], preferred_element_type=F32).astype(o_ref.dtype)


def _norm_matmul(x, g, w, *, tm, tn, name):
    m, d = x.shape
    n = w.shape[1]
    tm, tn = min(tm, m), min(tn, n)
    return pl.pallas_call(
        _norm_matmul_kernel,
        grid=(m // tm, n // tn),
        in_specs=[pl.BlockSpec((tm, d), lambda i, j: (i, 0)),
                  pl.BlockSpec((1, d), lambda i, j: (0, 0)),
                  pl.BlockSpec((d, tn), lambda i, j: (0, j))],
        out_specs=pl.BlockSpec((tm, tn), lambda i, j: (i, j)),
        out_shape=jax.ShapeDtypeStruct((m, n), BF16),
        scratch_shapes=[pltpu.VMEM((tm, d), BF16)],
        compiler_params=_cparams("parallel", "arbitrary"),
        name=name,
    )(x, g.reshape(1, d), w)


def _chunk_major_perm(rows):
    rb = rows // CHUNK
    src = (np.arange(rows) % rb) * CHUNK + np.arange(rows) // rb
    return jnp.asarray(np.eye(rows, dtype=np.float32)[src], BF16)


def _in_proj_kernel(x_ref, g_ref, w_ref, p_ref, o3_ref, o_ref, hp_ref, hn_ref, *, n_perm):
    j = pl.program_id(1)
    tm = x_ref.shape[0]
    pr = p_ref.shape[0]
    rb = pr // CHUNK

    @pl.when(j == 0)
    def _():
        for s in range(tm // pr):
            x = x_ref[s * pr:(s + 1) * pr, :]
            var = jnp.mean(x * x, axis=-1, keepdims=True)
            hn = (x * lax.rsqrt(var + NORM_EPS) * g_ref[...]).astype(BF16)
            hn_ref[s * pr:(s + 1) * pr, :] = hn
            hp_ref[s * pr:(s + 1) * pr, :] = jnp.dot(p_ref[...], hn, preferred_element_type=F32).astype(BF16)

    @pl.when(j < n_perm)
    def _():
        res = jnp.dot(hp_ref[...], w_ref[...], preferred_element_type=F32)
        for s in range(tm // pr):
            for i in range(CHUNK):
                rows = slice(s * pr + i * rb, s * pr + (i + 1) * rb)
                o3_ref[i, s * rb:(s + 1) * rb, :] = res[rows].astype(o3_ref.dtype)

    @pl.when(j >= n_perm)
    def _():
        o_ref[...] = jnp.dot(hn_ref[...], w_ref[...], preferred_element_type=F32).astype(o_ref.dtype)


def _in_proj(x, g, w, layer, *, n_perm_cols, tm, tn, perm_rows):
    m, d = x.shape
    n = w.shape[2]
    tm = min(tm, m)
    perm_rows = min(perm_rows, tm)
    n_perm = n_perm_cols // tn
    return pl.pallas_call(
        functools.partial(_in_proj_kernel, n_perm=n_perm),
        grid=(m // tm, n // tn),
        in_specs=[pl.BlockSpec((tm, d), lambda i, j: (i, 0)),
                  pl.BlockSpec((1, d), lambda i, j: (0, 0)),
                  pl.BlockSpec((None, d, tn), lambda i, j: (layer, 0, j)),
                  pl.BlockSpec((perm_rows, perm_rows), lambda i, j: (0, 0))],
        out_specs=[pl.BlockSpec((CHUNK, tm // CHUNK, tn), lambda i, j: (0, i, jnp.minimum(j, n_perm - 1))),
                   pl.BlockSpec((tm, tn), lambda i, j: (i, jnp.maximum(j - n_perm, 0)))],
        out_shape=[jax.ShapeDtypeStruct((CHUNK, m // CHUNK, n_perm_cols), BF16),
                   jax.ShapeDtypeStruct((m, n - n_perm_cols), BF16)],
        scratch_shapes=[pltpu.VMEM((tm, d), BF16), pltpu.VMEM((tm, d), BF16)],
        compiler_params=_cparams("parallel", "arbitrary"),
        name="in_proj",
    )(x, g.reshape(1, d), w, _chunk_major_perm(perm_rows))


def _rope_kernel(q_ref, k_ref, v_ref, cos_ref, sin_ref, qo_ref, ko_ref, vt_ref, *, q_scale):
    cos = cos_ref[...]
    sin = sin_ref[...]
    lane = lax.broadcasted_iota(jnp.int32, cos.shape, 1)
    first_half = (lane % DIFF_HEAD_DIM) < (DIFF_HEAD_DIM // 2)
    for x_ref, o_ref, mult in ((q_ref, qo_ref, q_scale), (k_ref, ko_ref, 1.0)):
        for h in range(DIFF_HEADS):
            x = x_ref[:, h * LANES:(h + 1) * LANES].astype(F32)
            partner = jnp.where(first_half,
                                pltpu.roll(x, LANES - DIFF_HEAD_DIM // 2, axis=1),
                                pltpu.roll(x, DIFF_HEAD_DIM // 2, axis=1))
            o_ref[:, h * LANES:(h + 1) * LANES] = ((x * cos + partner * sin) * mult).astype(o_ref.dtype)
    d = 2 * DIFF_HEAD_DIM
    for h in range(DIFF_HEADS):
        vt_ref[h, 0:d, :] = v_ref[:, h * d:(h + 1) * d].astype(F32).T.astype(vt_ref.dtype)
        vt_ref[h, d:, :] = jnp.ones((vt_ref.shape[1] - d, vt_ref.shape[2]), vt_ref.dtype)


def _rope(proj, cos_t, sin_t, *, bsz, seq, q_col_block, tm):
    m = proj.shape[0]
    tm = min(tm, seq)
    nb = seq // tm
    w = DIFF_HEADS * LANES
    v_rows = 2 * DIFF_HEAD_DIM + BF16_SUBLANES
    q_scale = DIFF_HEAD_DIM ** -0.5 * math.log2(math.e)
    return pl.pallas_call(
        functools.partial(_rope_kernel, q_scale=q_scale),
        grid=(m // tm,),
        in_specs=[pl.BlockSpec((tm, w), lambda i: (i, q_col_block)),
                  pl.BlockSpec((tm, w), lambda i: (i, q_col_block + 1)),
                  pl.BlockSpec((tm, w), lambda i: (i, q_col_block + 2)),
                  pl.BlockSpec((tm, LANES), lambda i: (i, 0)),
                  pl.BlockSpec((tm, LANES), lambda i: (i, 0))],
        out_specs=[pl.BlockSpec((tm, w), lambda i: (i, 0)),
                   pl.BlockSpec((tm, w), lambda i: (i, 0)),
                   pl.BlockSpec((DIFF_HEADS, v_rows, tm), lambda i: (i // nb, 0, i % nb))],
        out_shape=[jax.ShapeDtypeStruct((m, w), BF16),
                   jax.ShapeDtypeStruct((m, w), BF16),
                   jax.ShapeDtypeStruct((bsz * DIFF_HEADS, v_rows, seq), BF16)],
        compiler_params=_cparams("parallel"),
        name="rope",
    )(proj, proj, proj, cos_t, sin_t)


def _diff_attn_kernel(q_ref, k_ref, vt_ref, g_ref, lam_ref, sg_ref, o_ref, acc1, acc2, s_a, s_b, *, tk):
    qi = pl.program_id(2)
    tq = q_ref.shape[0]
    diag_blocks = tq // tk
    assert diag_blocks % 2 == 0
    q = q_ref[...]
    lane = lax.broadcasted_iota(jnp.int32, q.shape, 1)
    q_maps = (jnp.where(lane < DIFF_HEAD_DIM, q, jnp.zeros_like(q)),
              jnp.where(lane >= DIFF_HEAD_DIM, q, jnp.zeros_like(q)))
    accs = (acc1, acc2)
    acc1[...] = jnp.zeros_like(acc1)
    acc2[...] = jnp.zeros_like(acc2)

    def scores_into(s_ref, j):
        k = k_ref[pl.ds(pl.multiple_of(j * tk, tk), tk), :]
        for mi, qm in enumerate(q_maps):
            s_ref[mi] = lax.dot_general(k, qm, (((1,), (1,)), ((), ())),
                                        preferred_element_type=F32)

    def consume(s_ref, j, ms, key_offset=None):
        vt = vt_ref[0, :, pl.ds(pl.multiple_of(j * tk, tk), tk)]
        out = []
        for mi, (m_old, acc) in enumerate(zip(ms, accs)):
            s = s_ref[mi]
            if key_offset is not None:
                key = lax.broadcasted_iota(jnp.int32, s.shape, 0) + key_offset
                qry = lax.broadcasted_iota(jnp.int32, s.shape, 1)
                s = jnp.where(key <= qry, s, MASK_VALUE)
            m_new = jnp.maximum(m_old, jnp.max(s, axis=0, keepdims=True))
            p = jnp.exp2(s - m_new)
            acc[...] = acc[...] * jnp.exp2(m_old - m_new) + jnp.dot(vt, p.astype(BF16),
                                                                    preferred_element_type=F32)
            out.append(m_new)
        return tuple(out)

    def finish():
        d = 2 * DIFF_HEAD_DIM
        a1, a2 = acc1[...], acc2[...]
        o_t = a1[:d] / a1[d:d + 1] - lam_ref[0] * (a2[:d] / a2[d:d + 1])
        o = o_t.T
        var = jnp.mean(o * o, axis=-1, keepdims=True)
        o = o * lax.rsqrt(var + NORM_EPS) * sg_ref[...]
        o_ref[...] = (o * jax.nn.silu(g_ref[...].astype(F32))).astype(o_ref.dtype)

    scores_into(s_a, 0)

    def two_blocks(i, ms):
        scores_into(s_b, 2 * i + 1)
        ms = consume(s_a, 2 * i, ms)
        scores_into(s_a, 2 * i + 2)
        return consume(s_b, 2 * i + 1, ms)

    m0 = jnp.full((1, tq), MASK_VALUE, F32)
    first_diag = qi * diag_blocks
    ms = lax.fori_loop(0, first_diag // 2, two_blocks, (m0, m0))
    for d in range(diag_blocks):
        cur, nxt = (s_a, s_b) if d % 2 == 0 else (s_b, s_a)
        if d + 1 < diag_blocks:
            scores_into(nxt, first_diag + d + 1)
        ms = consume(cur, first_diag + d, ms, key_offset=d * tk)
    finish()


def _diff_attn(q_rot, k_rot, vt_ext, proj, lam, sg_vec, *, bsz, seq, g_col, tq):
    tq = min(tq, seq)
    tk = tq // 2
    nq = seq // tq
    v_rows = vt_ext.shape[1]
    return pl.pallas_call(
        functools.partial(_diff_attn_kernel, tk=tk),
        grid=(bsz, DIFF_HEADS, nq),
        in_specs=[
            pl.BlockSpec((tq, LANES), lambda b, h, i: (b * nq + i, h)),
            pl.BlockSpec((seq, LANES), lambda b, h, i: (b, h)),
            pl.BlockSpec((1, v_rows, seq), lambda b, h, i: (b * DIFF_HEADS + h, 0, 0)),
            pl.BlockSpec((tq, LANES), lambda b, h, i: (b * nq + i, g_col + h)),
            pl.BlockSpec(memory_space=pltpu.SMEM),
            pl.BlockSpec((1, LANES), lambda b, h, i: (0, 0)),
        ],
        out_specs=pl.BlockSpec((tq, LANES), lambda b, h, i: (b * nq + i, h)),
        out_shape=jax.ShapeDtypeStruct((bsz * seq, DIFF_HEADS * LANES), BF16),
        scratch_shapes=[pltpu.VMEM((v_rows, tq), F32), pltpu.VMEM((v_rows, tq), F32),
                        pltpu.VMEM((2, tk, tq), F32), pltpu.VMEM((2, tk, tq), F32)],
        compiler_params=_cparams("parallel", "parallel", "arbitrary"),
        name="diff_attn",
    )(q_rot, k_rot, vt_ext, proj, lam, sg_vec)


def _mem_attn_kernel(q_ref, g_ref, mk_ref, mv_ref, o_ref):
    scale = XATTN_HEAD_DIM ** -0.5
    for h in range(XATTN_HEADS):
        sl = slice(h * XATTN_HEAD_DIM, (h + 1) * XATTN_HEAD_DIM)
        s = lax.dot_general(q_ref[:, sl], mk_ref[:, sl], (((1,), (1,)), ((), ())),
                            preferred_element_type=F32) * scale
        p = jnp.exp(s - jnp.max(s, axis=-1, keepdims=True))
        p = p / jnp.sum(p, axis=-1, keepdims=True)
        o = jnp.dot(p.astype(BF16), mv_ref[:, sl], preferred_element_type=F32)
        o_ref[:, sl] = (o * jax.nn.silu(g_ref[:, sl].astype(F32))).astype(o_ref.dtype)


def _mem_attn(proj, mem_kv, *, bsz, seq, mem_tokens, q_col_block, g_col_block, tq):
    tq = min(tq, seq)
    nq = seq // tq
    w = XATTN_HEADS * XATTN_HEAD_DIM
    return pl.pallas_call(
        _mem_attn_kernel,
        grid=(bsz, nq),
        in_specs=[pl.BlockSpec((tq, w), lambda b, i: (b * nq + i, q_col_block)),
                  pl.BlockSpec((tq, w), lambda b, i: (b * nq + i, g_col_block)),
                  pl.BlockSpec((mem_tokens, w), lambda b, i: (b, 0)),
                  pl.BlockSpec((mem_tokens, w), lambda b, i: (b, 1))],
        out_specs=pl.BlockSpec((tq, w), lambda b, i: (b * nq + i, 0)),
        out_shape=jax.ShapeDtypeStruct((bsz * seq, w), BF16),
        compiler_params=_cparams("parallel", "parallel"),
        name="mem_attn",
    )(proj, proj, mem_kv, mem_kv)


def _s5_taps_kernel(bt_ref, ca_ref, d_ref, o_ref):
    for s in range(bt_ref.shape[0]):
        k = lax.dot_general(bt_ref[s], ca_ref[s], (((1,), (1,)), ((), ())),
                            preferred_element_type=F32, precision=lax.Precision.HIGHEST)
        row = lax.broadcasted_iota(jnp.int32, k.shape, 0)
        col = lax.broadcasted_iota(jnp.int32, k.shape, 1)
        o_ref[s] = k + jnp.where(row == col, d_ref[s], 0.0)


def _s5_taps(bt, ca, d_pad):
    g, h, k = bt.shape
    n = ca.shape[1]
    gb = GROUPS_PER_TILE
    return pl.pallas_call(
        _s5_taps_kernel,
        grid=(g // gb,),
        in_specs=[pl.BlockSpec((gb, h, k), lambda i: (i, 0, 0)),
                  pl.BlockSpec((gb, n, k), lambda i: (i, 0, 0)),
                  pl.BlockSpec((gb, 1, n), lambda i: (i, 0, 0))],
        out_specs=pl.BlockSpec((gb, h, n), lambda i: (i, 0, 0)),
        out_shape=jax.ShapeDtypeStruct((g, h, n), F32),
        compiler_params=_cparams("parallel"),
        name="s5_taps",
    )(bt, ca, d_pad)


def _s5_weights(a_re, a_im, log_dt, b_re, b_im, c_re, c_im, d_skip):
    g = a_re.shape[0]
    p, h, t = SSM_STATE, SSM_GROUP, CHUNK
    dt = jnp.exp(log_dt.astype(F32))[:, None]
    lr, li = a_re.astype(F32), a_im.astype(F32)
    mag = jnp.exp(lr * dt)
    abar_re, abar_im = mag * jnp.cos(li * dt), mag * jnp.sin(li * dt)
    den = lr * lr + li * li
    nr, ni = abar_re - 1.0, abar_im
    z_re = ((nr * lr + ni * li) / den)[:, None, :]
    z_im = ((ni * lr - nr * li) / den)[:, None, :]
    br, bi = b_re.astype(F32).transpose(0, 2, 1), b_im.astype(F32).transpose(0, 2, 1)
    bt_re = z_re * br - z_im * bi
    bt_im = z_re * bi + z_im * br
    cr, ci = c_re.astype(F32), c_im.astype(F32)

    tau = jnp.arange(t + 1, dtype=F32)[None, :, None]
    pmag = jnp.exp(tau * (lr * dt)[:, None, :])
    pw_re = pmag * jnp.cos(tau * (li * dt)[:, None, :])
    pw_im = pmag * jnp.sin(tau * (li * dt)[:, None, :])

    ca_re = cr[:, None] * pw_re[:, :, None, :] - ci[:, None] * pw_im[:, :, None, :]
    ca_im = cr[:, None] * pw_im[:, :, None, :] + ci[:, None] * pw_re[:, :, None, :]

    ca = jnp.concatenate([ca_re[:, :t], ca_im[:, :t]], axis=-1).reshape(g, t * h, 2 * p)
    bt = jnp.concatenate([bt_re, -bt_im], axis=-1)
    d_pad = jnp.pad(d_skip.astype(F32), ((0, 0), (0, (t - 1) * h)))[:, None, :]
    taps_t = _s5_taps(bt, ca, d_pad)

    gt = GROUPS_PER_TILE
    nt = g // gt
    idx = jnp.arange(t)

    k_cat = taps_t.reshape(nt, gt * h, t * h)

    pj_re, pj_im = pw_re[:, t - 1 - idx][:, :, None, :], pw_im[:, t - 1 - idx][:, :, None, :]
    win = jnp.concatenate([pj_re * bt_re[:, None] - pj_im * bt_im[:, None],
                           pj_re * bt_im[:, None] + pj_im * bt_re[:, None]], axis=-1)
    win_cat = win.astype(BF16).reshape(nt, gt, t, h, 2 * p).transpose(0, 2, 1, 3, 4)
    win_cat = win_cat.reshape(nt, t * gt * h, 2 * p)

    wo = jnp.stack([ca_re[:, 1:].reshape(g, t * h, p), -ca_im[:, 1:].reshape(g, t * h, p)], axis=1)
    wo_cat = wo.astype(BF16).transpose(0, 1, 3, 2).reshape(nt, gt, 2, p, t * h).transpose(0, 2, 1, 3, 4)
    wo_cat = wo_cat.reshape(nt, 2 * gt * p, t * h)

    kk = (t * jnp.arange(2 * SUBLANES, dtype=F32))[:, None, None]
    cmag = jnp.exp(kk * (lr * dt)[None])
    a_chunk_re = (cmag * jnp.cos(kk * (li * dt)[None])).reshape(2 * SUBLANES, g * p)
    a_chunk_im = (cmag * jnp.sin(kk * (li * dt)[None])).reshape(2 * SUBLANES, g * p)
    return k_cat.astype(BF16), win_cat, wo_cat, a_chunk_re, a_chunk_im


def _lane_repeat(blocks, width, copies):
    r = np.kron(np.eye(blocks, dtype=np.float32),
                np.kron(np.ones((1, copies), np.float32), np.eye(width, dtype=np.float32)))
    return jnp.asarray(r, BF16)


def _group_of(index, period, size):
    assert period & (period - 1) == 0 and size & (size - 1) == 0
    return lax.shift_right_logical(index & (period - 1), size.bit_length() - 1)


def _expand_block_diag(compact, rep, row_group, col_group):
    e = jnp.dot(compact, rep, preferred_element_type=F32)
    r = lax.broadcasted_iota(jnp.int32, e.shape, 0)
    c = lax.broadcasted_iota(jnp.int32, e.shape, 1)
    return jnp.where(row_group(r) == col_group(c), e, 0.0).astype(BF16)


def _chunk_lhs(u_ref):
    return jnp.concatenate([u_ref[j] for j in range(CHUNK)], axis=1)


def _s5_state_in_kernel(u_ref, wc_ref, rep_ref, sre_ref, sim_ref, w_scr):
    @pl.when(pl.program_id(1) == 0)
    def _():
        w_scr[...] = _expand_block_diag(
            wc_ref[0], rep_ref[...],
            lambda r: _group_of(r, LANES, SSM_GROUP),
            lambda c: _group_of(c, GROUPS_PER_TILE * SSM_STATE, SSM_STATE))

    s = jnp.dot(_chunk_lhs(u_ref), w_scr[...], preferred_element_type=F32)
    half = s.shape[1] // 2
    sre_ref[...] = s[:, :half]
    sim_ref[...] = s[:, half:]


def _s5_state_in(us3, win_cat, layer, *, rb):
    t, r, width = us3.shape
    nt = width // (2 * LANES)
    _, k, nc = win_cat.shape
    n = nc * GROUPS_PER_TILE
    rb = min(rb, r)
    rep = _lane_repeat(2, SSM_STATE, GROUPS_PER_TILE)
    return pl.pallas_call(
        _s5_state_in_kernel,
        grid=(nt, r // rb),
        in_specs=[pl.BlockSpec((t, rb, LANES), lambda q, i: (0, i, q)),
                  pl.BlockSpec((1, k, nc), lambda q, i: (layer * nt + q, 0, 0)),
                  pl.BlockSpec(rep.shape, lambda q, i: (0, 0))],
        out_specs=[pl.BlockSpec((rb, n // 2), lambda q, i: (i, q)),
                   pl.BlockSpec((rb, n // 2), lambda q, i: (i, q))],
        out_shape=[jax.ShapeDtypeStruct((r, nt * n // 2), F32)] * 2,
        scratch_shapes=[pltpu.VMEM((k, n), BF16)],
        compiler_params=_cparams("parallel", "arbitrary"),
        name="s5_state_in",
    )(us3, win_cat, rep)


def _s5_recurrence_kernel(sre_ref, sim_ref, pre_ref, pim_ref, hre_ref, him_ref, *, bsz, n_chunks):
    tn = sre_ref.shape[1]
    shape = (SUBLANES, tn)
    row = lax.broadcasted_iota(jnp.int32, shape, 0)
    pw_re = pre_ref[0:SUBLANES, :]
    pw_im = pim_ref[0:SUBLANES, :]
    a_tile_re = jnp.broadcast_to(pre_ref[SUBLANES:SUBLANES + 1, :], shape)
    a_tile_im = jnp.broadcast_to(pim_ref[SUBLANES:SUBLANES + 1, :], shape)

    def tile_step(it, carry):
        out = []
        for b in range(bsz):
            h_re, h_im = carry[2 * b], carry[2 * b + 1]
            rows = pl.ds(pl.multiple_of(b * n_chunks + it * SUBLANES, SUBLANES), SUBLANES)
            y_re, y_im = sre_ref[rows, :], sim_ref[rows, :]
            for d in (1, 2, 4):
                m_re = jnp.broadcast_to(pre_ref[d:d + 1, :], shape)
                m_im = jnp.broadcast_to(pim_ref[d:d + 1, :], shape)
                s_re = jnp.where(row >= d, pltpu.roll(y_re, d, axis=0), 0.0)
                s_im = jnp.where(row >= d, pltpu.roll(y_im, d, axis=0), 0.0)
                y_re, y_im = (y_re + m_re * s_re - m_im * s_im,
                              y_im + m_re * s_im + m_im * s_re)
            e_re = jnp.where(row >= 1, pltpu.roll(y_re, 1, axis=0), 0.0)
            e_im = jnp.where(row >= 1, pltpu.roll(y_im, 1, axis=0), 0.0)
            hre_ref[rows, :] = pw_re * h_re - pw_im * h_im + e_re
            him_ref[rows, :] = pw_re * h_im + pw_im * h_re + e_im
            last_re = jnp.broadcast_to(y_re[SUBLANES - 1:SUBLANES, :], shape)
            last_im = jnp.broadcast_to(y_im[SUBLANES - 1:SUBLANES, :], shape)
            out.append(a_tile_re * h_re - a_tile_im * h_im + last_re)
            out.append(a_tile_re * h_im + a_tile_im * h_re + last_im)
        return tuple(out)

    zero = jnp.zeros(shape, F32)
    lax.fori_loop(0, n_chunks // SUBLANES, tile_step, (zero,) * (2 * bsz))


def _s5_recurrence(s_re, s_im, pw_re, pw_im, layer, *, bsz, tn):
    r, n = s_re.shape
    tn = min(tn, n)
    spec = pl.BlockSpec((r, tn), lambda j: (0, j))
    pspec = pl.BlockSpec((pw_re.shape[0], tn), lambda j: (0, layer * (n // tn) + j))
    return pl.pallas_call(
        functools.partial(_s5_recurrence_kernel, bsz=bsz, n_chunks=r // bsz),
        grid=(n // tn,),
        in_specs=[spec, spec, pspec, pspec],
        out_specs=[spec, spec],
        out_shape=[jax.ShapeDtypeStruct((r, n), F32)] * 2,
        compiler_params=_cparams("parallel"),
        name="s5_recurrence",
    )(s_re, s_im, pw_re, pw_im)


def _s5_out_kernel(u_ref, kc_ref, hre_ref, him_ref, wc_ref, rep_ref, y_ref, t_scr, w_scr):
    @pl.when(pl.program_id(1) == 0)
    def _():
        rep = rep_ref[...]
        col_group = lambda c: _group_of(c, LANES, SSM_GROUP)
        taps = _expand_block_diag(kc_ref[0], rep, lambda r: _group_of(r, LANES, SSM_GROUP), col_group)
        t_scr[...] = jnp.zeros_like(t_scr)
        for j in range(CHUNK):
            t_scr[j * LANES:(j + 1) * LANES, j * LANES:] = taps[:, :(CHUNK - j) * LANES]
        w_scr[...] = _expand_block_diag(
            wc_ref[0], rep, lambda r: _group_of(r, GROUPS_PER_TILE * SSM_STATE, SSM_STATE), col_group)

    h = jnp.concatenate([hre_ref[...], him_ref[...]], axis=1).astype(BF16)
    lhs = _chunk_lhs(u_ref)
    band = CHUNK // S5_OUT_BANDS
    for c in range(S5_OUT_BANDS):
        cols = slice(c * band * LANES, (c + 1) * band * LANES)
        y = jnp.dot(lhs[:, :(c + 1) * band * LANES], t_scr[:(c + 1) * band * LANES, cols],
                    preferred_element_type=F32)
        y += jnp.dot(h, w_scr[:, cols], preferred_element_type=F32)
        for i in range(band):
            y_ref[c * band + i] = y[:, i * LANES:(i + 1) * LANES].astype(y_ref.dtype)


def _s5_out(us3, k_cat, h_re, h_im, wo_cat, layer, *, rb):
    t, r, width = us3.shape
    nt = width // (2 * LANES)
    _, ks, kc = wo_cat.shape
    k = t * LANES
    rb = min(rb, r)
    rep = _lane_repeat(t, SSM_GROUP, GROUPS_PER_TILE)
    return pl.pallas_call(
        _s5_out_kernel,
        grid=(nt, r // rb),
        in_specs=[pl.BlockSpec((t, rb, LANES), lambda q, i: (0, i, q)),
                  pl.BlockSpec((1, LANES, kc), lambda q, i: (layer * nt + q, 0, 0)),
                  pl.BlockSpec((rb, ks // 2), lambda q, i: (i, q)),
                  pl.BlockSpec((rb, ks // 2), lambda q, i: (i, q)),
                  pl.BlockSpec((1, ks, kc), lambda q, i: (layer * nt + q, 0, 0)),
                  pl.BlockSpec(rep.shape, lambda q, i: (0, 0))],
        out_specs=pl.BlockSpec((t, rb, LANES), lambda q, i: (0, i, q)),
        out_shape=jax.ShapeDtypeStruct((t, r, nt * LANES), BF16),
        scratch_shapes=[pltpu.VMEM((k, k), BF16), pltpu.VMEM((ks, k), BF16)],
        compiler_params=_cparams("parallel", "arbitrary"),
        name="s5_out",
    )(us3, k_cat, h_re, h_im, wo_cat, rep)


def _s5_glu_kernel(y_ref, gate_ref, w_ref, b_ref, pt_ref, o_ref):
    t, rb, w = y_ref.shape
    y = jax.nn.gelu(y_ref[...].reshape(t * rb, w).astype(F32))
    z = jnp.dot(y.astype(BF16), w_ref[...], preferred_element_type=F32) + b_ref[...]
    y = y * jax.nn.sigmoid(z)
    y = (y * jax.nn.silu(gate_ref[...].reshape(t * rb, w).astype(F32))).astype(BF16)
    o_ref[...] = jnp.dot(pt_ref[...], y, preferred_element_type=F32).astype(o_ref.dtype)


def _s5_glu(y3, us3, w_glu, b_glu, *, rb):
    t, r, w = y3.shape
    rb = min(rb, r)
    rows = t * rb
    return pl.pallas_call(
        _s5_glu_kernel,
        grid=(r // rb,),
        in_specs=[pl.BlockSpec((t, rb, w), lambda i: (0, i, 0)),
                  pl.BlockSpec((t, rb, w), lambda i: (0, i, 1)),
                  pl.BlockSpec((w, w), lambda i: (0, 0)),
                  pl.BlockSpec((1, w), lambda i: (0, 0)),
                  pl.BlockSpec((rows, rows), lambda i: (0, 0))],
        out_specs=pl.BlockSpec((rows, w), lambda i: (i, 0)),
        out_shape=jax.ShapeDtypeStruct((t * r, w), BF16),
        compiler_params=_cparams("parallel"),
        name="s5_glu",
    )(y3, us3, w_glu, b_glu.reshape(1, w), _chunk_major_perm(rows).T)


def _out_proj_kernel(ys_ref, yd_ref, yx_ref, w_ref, x_ref, g_ref, o_ref):
    ws, wd = ys_ref.shape[1], yd_ref.shape[1]
    mix = jnp.dot(ys_ref[...], w_ref[0:ws, :], preferred_element_type=F32)
    mix += jnp.dot(yd_ref[...], w_ref[ws:ws + wd, :], preferred_element_type=F32)
    mix += jnp.dot(yx_ref[...], w_ref[ws + wd:, :], preferred_element_type=F32)
    var = jnp.mean(mix * mix, axis=-1, keepdims=True)
    o_ref[...] = x_ref[...] + mix * lax.rsqrt(var + NORM_EPS) * g_ref[...]


def _out_proj(y_s, y_d, y_x, w_out, layer, x, g_post, *, tm):
    m, d = x.shape
    tm = min(tm, m)
    row = lambda i: (i, 0)
    const = lambda i: (0, 0)
    return pl.pallas_call(
        _out_proj_kernel,
        grid=(m // tm,),
        in_specs=[pl.BlockSpec((tm, y_s.shape[1]), row),
                  pl.BlockSpec((tm, y_d.shape[1]), row),
                  pl.BlockSpec((tm, y_x.shape[1]), row),
                  pl.BlockSpec((None,) + w_out.shape[1:], lambda i: (layer, 0, 0)),
                  pl.BlockSpec((tm, d), row),
                  pl.BlockSpec((1, d), const)],
        out_specs=pl.BlockSpec((tm, d), row),
        out_shape=jax.ShapeDtypeStruct((m, d), F32),
        compiler_params=_cparams("parallel"),
        name="out_proj",
    )(y_s, y_d, y_x, w_out, x, g_post.reshape(1, d))


def kernel(x, mem, positions, norm_pre, norm_post, norm_mem, w_in, w_out, w_mem_kv, ssm_a_re, ssm_a_im, ssm_log_dt, ssm_b_re, ssm_b_im, ssm_c_re, ssm_c_im, ssm_d, w_glu, b_glu, diff_lq1, diff_lk1, diff_lq2, diff_lk2, diff_subln):
    bsz, seq, d_model = x.shape
    mem_tokens = mem.shape[1]
    depth = w_in.shape[0]
    m = bsz * seq
    ssm_width = ssm_a_re.shape[1] * SSM_GROUP
    n_groups = ssm_width // SSM_GROUP
    diff_width = DIFF_HEADS * 2 * DIFF_HEAD_DIM
    xattn_width = XATTN_HEADS * XATTN_HEAD_DIM
    n_chunks = seq // CHUNK
    n_s5_cols = 2 * ssm_width
    col_qd = 0
    col_kd = col_qd + diff_width
    col_vd = col_kd + diff_width
    col_gd = col_vd + diff_width
    col_qx = col_gd + diff_width
    col_gx = col_qx + xattn_width
    assert col_kd == col_qd + diff_width and col_qd % diff_width == 0

    inv = ROPE_THETA ** (-jnp.arange(0, DIFF_HEAD_DIM, 2, dtype=F32) / DIFF_HEAD_DIM)
    half = DIFF_HEAD_DIM // 2
    ang = positions.astype(F32).reshape(m, 1) * jnp.tile(inv, LANES // half)
    cos_t = jnp.cos(ang)
    sin_t = jnp.sin(ang) * jnp.tile(jnp.concatenate([-jnp.ones(half, F32), jnp.ones(half, F32)]),
                                    LANES // DIFF_HEAD_DIM)

    xf = x.reshape(m, d_model)
    mem_f = mem.reshape(bsz * mem_tokens, d_model)
    w_in_bf, w_out_bf = w_in.astype(BF16), w_out.astype(BF16)
    all_groups = lambda a: a.reshape((depth * n_groups,) + a.shape[2:])
    k_cat, win_cat, wo_cat, a_re, a_im = _s5_weights(
        *(all_groups(a) for a in (ssm_a_re, ssm_a_im, ssm_log_dt, ssm_b_re, ssm_b_im,
                                  ssm_c_re, ssm_c_im, ssm_d)))
    for l in range(depth):
        lambda_init = 0.8 - 0.6 * math.exp(-0.3 * l)
        us3, proj = _in_proj(xf, norm_pre[l], w_in_bf, l, n_perm_cols=n_s5_cols, tm=1024, tn=1024,
                             perm_rows=512)

        s_re, s_im = _s5_state_in(us3, win_cat, l, rb=512)
        h_re, h_im = _s5_recurrence(s_re, s_im, a_re, a_im, l, bsz=bsz, tn=1024)
        y3 = _s5_out(us3, k_cat, h_re, h_im, wo_cat, l, rb=512)
        y_s = _s5_glu(y3, us3, w_glu[l].astype(BF16), b_glu[l], rb=64)

        assert col_vd == col_kd + diff_width
        q_rot, k_rot, vt_ext = _rope(proj, cos_t, sin_t, bsz=bsz, seq=seq,
                                     q_col_block=col_qd // diff_width, tm=1024)
        lam = (jnp.exp(jnp.sum(diff_lq1[l].astype(F32) * diff_lk1[l].astype(F32)))
               - jnp.exp(jnp.sum(diff_lq2[l].astype(F32) * diff_lk2[l].astype(F32))) + lambda_init)
        sg_vec = (diff_subln[l].astype(F32) * (1.0 - lambda_init)).reshape(1, LANES)
        y_d = _diff_attn(q_rot, k_rot, vt_ext, proj, lam.reshape(1), sg_vec, bsz=bsz, seq=seq,
                         g_col=col_gd // LANES, tq=1024)

        mem_kv = _norm_matmul(mem_f, norm_mem[l], w_mem_kv[l].astype(BF16), tm=512, tn=1024, name="mem_kv")
        y_x = _mem_attn(proj, mem_kv, bsz=bsz, seq=seq, mem_tokens=mem_tokens,
                        q_col_block=col_qx // xattn_width, g_col_block=col_gx // xattn_width, tq=1024)

        xf = _out_proj(y_s, y_d, y_x, w_out_bf, l, xf, norm_post[l], tm=512)
    return xf.reshape(bsz, seq, d_model)
```
